```python
import math
import jax, jax.numpy as jnp
from jax import lax
import numpy as np

D_MODEL = 1024
BATCH = 8
SEQ = 2048
DEPTH = 1
DEC_BATCH = 128
DEC_SEQ = 1
PAST_LEN = 16384
PAGE_SIZE = 128

MIX_WIDTH = D_MODEL
A_WIDTH = MIX_WIDTH // 2
B_WIDTH = MIX_WIDTH - A_WIDTH
A_GROUPS = 4
A_DIM = A_WIDTH // A_GROUPS
CHUNK = 128
B_HEADS = 4
B_DK = B_WIDTH // B_HEADS
B_DV = B_WIDTH // B_HEADS
D_FF = 4 * D_MODEL
IN_COLS = 2 * A_WIDTH + 4 * B_WIDTH
ROPE_BASE = 10000.0
EPS = 1e-6

kernel_name = "hybrid_gmlp_retention_decoder_step"


def rms_norm(x, w):
    xf = x.astype(jnp.float32)
    y = xf * lax.rsqrt(jnp.mean(xf * xf, axis=-1, keepdims=True) + EPS)
    return (y * w.astype(jnp.float32)).astype(x.dtype)


def layer_norm(x, w, b):
    xf = x.astype(jnp.float32)
    mu = jnp.mean(xf, axis=-1, keepdims=True)
    xc = xf - mu
    y = xc * lax.rsqrt(jnp.mean(xc * xc, axis=-1, keepdims=True) + EPS)
    return (y * w.astype(jnp.float32) + b.astype(jnp.float32)).astype(x.dtype)


def head_norm(o, w):
    mu = jnp.mean(o, axis=-1, keepdims=True)
    oc = o - mu
    return oc * lax.rsqrt(jnp.mean(oc * oc, axis=-1, keepdims=True) + EPS) * w.astype(jnp.float32)


def rope(x, pos):
    half = x.shape[-1] // 2
    inv = ROPE_BASE ** (-jnp.arange(half, dtype=jnp.float32) / half)
    ang = pos.astype(jnp.float32)[:, None] * inv[None, :]
    cos = jnp.cos(ang)[:, None, :]
    sin = jnp.sin(ang)[:, None, :]
    xf = x.astype(jnp.float32)
    x1, x2 = xf[..., :half], xf[..., half:]
    return jnp.concatenate([x1 * cos - x2 * sin, x1 * sin + x2 * cos], axis=-1)


def retention_log_gamma():
    return jnp.log(1.0 - 2.0 ** (-5.0 - jnp.arange(B_HEADS, dtype=jnp.float32)))


def retention_chunkwise(q, k, v, s0):
    B, L, H, DK = q.shape
    DV = v.shape[-1]
    C = min(CHUNK, L)
    n = L // C
    lg = retention_log_gamma()
    idx = jnp.arange(C, dtype=jnp.float32)
    diff = idx[:, None] - idx[None, :]
    decay_mask = jnp.where(diff[None] >= 0,
                           jnp.exp(lg[:, None, None] * jnp.maximum(diff, 0.0)[None]), 0.0)
    q_decay = jnp.exp(lg[:, None] * (idx[None, :] + 1.0))
    k_decay = jnp.exp(lg[:, None] * (C - 1.0 - idx[None, :]))
    chunk_decay = jnp.exp(lg * C)

    def to_chunks(t):
        return t.reshape(B, n, C, H, t.shape[-1]).transpose(1, 0, 3, 2, 4)

    qc, kc, vc = to_chunks(q), to_chunks(k), to_chunks(v)

    def step(s, inp):
        qi, ki, vi = inp
        scores = jnp.einsum('bhid,bhjd->bhij', qi, ki) * decay_mask
        inner = jnp.einsum('bhij,bhjv->bhiv', scores, vi)
        cross = jnp.einsum('bhid,bhdv->bhiv', qi * q_decay[None, :, :, None], s)
        s_new = s * chunk_decay[None, :, None, None] + jnp.einsum(
            'bhjd,bhjv->bhdv', ki * k_decay[None, :, :, None], vi)
        return s_new, inner + cross

    s_final, out = lax.scan(step, s0, (qc, kc, vc))
    out = out.transpose(1, 0, 3, 2, 4).reshape(B, L, H, DV)
    return out, s_final


def chunk_spatial_gating(u, v, w_s, b_s):
    B, L, G, D = v.shape
    C = min(CHUNK, L)
    n = L // C
    mask = jnp.tril(jnp.ones((C, C), dtype=bool))
    w = jnp.where(mask[None], w_s[:, :C, :C], 0.0).astype(v.dtype)
    vc = v.reshape(B, n, C, G, D)
    mixed = jnp.einsum('gts,bnsgd->bntgd', w, vc) + b_s[:, :C].T[None, None, :, :, None]
    return u * mixed.reshape(B, L, G, D)


def hybrid_layer(x, pos, s0, pre_mix_w, w_in, ln_v_w, ln_v_b, w_s, b_s, gn_w,
                 w_out, post_mix_w, pre_mlp_w, w_up, w_down, post_mlp_w):
    B, L, _ = x.shape
    h = rms_norm(x, pre_mix_w)
    z = h @ w_in
    splits = [A_WIDTH, 2 * A_WIDTH, 2 * A_WIDTH + B_WIDTH,
              2 * A_WIDTH + 2 * B_WIDTH, 2 * A_WIDTH + 3 * B_WIDTH]
    ua, va, qb, kb, vb, gb = jnp.split(z, splits, axis=-1)
    ua = jax.nn.gelu(ua).reshape(B, L, A_GROUPS, A_DIM)
    va = layer_norm(jax.nn.gelu(va).reshape(B, L, A_GROUPS, A_DIM), ln_v_w, ln_v_b)
    out_a = chunk_spatial_gating(ua, va, w_s, b_s).reshape(B, L, A_WIDTH)
    q = rope(qb.reshape(B, L, B_HEADS, B_DK), pos)
    k = rope(kb.reshape(B, L, B_HEADS, B_DK), pos) * (B_DK ** -0.5)
    v = vb.reshape(B, L, B_HEADS, B_DV).astype(jnp.float32)
    o, s_new = retention_chunkwise(q, k, v, s0.astype(jnp.float32))
    o = head_norm(o, gn_w).reshape(B, L, B_WIDTH).astype(x.dtype) * jax.nn.silu(gb)
    mix = jnp.concatenate([out_a, o], axis=-1) @ w_out
    x = x + rms_norm(mix, post_mix_w)
    f = jnp.square(jax.nn.relu(rms_norm(x, pre_mlp_w) @ w_up)) @ w_down
    x = x + rms_norm(f, post_mlp_w)
    return x, s_new, va


def setup_inputs(seed: int = 0) -> dict:
    key = jax.random.key(seed)
    ks = jax.random.split(key, 20)
    f32 = jnp.float32

    def nrm(k, shape, scale):
        return jax.random.normal(k, shape, f32) * scale

    return {
        "x_prompt": nrm(ks[0], (BATCH, SEQ, D_MODEL), 1.0),
        "x_sample": nrm(ks[1], (DEC_BATCH, DEC_SEQ, D_MODEL), 1.0),
        "state_ret": nrm(ks[2], (DEPTH, DEC_BATCH, B_HEADS, B_DK, B_DV), 0.5),
        "pre_mix_w": 1.0 + nrm(ks[3], (DEPTH, D_MODEL), 0.02),
        "w_in": nrm(ks[4], (DEPTH, D_MODEL, IN_COLS), D_MODEL ** -0.5),
        "ln_v_w": 1.0 + nrm(ks[5], (DEPTH, A_GROUPS, A_DIM), 0.02),
        "ln_v_b": nrm(ks[6], (DEPTH, A_GROUPS, A_DIM), 0.02),
        "w_s": nrm(ks[7], (DEPTH, A_GROUPS, CHUNK, CHUNK), CHUNK ** -0.5),
        "b_s": 1.0 + nrm(ks[8], (DEPTH, A_GROUPS, CHUNK), 0.02),
        "gn_w": 1.0 + nrm(ks[9], (DEPTH, B_HEADS, B_DV), 0.02),
        "w_out": nrm(ks[10], (DEPTH, MIX_WIDTH, D_MODEL), MIX_WIDTH ** -0.5),
        "post_mix_w": 1.0 + nrm(ks[11], (DEPTH, D_MODEL), 0.02),
        "pre_mlp_w": 1.0 + nrm(ks[12], (DEPTH, D_MODEL), 0.02),
        "w_up": nrm(ks[13], (DEPTH, D_MODEL, D_FF), D_MODEL ** -0.5),
        "w_down": nrm(ks[14], (DEPTH, D_FF, D_MODEL), D_FF ** -0.5),
        "post_mlp_w": 1.0 + nrm(ks[15], (DEPTH, D_MODEL), 0.02),
    }


def reference(x_prompt, x_sample, state_ret, pre_mix_w, w_in, ln_v_w, ln_v_b, w_s, b_s,
              gn_w, w_out, post_mix_w, pre_mlp_w, w_up, w_down, post_mlp_w):
    pos_prompt = jnp.arange(SEQ, dtype=jnp.int32)
    pos_sample = PAST_LEN + jnp.arange(DEC_SEQ, dtype=jnp.int32)
    yp, ys = x_prompt, x_sample
    sp_list, ss_list, vs_list = [], [], []
    for l in range(DEPTH):
        params = (pre_mix_w[l], w_in[l], ln_v_w[l], ln_v_b[l], w_s[l], b_s[l], gn_w[l],
                  w_out[l], post_mix_w[l], pre_mlp_w[l], w_up[l], w_down[l], post_mlp_w[l])
        s0_prompt = jnp.zeros((BATCH, B_HEADS, B_DK, B_DV), jnp.float32)
        yp, sp, _ = hybrid_layer(yp, pos_prompt, s0_prompt, *params)
        ys, ss, vs = hybrid_layer(ys, pos_sample, state_ret[l], *params)
        sp_list.append(sp.astype(state_ret.dtype))
        ss_list.append(ss.astype(state_ret.dtype))
        vs_list.append(vs)
    state_ret_prompt = jnp.stack(sp_list, axis=0)
    state_ret_sample = jnp.stack(ss_list, axis=0)
    state_v_sample = jnp.stack(vs_list, axis=0)
    return (yp, ys, state_ret_prompt, state_ret_sample, state_v_sample)
```

```python
import functools
import math

import jax
import jax.numpy as jnp
from jax import lax
from jax.experimental import pallas as pl
from jax.experimental.pallas import tpu as pltpu

D_MODEL = 1024
A_WIDTH = 512
B_WIDTH = 512
A_GROUPS = 4
A_DIM = 128
CHUNK = 128
B_HEADS = 4
B_DK = 128
B_DV = 128
D_FF = 4096
IN_COLS = 2 * A_WIDTH + 4 * B_WIDTH
ROPE_BASE = 10000.0
EPS = 1e-6
PAST_LEN = 16384

ROW_TILE = 256
FF_TILE = 1024
STATE_ROWS = 16
VMEM_LIMIT_BYTES = 56 * 1024 * 1024

F32 = jnp.float32
BF16 = jnp.bfloat16


def _rms(x, w):
    return x * lax.rsqrt(jnp.mean(x * x, axis=-1, keepdims=True) + EPS) * w


def _center_norm(x):
    mu = jnp.mean(x, axis=-1, keepdims=True)
    xc = x - mu
    return xc * lax.rsqrt(jnp.mean(xc * xc, axis=-1, keepdims=True) + EPS)


def _gelu(x):
    return jax.nn.gelu(x, approximate=True)


def _silu(x):
    return x * (1.0 / (1.0 + jnp.exp(-x)))


def _rope(x, cosf, sinf):
    return x * cosf + pltpu.roll(x, B_DK // 2, axis=1) * sinf


def _dot(a, b):
    return jnp.dot(a, b, preferred_element_type=F32)


def _dot_nt(a, b):
    return lax.dot_general(a, b, (((1,), (1,)), ((), ())), preferred_element_type=F32)


def _dot_tn(a, b):
    return lax.dot_general(a, b, (((0,), (0,)), ((), ())), preferred_element_type=F32)


def _mlp(x1, pre_mlp_w, w_up_ref, w_down_ref, post_mlp_w):
    hn = _rms(x1, pre_mlp_w).astype(BF16)
    f = None
    for j in range(D_FF // FF_TILE):
        cols = slice(j * FF_TILE, (j + 1) * FF_TILE)
        u = jnp.maximum(_dot(hn, w_up_ref[:, cols]), 0.0)
        part = _dot((u * u).astype(BF16), w_down_ref[cols, :])
        f = part if f is None else f + part
    return x1 + _rms(f, post_mlp_w)


def _prompt_kernel(x_ref, pre_mix_ref, w_in_ref, lnw_ref, lnb_ref, ws_ref, bs_ref, gn_ref,
                   w_out_ref, post_mix_ref, pre_mlp_ref, w_up_ref, w_down_ref, post_mlp_ref,
                   cos_ref, sin_ref, mask_ref, qdec_ref, kdec_ref, cdec_ref,
                   y_ref, s_ref, mix_scr):
    t = pl.program_id(1)

    @pl.when(t == 0)
    def _():
        s_ref[...] = jnp.zeros_like(s_ref)

    x = x_ref[...]
    h = _rms(x, pre_mix_ref[...]).astype(BF16)

    za = _dot(h, w_in_ref[:, :2 * A_WIDTH])
    row = lax.broadcasted_iota(jnp.int32, (CHUNK, CHUNK), 0)
    col = lax.broadcasted_iota(jnp.int32, (CHUNK, CHUNK), 1)
    for g in range(A_GROUPS):
        gc = slice(g * A_DIM, (g + 1) * A_DIM)
        ua = _gelu(za[:, g * A_DIM:(g + 1) * A_DIM])
        va = _gelu(za[:, A_WIDTH + g * A_DIM:A_WIDTH + (g + 1) * A_DIM])
        va = (_center_norm(va) * lnw_ref[:, gc] + lnb_ref[:, gc]).astype(BF16)
        w_tril = jnp.where(row >= col, ws_ref[g], 0.0).astype(BF16)
        bias = jnp.broadcast_to(bs_ref[:, g:g + 1], (CHUNK, A_DIM))
        for c in range(ROW_TILE // CHUNK):
            rows = slice(c * CHUNK, (c + 1) * CHUNK)
            mixed = _dot(w_tril, va[rows]) + bias
            mix_scr[rows, gc] = (ua[rows] * mixed).astype(BF16)

    zb = _dot(h, w_in_ref[:, 2 * A_WIDTH:])
    cosf = cos_ref[...]
    sinf = sin_ref[...]
    for hh in range(B_HEADS):
        hc = slice(hh * B_DK, (hh + 1) * B_DK)
        q = _rope(zb[:, hh * B_DK:(hh + 1) * B_DK], cosf, sinf)
        k = _rope(zb[:, B_WIDTH + hh * B_DK:B_WIDTH + (hh + 1) * B_DK], cosf, sinf) * (B_DK ** -0.5)
        v = zb[:, 2 * B_WIDTH + hh * B_DV:2 * B_WIDTH + (hh + 1) * B_DV].astype(BF16)
        gate = _silu(zb[:, 3 * B_WIDTH + hh * B_DV:3 * B_WIDTH + (hh + 1) * B_DV])
        mask = mask_ref[hh]
        qdec = qdec_ref[hh]
        kdec = kdec_ref[hh]
        cdec = cdec_ref[hh]
        gnw = gn_ref[:, hc]
        for c in range(ROW_TILE // CHUNK):
            rows = slice(c * CHUNK, (c + 1) * CHUNK)
            qc = q[rows]
            kc = k[rows]
            vc = v[rows]
            s = s_ref[hh]
            scores = _dot_nt(qc.astype(BF16), kc.astype(BF16)) * mask
            inner = _dot(scores.astype(BF16), vc)
            cross = _dot((qc * qdec).astype(BF16), s.astype(BF16))
            s_ref[hh] = s * cdec + _dot_tn((kc * kdec).astype(BF16), vc)
            o = _center_norm(inner + cross) * gnw
            mix_scr[rows, A_WIDTH + hh * B_DV:A_WIDTH + (hh + 1) * B_DV] = (o * gate[rows]).astype(BF16)

    mix = _dot(mix_scr[...], w_out_ref[...])
    x1 = x + _rms(mix, post_mix_ref[...])
    y_ref[...] = _mlp(x1, pre_mlp_ref[...], w_up_ref, w_down_ref, post_mlp_ref[...])


def _resident(shape):
    zeros = (0,) * len(shape)
    return pl.BlockSpec(shape, lambda b, t: zeros, pipeline_mode=pl.Buffered(1))


def _prompt_layer(x, p, consts):
    batch, seq, _ = x.shape
    grid = (batch, seq // ROW_TILE)
    row_spec = pl.BlockSpec((None, ROW_TILE, D_MODEL), lambda b, t: (b, t, 0))
    tab_spec = pl.BlockSpec((ROW_TILE, B_DK), lambda b, t: (t, 0))
    in_specs = [
        row_spec,
        _resident((1, D_MODEL)),
        _resident((D_MODEL, IN_COLS)),
        _resident((1, A_WIDTH)),
        _resident((1, A_WIDTH)),
        _resident((A_GROUPS, CHUNK, CHUNK)),
        _resident((CHUNK, A_GROUPS)),
        _resident((1, B_WIDTH)),
        _resident((D_MODEL, D_MODEL)),
        _resident((1, D_MODEL)),
        _resident((1, D_MODEL)),
        _resident((D_MODEL, D_FF)),
        _resident((D_FF, D_MODEL)),
        _resident((1, D_MODEL)),
        tab_spec,
        tab_spec,
        _resident((B_HEADS, CHUNK, CHUNK)),
        _resident((B_HEADS, CHUNK, B_DK)),
        _resident((B_HEADS, CHUNK, B_DK)),
        _resident((B_HEADS, 1, B_DV)),
    ]
    out_specs = [
        row_spec,
        pl.BlockSpec((None, B_HEADS, B_DK, B_DV), lambda b, t: (b, 0, 0, 0)),
    ]
    return pl.pallas_call(
        _prompt_kernel,
        grid=grid,
        in_specs=in_specs,
        out_specs=out_specs,
        out_shape=[
            jax.ShapeDtypeStruct(x.shape, F32),
            jax.ShapeDtypeStruct((batch, B_HEADS, B_DK, B_DV), F32),
        ],
        scratch_shapes=[pltpu.VMEM((ROW_TILE, D_MODEL), BF16)],
        compiler_params=pltpu.CompilerParams(
            dimension_semantics=("arbitrary", "arbitrary"),
            vmem_limit_bytes=VMEM_LIMIT_BYTES),
        name="prompt_layer",
    )(x, p["pre_mix_w"], p["w_in"], p["ln_v_w"], p["ln_v_b"], p["w_s"], p["b_s_t"], p["gn_w"],
      p["w_out"], p["post_mix_w"], p["pre_mlp_w"], p["w_up"], p["w_down"], p["post_mlp_w"],
      consts["cos"], consts["sin"], consts["mask"], consts["qdec"], consts["kdec"], consts["cdec"])


def _decode_in_kernel(x_ref, pre_mix_ref, w_in_ref, lnw_ref, lnb_ref, ws0_ref, bs0_ref,
                      cos_ref, sin_ref,
                      outa_ref, va_ref, q_ref, k_ref, v_ref, gate_ref):
    h = _rms(x_ref[...], pre_mix_ref[...]).astype(BF16)
    z = _dot(h, w_in_ref[...])
    cosf = cos_ref[...]
    sinf = sin_ref[...]
    for g in range(A_GROUPS):
        gc = slice(g * A_DIM, (g + 1) * A_DIM)
        ua = _gelu(z[:, g * A_DIM:(g + 1) * A_DIM])
        va = _gelu(z[:, A_WIDTH + g * A_DIM:A_WIDTH + (g + 1) * A_DIM])
        va = _center_norm(va) * lnw_ref[:, gc] + lnb_ref[:, gc]
        va_ref[:, gc] = va
        outa_ref[:, gc] = ua * (ws0_ref[:, gc] * va + bs0_ref[:, gc])
    for hh in range(B_HEADS):
        hc = slice(hh * B_DK, (hh + 1) * B_DK)
        base = 2 * A_WIDTH
        q_ref[:, hc] = _rope(z[:, base + hh * B_DK:base + (hh + 1) * B_DK], cosf, sinf)
        k_ref[:, hc] = _rope(z[:, base + B_WIDTH + hh * B_DK:base + B_WIDTH + (hh + 1) * B_DK],
                             cosf, sinf) * (B_DK ** -0.5)
    v_ref[...] = z[:, 2 * A_WIDTH + 2 * B_WIDTH:2 * A_WIDTH + 3 * B_WIDTH]
    gate_ref[...] = _silu(z[:, 2 * A_WIDTH + 3 * B_WIDTH:])


def _decode_state_kernel(q_ref, k_ref, v_ref, s_ref, gam_ref, o_ref, s_out_ref):
    rid = lax.broadcasted_iota(jnp.int32, (STATE_ROWS, B_DK), 0)
    for hh in range(B_HEADS):
        hc = slice(hh * B_DK, (hh + 1) * B_DK)
        q = q_ref[:, hc]
        k = k_ref[:, hc]
        v = v_ref[:, hc]
        gam = gam_ref[hh]
        qk = jnp.sum(q * k, axis=-1, keepdims=True)
        vb = v.astype(BF16)
        qg = (q * gam).astype(BF16)
        cross_rows = []
        for j in range(STATE_ROWS):
            s = s_ref[j, hh]
            cross_rows.append(_dot(qg, s.astype(BF16))[j:j + 1, :])
            kj = jnp.where(rid == j, k, 0.0).astype(BF16)
            s_out_ref[j, hh] = s * gam + _dot_tn(kj, vb)
        o_ref[:, hc] = qk * v + jnp.concatenate(cross_rows, axis=0)


def _decode_out_kernel(x_ref, outa_ref, o_ref, gate_ref, gn_ref, w_out_ref, post_mix_ref,
                       pre_mlp_ref, w_up_ref, w_down_ref, post_mlp_ref, y_ref):
    parts = [outa_ref[...].astype(BF16)]
    for hh in range(B_HEADS):
        hc = slice(hh * B_DV, (hh + 1) * B_DV)
        o = _center_norm(o_ref[:, hc]) * gn_ref[:, hc]
        parts.append((o * gate_ref[:, hc]).astype(BF16))
    mix = _dot(jnp.concatenate(parts, axis=-1), w_out_ref[...])
    x = x_ref[...]
    x1 = x + _rms(mix, post_mix_ref[...])
    y_ref[...] = _mlp(x1, pre_mlp_ref[...], w_up_ref, w_down_ref, post_mlp_ref[...])


def _full(shape):
    zeros = (0,) * len(shape)
    return pl.BlockSpec(shape, lambda *_: zeros)


def _decode_layer(x, s0, p, consts):
    n = x.shape[0]
    act = jax.ShapeDtypeStruct((n, A_WIDTH), F32)
    outa, va, q, k, v, gate = pl.pallas_call(
        _decode_in_kernel,
        grid=(1,),
        in_specs=[_full((n, D_MODEL)), _full((1, D_MODEL)), _full((D_MODEL, IN_COLS)),
                  _full((1, A_WIDTH)), _full((1, A_WIDTH)), _full((1, A_WIDTH)), _full((1, A_WIDTH)),
                  _full((1, B_DK)), _full((1, B_DK))],
        out_specs=[_full((n, A_WIDTH))] * 6,
        out_shape=[act] * 6,
        compiler_params=pltpu.CompilerParams(vmem_limit_bytes=VMEM_LIMIT_BYTES),
        name="decode_in",
    )(x, p["pre_mix_w"], p["w_in"], p["ln_v_w"], p["ln_v_b"], p["w_s0"], p["b_s0"],
      consts["cos1"], consts["sin1"])

    row_spec = pl.BlockSpec((STATE_ROWS, B_WIDTH), lambda i: (i, 0))
    st_spec = pl.BlockSpec((STATE_ROWS, B_HEADS, B_DK, B_DV), lambda i: (i, 0, 0, 0))
    o, s_new = pl.pallas_call(
        _decode_state_kernel,
        grid=(n // STATE_ROWS,),
        in_specs=[row_spec, row_spec, row_spec, st_spec, _full((B_HEADS, 1, B_DV))],
        out_specs=[row_spec, st_spec],
        out_shape=[act, jax.ShapeDtypeStruct(s0.shape, F32)],
        compiler_params=pltpu.CompilerParams(
            dimension_semantics=("arbitrary",), vmem_limit_bytes=VMEM_LIMIT_BYTES),
        name="decode_state",
    )(q, k, v, s0, consts["gam1"])

    y = pl.pallas_call(
        _decode_out_kernel,
        grid=(1,),
        in_specs=[_full((n, D_MODEL)), _full((n, A_WIDTH)), _full((n, B_WIDTH)), _full((n, B_WIDTH)),
                  _full((1, B_WIDTH)), _full((D_MODEL, D_MODEL)), _full((1, D_MODEL)),
                  _full((1, D_MODEL)), _full((D_MODEL, D_FF)), _full((D_FF, D_MODEL)),
                  _full((1, D_MODEL))],
        out_specs=_full((n, D_MODEL)),
        out_shape=jax.ShapeDtypeStruct((n, D_MODEL), F32),
        compiler_params=pltpu.CompilerParams(vmem_limit_bytes=VMEM_LIMIT_BYTES),
        name="decode_out",
    )(x, outa, o, gate, p["gn_w"], p["w_out"], p["post_mix_w"], p["pre_mlp_w"],
      p["w_up"], p["w_down"], p["post_mlp_w"])
    return y, s_new, va


def _rope_tables(pos):
    half = B_DK // 2
    inv = ROPE_BASE ** (-jnp.arange(half, dtype=F32) / half)
    ang = pos.astype(F32)[:, None] * inv[None, :]
    cos, sin = jnp.cos(ang), jnp.sin(ang)
    return jnp.concatenate([cos, cos], axis=-1), jnp.concatenate([-sin, sin], axis=-1)


def _decay_tables(c):
    lg = jnp.log(1.0 - 2.0 ** (-5.0 - jnp.arange(B_HEADS, dtype=F32)))
    idx = jnp.arange(c, dtype=F32)
    diff = idx[:, None] - idx[None, :]
    mask = jnp.where(diff[None] >= 0, jnp.exp(lg[:, None, None] * jnp.maximum(diff, 0.0)[None]), 0.0)
    qdec = jnp.exp(lg[:, None] * (idx[None, :] + 1.0))
    kdec = jnp.exp(lg[:, None] * (c - 1.0 - idx[None, :]))
    cdec = jnp.exp(lg * c)
    return mask, qdec, kdec, cdec


def kernel(x_prompt, x_sample, state_ret, pre_mix_w, w_in, ln_v_w, ln_v_b, w_s, b_s, gn_w,
           w_out, post_mix_w, pre_mlp_w, w_up, w_down, post_mlp_w):
    depth = w_in.shape[0]
    seq = x_prompt.shape[1]
    n_dec, dec_seq, _ = x_sample.shape
    assert dec_seq == 1 and seq % ROW_TILE == 0 and n_dec % STATE_ROWS == 0

    cos, sin = _rope_tables(jnp.arange(seq, dtype=jnp.int32))
    mask, qdec, kdec, cdec = _decay_tables(CHUNK)
    cos1, sin1 = _rope_tables(PAST_LEN + jnp.arange(1, dtype=jnp.int32))
    _, qdec1, _, _ = _decay_tables(1)
    consts = {
        "cos": cos, "sin": sin, "mask": mask,
        "qdec": jnp.broadcast_to(qdec[:, :, None], (B_HEADS, CHUNK, B_DK)),
        "kdec": jnp.broadcast_to(kdec[:, :, None], (B_HEADS, CHUNK, B_DK)),
        "cdec": jnp.broadcast_to(cdec[:, None, None], (B_HEADS, 1, B_DV)),
        "cos1": cos1, "sin1": sin1,
        "gam1": jnp.broadcast_to(qdec1[:, :, None], (B_HEADS, 1, B_DV)),
    }

    yp, ys = x_prompt, x_sample.reshape(n_dec, D_MODEL)
    sp_list, ss_list, vs_list = [], [], []
    for l in range(depth):
        p = {
            "pre_mix_w": pre_mix_w[l][None, :],
            "w_in": w_in[l].astype(BF16),
            "ln_v_w": ln_v_w[l].reshape(1, A_WIDTH),
            "ln_v_b": ln_v_b[l].reshape(1, A_WIDTH),
            "w_s": w_s[l],
            "b_s_t": b_s[l].T,
            "w_s0": jnp.repeat(w_s[l][:, 0, 0], A_DIM)[None, :],
            "b_s0": jnp.repeat(b_s[l][:, 0], A_DIM)[None, :],
            "gn_w": gn_w[l].reshape(1, B_WIDTH),
            "w_out": w_out[l].astype(BF16),
            "post_mix_w": post_mix_w[l][None, :],
            "pre_mlp_w": pre_mlp_w[l][None, :],
            "w_up": w_up[l].astype(BF16),
            "w_down": w_down[l].astype(BF16),
            "post_mlp_w": post_mlp_w[l][None, :],
        }
        yp, sp = _prompt_layer(yp, p, consts)
        ys, ss, vs = _decode_layer(ys, state_ret[l], p, consts)
        sp_list.append(sp)
        ss_list.append(ss)
        vs_list.append(vs.reshape(n_dec, 1, A_GROUPS, A_DIM))
    return (yp, ys.reshape(n_dec, 1, D_MODEL), jnp.stack(sp_list, axis=0),
            jnp.stack(ss_list, axis=0), jnp.stack(vs_list, axis=0))
```

```python
import functools
import math

import jax
import jax.numpy as jnp
from jax import lax
from jax.experimental import pallas as pl
from jax.experimental.pallas import tpu as pltpu

D_MODEL = 1024
A_WIDTH = 512
B_WIDTH = 512
A_GROUPS = 4
A_DIM = 128
CHUNK = 128
B_HEADS = 4
B_DK = 128
B_DV = 128
D_FF = 4096
IN_COLS = 2 * A_WIDTH + 4 * B_WIDTH
ROPE_BASE = 10000.0
EPS = 1e-6
PAST_LEN = 16384

ROW_TILE = 512
FF_TILE = 1024
STATE_ROWS = 16
VMEM_LIMIT_BYTES = 56 * 1024 * 1024

F32 = jnp.float32
BF16 = jnp.bfloat16


def _rms(x, w):
    return x * lax.rsqrt(jnp.mean(x * x, axis=-1, keepdims=True) + EPS) * w


def _center_norm(x):
    mu = jnp.mean(x, axis=-1, keepdims=True)
    xc = x - mu
    return xc * lax.rsqrt(jnp.mean(xc * xc, axis=-1, keepdims=True) + EPS)


def _gelu(x):
    return jax.nn.gelu(x, approximate=True)


def _silu(x):
    return x * (1.0 / (1.0 + jnp.exp(-x)))


def _rope(x, cosf, sinf):
    return x * cosf + pltpu.roll(x, B_DK // 2, axis=1) * sinf


def _dot(a, b):
    return jnp.dot(a, b, preferred_element_type=F32)


def _dot_nt(a, b):
    return lax.dot_general(a, b, (((1,), (1,)), ((), ())), preferred_element_type=F32)


def _dot_tn(a, b):
    return lax.dot_general(a, b, (((0,), (0,)), ((), ())), preferred_element_type=F32)


def _mlp(x1, pre_mlp_w, w_up_ref, w_down_ref, post_mlp_w):
    hn = _rms(x1, pre_mlp_w).astype(BF16)
    f = None
    for j in range(D_FF // FF_TILE):
        cols = slice(j * FF_TILE, (j + 1) * FF_TILE)
        u = jnp.maximum(_dot(hn, w_up_ref[:, cols]), 0.0)
        part = _dot((u * u).astype(BF16), w_down_ref[cols, :])
        f = part if f is None else f + part
    return x1 + _rms(f, post_mlp_w)


def _prompt_kernel(x_ref, pre_mix_ref, w_in_ref, lnw_ref, lnb_ref, ws_ref, bs_ref, gn_ref,
                   w_out_ref, post_mix_ref, pre_mlp_ref, w_up_ref, w_down_ref, post_mlp_ref,
                   cos_ref, sin_ref, mask_ref, qdec_ref, kdec_ref, cdec_ref,
                   y_ref, s_ref, mix_scr):
    t = pl.program_id(1)

    @pl.when(t == 0)
    def _():
        s_ref[...] = jnp.zeros_like(s_ref)

    x = x_ref[...]
    h = _rms(x, pre_mix_ref[...]).astype(BF16)

    za = _dot(h, w_in_ref[:, :2 * A_WIDTH])
    row = lax.broadcasted_iota(jnp.int32, (CHUNK, CHUNK), 0)
    col = lax.broadcasted_iota(jnp.int32, (CHUNK, CHUNK), 1)
    for g in range(A_GROUPS):
        gc = slice(g * A_DIM, (g + 1) * A_DIM)
        ua = _gelu(za[:, g * A_DIM:(g + 1) * A_DIM])
        va = _gelu(za[:, A_WIDTH + g * A_DIM:A_WIDTH + (g + 1) * A_DIM])
        va = (_center_norm(va) * lnw_ref[:, gc] + lnb_ref[:, gc]).astype(BF16)
        w_tril = jnp.where(row >= col, ws_ref[g], 0.0).astype(BF16)
        bias = jnp.broadcast_to(bs_ref[:, g:g + 1], (CHUNK, A_DIM))
        for c in range(ROW_TILE // CHUNK):
            rows = slice(c * CHUNK, (c + 1) * CHUNK)
            mixed = _dot(w_tril, va[rows]) + bias
            mix_scr[rows, gc] = (ua[rows] * mixed).astype(BF16)

    zb = _dot(h, w_in_ref[:, 2 * A_WIDTH:])
    cosf = cos_ref[...]
    sinf = sin_ref[...]
    for hh in range(B_HEADS):
        hc = slice(hh * B_DK, (hh + 1) * B_DK)
        q = _rope(zb[:, hh * B_DK:(hh + 1) * B_DK], cosf, sinf)
        k = _rope(zb[:, B_WIDTH + hh * B_DK:B_WIDTH + (hh + 1) * B_DK], cosf, sinf) * (B_DK ** -0.5)
        v = zb[:, 2 * B_WIDTH + hh * B_DV:2 * B_WIDTH + (hh + 1) * B_DV].astype(BF16)
        gate = _silu(zb[:, 3 * B_WIDTH + hh * B_DV:3 * B_WIDTH + (hh + 1) * B_DV])
        mask = mask_ref[hh]
        qdec = qdec_ref[hh]
        kdec = kdec_ref[hh]
        cdec = cdec_ref[hh]
        gnw = gn_ref[:, hc]
        for c in range(ROW_TILE // CHUNK):
            rows = slice(c * CHUNK, (c + 1) * CHUNK)
            qc = q[rows]
            kc = k[rows]
            vc = v[rows]
            s = s_ref[hh]
            scores = _dot_nt(qc.astype(BF16), kc.astype(BF16)) * mask
            inner = _dot(scores.astype(BF16), vc)
            cross = _dot((qc * qdec).astype(BF16), s.astype(BF16))
            s_ref[hh] = s * cdec + _dot_tn((kc * kdec).astype(BF16), vc)
            o = _center_norm(inner + cross) * gnw
            mix_scr[rows, A_WIDTH + hh * B_DV:A_WIDTH + (hh + 1) * B_DV] = (o * gate[rows]).astype(BF16)

    mix = _dot(mix_scr[...], w_out_ref[...])
    x1 = x + _rms(mix, post_mix_ref[...])
    y_ref[...] = _mlp(x1, pre_mlp_ref[...], w_up_ref, w_down_ref, post_mlp_ref[...])


def _resident(shape):
    zeros = (0,) * len(shape)
    return pl.BlockSpec(shape, lambda b, t: zeros, pipeline_mode=pl.Buffered(1))


def _prompt_layer(x, p, consts):
    batch, seq, _ = x.shape
    grid = (batch, seq // ROW_TILE)
    row_spec = pl.BlockSpec((None, ROW_TILE, D_MODEL), lambda b, t: (b, t, 0))
    tab_spec = pl.BlockSpec((ROW_TILE, B_DK), lambda b, t: (t, 0))
    in_specs = [
        row_spec,
        _resident((1, D_MODEL)),
        _resident((D_MODEL, IN_COLS)),
        _resident((1, A_WIDTH)),
        _resident((1, A_WIDTH)),
        _resident((A_GROUPS, CHUNK, CHUNK)),
        _resident((CHUNK, A_GROUPS)),
        _resident((1, B_WIDTH)),
        _resident((D_MODEL, D_MODEL)),
        _resident((1, D_MODEL)),
        _resident((1, D_MODEL)),
        _resident((D_MODEL, D_FF)),
        _resident((D_FF, D_MODEL)),
        _resident((1, D_MODEL)),
        tab_spec,
        tab_spec,
        _resident((B_HEADS, CHUNK, CHUNK)),
        _resident((B_HEADS, CHUNK, B_DK)),
        _resident((B_HEADS, CHUNK, B_DK)),
        _resident((B_HEADS, 1, B_DV)),
    ]
    out_specs = [
        row_spec,
        pl.BlockSpec((None, B_HEADS, B_DK, B_DV), lambda b, t: (b, 0, 0, 0)),
    ]
    return pl.pallas_call(
        _prompt_kernel,
        grid=grid,
        in_specs=in_specs,
        out_specs=out_specs,
        out_shape=[
            jax.ShapeDtypeStruct(x.shape, F32),
            jax.ShapeDtypeStruct((batch, B_HEADS, B_DK, B_DV), F32),
        ],
        scratch_shapes=[pltpu.VMEM((ROW_TILE, D_MODEL), BF16)],
        compiler_params=pltpu.CompilerParams(
            dimension_semantics=("arbitrary", "arbitrary"),
            vmem_limit_bytes=VMEM_LIMIT_BYTES),
        name="prompt_layer",
    )(x, p["pre_mix_w"], p["w_in"], p["ln_v_w"], p["ln_v_b"], p["w_s"], p["b_s_t"], p["gn_w"],
      p["w_out"], p["post_mix_w"], p["pre_mlp_w"], p["w_up"], p["w_down"], p["post_mlp_w"],
      consts["cos"], consts["sin"], consts["mask"], consts["qdec"], consts["kdec"], consts["cdec"])


def _decode_in_kernel(x_ref, pre_mix_ref, w_in_ref, lnw_ref, lnb_ref, ws0_ref, bs0_ref,
                      cos_ref, sin_ref,
                      outa_ref, va_ref, q_ref, k_ref, v_ref, gate_ref):
    h = _rms(x_ref[...], pre_mix_ref[...]).astype(BF16)
    z = _dot(h, w_in_ref[...])
    cosf = cos_ref[...]
    sinf = sin_ref[...]
    for g in range(A_GROUPS):
        gc = slice(g * A_DIM, (g + 1) * A_DIM)
        ua = _gelu(z[:, g * A_DIM:(g + 1) * A_DIM])
        va = _gelu(z[:, A_WIDTH + g * A_DIM:A_WIDTH + (g + 1) * A_DIM])
        va = _center_norm(va) * lnw_ref[:, gc] + lnb_ref[:, gc]
        va_ref[:, gc] = va
        outa_ref[:, gc] = ua * (ws0_ref[:, gc] * va + bs0_ref[:, gc])
    for hh in range(B_HEADS):
        hc = slice(hh * B_DK, (hh + 1) * B_DK)
        base = 2 * A_WIDTH
        q_ref[:, hc] = _rope(z[:, base + hh * B_DK:base + (hh + 1) * B_DK], cosf, sinf)
        k_ref[:, hc] = _rope(z[:, base + B_WIDTH + hh * B_DK:base + B_WIDTH + (hh + 1) * B_DK],
                             cosf, sinf) * (B_DK ** -0.5)
    v_ref[...] = z[:, 2 * A_WIDTH + 2 * B_WIDTH:2 * A_WIDTH + 3 * B_WIDTH]
    gate_ref[...] = _silu(z[:, 2 * A_WIDTH + 3 * B_WIDTH:])


def _decode_state_kernel(q_ref, k_ref, v_ref, s_ref, gam_ref, o_ref, s_out_ref):
    rid = lax.broadcasted_iota(jnp.int32, (STATE_ROWS, B_DK), 0)
    for hh in range(B_HEADS):
        hc = slice(hh * B_DK, (hh + 1) * B_DK)
        q = q_ref[:, hc]
        k = k_ref[:, hc]
        v = v_ref[:, hc]
        gam = gam_ref[hh]
        qk = jnp.sum(q * k, axis=-1, keepdims=True)
        vb = v.astype(BF16)
        qg = (q * gam).astype(BF16)
        cross_rows = []
        for j in range(STATE_ROWS):
            s = s_ref[j, hh]
            cross_rows.append(_dot(qg, s.astype(BF16))[j:j + 1, :])
            kj = jnp.where(rid == j, k, 0.0).astype(BF16)
            s_out_ref[j, hh] = s * gam + _dot_tn(kj, vb)
        o_ref[:, hc] = qk * v + jnp.concatenate(cross_rows, axis=0)


def _decode_out_kernel(x_ref, outa_ref, o_ref, gate_ref, gn_ref, w_out_ref, post_mix_ref,
                       pre_mlp_ref, w_up_ref, w_down_ref, post_mlp_ref, y_ref):
    parts = [outa_ref[...].astype(BF16)]
    for hh in range(B_HEADS):
        hc = slice(hh * B_DV, (hh + 1) * B_DV)
        o = _center_norm(o_ref[:, hc]) * gn_ref[:, hc]
        parts.append((o * gate_ref[:, hc]).astype(BF16))
    mix = _dot(jnp.concatenate(parts, axis=-1), w_out_ref[...])
    x = x_ref[...]
    x1 = x + _rms(mix, post_mix_ref[...])
    y_ref[...] = _mlp(x1, pre_mlp_ref[...], w_up_ref, w_down_ref, post_mlp_ref[...])


def _full(shape):
    zeros = (0,) * len(shape)
    return pl.BlockSpec(shape, lambda *_: zeros)


def _decode_layer(x, s0, p, consts):
    n = x.shape[0]
    act = jax.ShapeDtypeStruct((n, A_WIDTH), F32)
    outa, va, q, k, v, gate = pl.pallas_call(
        _decode_in_kernel,
        grid=(1,),
        in_specs=[_full((n, D_MODEL)), _full((1, D_MODEL)), _full((D_MODEL, IN_COLS)),
                  _full((1, A_WIDTH)), _full((1, A_WIDTH)), _full((1, A_WIDTH)), _full((1, A_WIDTH)),
                  _full((1, B_DK)), _full((1, B_DK))],
        out_specs=[_full((n, A_WIDTH))] * 6,
        out_shape=[act] * 6,
        compiler_params=pltpu.CompilerParams(vmem_limit_bytes=VMEM_LIMIT_BYTES),
        name="decode_in",
    )(x, p["pre_mix_w"], p["w_in"], p["ln_v_w"], p["ln_v_b"], p["w_s0"], p["b_s0"],
      consts["cos1"], consts["sin1"])

    row_spec = pl.BlockSpec((STATE_ROWS, B_WIDTH), lambda i: (i, 0))
    st_spec = pl.BlockSpec((STATE_ROWS, B_HEADS, B_DK, B_DV), lambda i: (i, 0, 0, 0))
    o, s_new = pl.pallas_call(
        _decode_state_kernel,
        grid=(n // STATE_ROWS,),
        in_specs=[row_spec, row_spec, row_spec, st_spec, _full((B_HEADS, 1, B_DV))],
        out_specs=[row_spec, st_spec],
        out_shape=[act, jax.ShapeDtypeStruct(s0.shape, F32)],
        compiler_params=pltpu.CompilerParams(
            dimension_semantics=("arbitrary",), vmem_limit_bytes=VMEM_LIMIT_BYTES),
        name="decode_state",
    )(q, k, v, s0, consts["gam1"])

    y = pl.pallas_call(
        _decode_out_kernel,
        grid=(1,),
        in_specs=[_full((n, D_MODEL)), _full((n, A_WIDTH)), _full((n, B_WIDTH)), _full((n, B_WIDTH)),
                  _full((1, B_WIDTH)), _full((D_MODEL, D_MODEL)), _full((1, D_MODEL)),
                  _full((1, D_MODEL)), _full((D_MODEL, D_FF)), _full((D_FF, D_MODEL)),
                  _full((1, D_MODEL))],
        out_specs=_full((n, D_MODEL)),
        out_shape=jax.ShapeDtypeStruct((n, D_MODEL), F32),
        compiler_params=pltpu.CompilerParams(vmem_limit_bytes=VMEM_LIMIT_BYTES),
        name="decode_out",
    )(x, outa, o, gate, p["gn_w"], p["w_out"], p["post_mix_w"], p["pre_mlp_w"],
      p["w_up"], p["w_down"], p["post_mlp_w"])
    return y, s_new, va


def _rope_tables(pos):
    half = B_DK // 2
    inv = ROPE_BASE ** (-jnp.arange(half, dtype=F32) / half)
    ang = pos.astype(F32)[:, None] * inv[None, :]
    cos, sin = jnp.cos(ang), jnp.sin(ang)
    return jnp.concatenate([cos, cos], axis=-1), jnp.concatenate([-sin, sin], axis=-1)


def _decay_tables(c):
    lg = jnp.log(1.0 - 2.0 ** (-5.0 - jnp.arange(B_HEADS, dtype=F32)))
    idx = jnp.arange(c, dtype=F32)
    diff = idx[:, None] - idx[None, :]
    mask = jnp.where(diff[None] >= 0, jnp.exp(lg[:, None, None] * jnp.maximum(diff, 0.0)[None]), 0.0)
    qdec = jnp.exp(lg[:, None] * (idx[None, :] + 1.0))
    kdec = jnp.exp(lg[:, None] * (c - 1.0 - idx[None, :]))
    cdec = jnp.exp(lg * c)
    return mask, qdec, kdec, cdec


def kernel(x_prompt, x_sample, state_ret, pre_mix_w, w_in, ln_v_w, ln_v_b, w_s, b_s, gn_w,
           w_out, post_mix_w, pre_mlp_w, w_up, w_down, post_mlp_w):
    depth = w_in.shape[0]
    seq = x_prompt.shape[1]
    n_dec, dec_seq, _ = x_sample.shape
    assert dec_seq == 1 and seq % ROW_TILE == 0 and n_dec % STATE_ROWS == 0

    cos, sin = _rope_tables(jnp.arange(seq, dtype=jnp.int32))
    mask, qdec, kdec, cdec = _decay_tables(CHUNK)
    cos1, sin1 = _rope_tables(PAST_LEN + jnp.arange(1, dtype=jnp.int32))
    _, qdec1, _, _ = _decay_tables(1)
    consts = {
        "cos": cos, "sin": sin, "mask": mask,
        "qdec": jnp.broadcast_to(qdec[:, :, None], (B_HEADS, CHUNK, B_DK)),
        "kdec": jnp.broadcast_to(kdec[:, :, None], (B_HEADS, CHUNK, B_DK)),
        "cdec": jnp.broadcast_to(cdec[:, None, None], (B_HEADS, 1, B_DV)),
        "cos1": cos1, "sin1": sin1,
        "gam1": jnp.broadcast_to(qdec1[:, :, None], (B_HEADS, 1, B_DV)),
    }

    yp, ys = x_prompt, x_sample.reshape(n_dec, D_MODEL)
    sp_list, ss_list, vs_list = [], [], []
    for l in range(depth):
        p = {
            "pre_mix_w": pre_mix_w[l][None, :],
            "w_in": w_in[l].astype(BF16),
            "ln_v_w": ln_v_w[l].reshape(1, A_WIDTH),
            "ln_v_b": ln_v_b[l].reshape(1, A_WIDTH),
            "w_s": w_s[l],
            "b_s_t": b_s[l].T,
            "w_s0": jnp.repeat(w_s[l][:, 0, 0], A_DIM)[None, :],
            "b_s0": jnp.repeat(b_s[l][:, 0], A_DIM)[None, :],
            "gn_w": gn_w[l].reshape(1, B_WIDTH),
            "w_out": w_out[l].astype(BF16),
            "post_mix_w": post_mix_w[l][None, :],
            "pre_mlp_w": pre_mlp_w[l][None, :],
            "w_up": w_up[l].astype(BF16),
            "w_down": w_down[l].astype(BF16),
            "post_mlp_w": post_mlp_w[l][None, :],
        }
        yp, sp = _prompt_layer(yp, p, consts)
        ys, ss, vs = _decode_layer(ys, state_ret[l], p, consts)
        sp_list.append(sp)
        ss_list.append(ss)
        vs_list.append(vs.reshape(n_dec, 1, A_GROUPS, A_DIM))
    return (yp, ys.reshape(n_dec, 1, D_MODEL), jnp.stack(sp_list, axis=0),
            jnp.stack(ss_list, axis=0), jnp.stack(vs_list, axis=0))
```

```python
import functools
import math

import jax
import jax.numpy as jnp
from jax import lax
from jax.experimental import pallas as pl
from jax.experimental.pallas import tpu as pltpu

D_MODEL = 1024
A_WIDTH = 512
B_WIDTH = 512
A_GROUPS = 4
A_DIM = 128
CHUNK = 128
B_HEADS = 4
B_DK = 128
B_DV = 128
D_FF = 4096
IN_COLS = 2 * A_WIDTH + 4 * B_WIDTH
ROPE_BASE = 10000.0
EPS = 1e-6
PAST_LEN = 16384

ROW_TILE = 512
FF_TILE = 512
STATE_ROWS = 16
VMEM_LIMIT_BYTES = 56 * 1024 * 1024

F32 = jnp.float32
BF16 = jnp.bfloat16


def _rms(x, w):
    return x * lax.rsqrt(jnp.mean(x * x, axis=-1, keepdims=True) + EPS) * w


def _center_norm(x):
    mu = jnp.mean(x, axis=-1, keepdims=True)
    xc = x - mu
    return xc * lax.rsqrt(jnp.mean(xc * xc, axis=-1, keepdims=True) + EPS)


def _gelu(x):
    return jax.nn.gelu(x, approximate=True)


def _silu(x):
    return x * (1.0 / (1.0 + jnp.exp(-x)))


def _rope(x, cosf, sinf):
    return x * cosf + pltpu.roll(x, B_DK // 2, axis=1) * sinf


def _dot(a, b):
    return jnp.dot(a, b, preferred_element_type=F32)


def _dot_nt(a, b):
    return lax.dot_general(a, b, (((1,), (1,)), ((), ())), preferred_element_type=F32)


def _dot_tn(a, b):
    return lax.dot_general(a, b, (((0,), (0,)), ((), ())), preferred_element_type=F32)


def _mlp(x1, pre_mlp_w, w_up_ref, w_down_ref, post_mlp_w):
    hn = _rms(x1, pre_mlp_w).astype(BF16)
    f = None
    for j in range(D_FF // FF_TILE):
        cols = slice(j * FF_TILE, (j + 1) * FF_TILE)
        u = jnp.maximum(_dot(hn, w_up_ref[:, cols]), 0.0)
        part = _dot((u * u).astype(BF16), w_down_ref[cols, :])
        f = part if f is None else f + part
    return x1 + _rms(f, post_mlp_w)


def _prompt_kernel(x_ref, pre_mix_ref, w_in_ref, lnw_ref, lnb_ref, ws_ref, bs_ref, gn_ref,
                   w_out_ref, post_mix_ref, pre_mlp_ref, w_up_ref, w_down_ref, post_mlp_ref,
                   cos_ref, sin_ref, mask_ref, qdec_ref, kdec_ref, cdec_ref,
                   y_ref, s_ref, mix_scr, x1_scr, hn_scr, *, tiles_per_seq, n_tiles):
    i = pl.program_id(0)
    slot = lax.rem(i, 2)
    live = i < n_tiles

    @pl.when(i == 0)
    def _():
        x1_scr[1] = jnp.zeros((ROW_TILE, D_MODEL), F32)
        hn_scr[1] = jnp.zeros((ROW_TILE, D_MODEL), BF16)

    @pl.when(jnp.logical_and(lax.rem(i, tiles_per_seq) == 0, live))
    def _():
        s_ref[...] = jnp.zeros_like(s_ref)

    n_chunks = ROW_TILE // CHUNK
    chunk_rows = [slice(c * CHUNK, (c + 1) * CHUNK) for c in range(n_chunks)]
    hn_prev = hn_scr.at[1 - slot]
    acc = []

    def mlp_piece(j):
        cols = slice(j * FF_TILE, (j + 1) * FF_TILE)
        u = jnp.maximum(_dot(hn_prev[...], w_up_ref[:, cols]), 0.0)
        part = _dot((u * u).astype(BF16), w_down_ref[cols, :])
        acc[:] = [part if not acc else acc[0] + part]

    x = x_ref[...]
    mlp_piece(0)
    h = _rms(x, pre_mix_ref[...]).astype(BF16)
    z = _dot(h, w_in_ref[...])
    cosf = cos_ref[...]
    sinf = sin_ref[...]

    def head_inputs(hh):
        base = 2 * A_WIDTH + hh * B_DK
        q = _rope(z[:, base:base + B_DK], cosf, sinf)
        k = _rope(z[:, base + B_WIDTH:base + B_WIDTH + B_DK], cosf, sinf) * (B_DK ** -0.5)
        v = z[:, base + 2 * B_WIDTH:base + 2 * B_WIDTH + B_DV].astype(BF16)
        return {
            "q": q.astype(BF16),
            "k": k.astype(BF16),
            "v": v,
            "qd": [(q[r] * qdec_ref[hh]).astype(BF16) for r in chunk_rows],
            "kd": [(k[r] * kdec_ref[hh]).astype(BF16) for r in chunk_rows],
        }

    def head_dots_a(hd):
        hd["scores"] = [_dot_nt(hd["q"][r], hd["k"][r]) for r in chunk_rows]
        hd["kv"] = [_dot_tn(hd["kd"][c], hd["v"][r]) for c, r in enumerate(chunk_rows)]

    def head_chain(hh, hd):
        s = s_ref[hh]
        hd["s"] = []
        for c in range(n_chunks):
            hd["s"].append(s.astype(BF16))
            s = s * cdec_ref[hh] + hd["kv"][c]
        s_ref[hh] = jnp.where(live, s, s_ref[hh])
        hd["p"] = [(hd["scores"][c] * mask_ref[hh]).astype(BF16) for c in range(n_chunks)]

    def head_dots_b(hd):
        hd["o"] = [_dot(hd["p"][c], hd["v"][r]) + _dot(hd["qd"][c], hd["s"][c])
                   for c, r in enumerate(chunk_rows)]

    def head_finish(hh, hd):
        base = 2 * A_WIDTH + 3 * B_WIDTH + hh * B_DV
        gate = _silu(z[:, base:base + B_DV])
        gnw = gn_ref[:, hh * B_DV:(hh + 1) * B_DV]
        for c, r in enumerate(chunk_rows):
            o = _center_norm(hd["o"][c]) * gnw
            mix_scr[r, A_WIDTH + hh * B_DV:A_WIDTH + (hh + 1) * B_DV] = (o * gate[r]).astype(BF16)

    mlp_piece(1)
    ua, va = [], []
    for g in range(A_GROUPS):
        gc = slice(g * A_DIM, (g + 1) * A_DIM)
        ua.append(_gelu(z[:, g * A_DIM:(g + 1) * A_DIM]))
        vg = _gelu(z[:, A_WIDTH + g * A_DIM:A_WIDTH + (g + 1) * A_DIM])
        va.append((_center_norm(vg) * lnw_ref[:, gc] + lnb_ref[:, gc]).astype(BF16))
        if g == 1:
            mlp_piece(2)
    heads = [head_inputs(0)]

    row = lax.broadcasted_iota(jnp.int32, (CHUNK, CHUNK), 0)
    col = lax.broadcasted_iota(jnp.int32, (CHUNK, CHUNK), 1)
    mixed = []
    for g in range(A_GROUPS):
        w_tril = jnp.where(row >= col, ws_ref[g], 0.0).astype(BF16)
        mixed.append([_dot(w_tril, va[g][r]) for r in chunk_rows])
    head_dots_a(heads[0])

    for hh in range(B_HEADS):
        mlp_piece(3 + hh)
        if hh == 0:
            for g in range(A_GROUPS):
                gc = slice(g * A_DIM, (g + 1) * A_DIM)
                bias = jnp.broadcast_to(bs_ref[:, g:g + 1], (CHUNK, A_DIM))
                for c, r in enumerate(chunk_rows):
                    mix_scr[r, gc] = (ua[g][r] * (mixed[g][c] + bias)).astype(BF16)
        else:
            head_finish(hh - 1, heads[hh - 1])
        head_chain(hh, heads[hh])
        if hh + 1 < B_HEADS:
            heads.append(head_inputs(hh + 1))
        head_dots_b(heads[hh])
        if hh + 1 < B_HEADS:
            head_dots_a(heads[hh + 1])

    mlp_piece(7)
    head_finish(B_HEADS - 1, heads[B_HEADS - 1])
    half = ROW_TILE // 2
    mix_a = _dot(mix_scr[:half, :], w_out_ref[...])
    y_ref[...] = x1_scr[1 - slot] + _rms(acc[0], post_mlp_ref[...])
    mix_b = _dot(mix_scr[half:, :], w_out_ref[...])
    for rows, mix in ((slice(0, half), mix_a), (slice(half, ROW_TILE), mix_b)):
        x1 = x[rows] + _rms(mix, post_mix_ref[...])
        x1_scr[slot, rows, :] = x1
        hn_scr[slot, rows, :] = _rms(x1, pre_mlp_ref[...]).astype(BF16)


def _resident(shape):
    zeros = (0,) * len(shape)
    return pl.BlockSpec(shape, lambda i: zeros, pipeline_mode=pl.Buffered(1))


def _prompt_layer(x, p, consts):
    batch, seq, _ = x.shape
    tiles_per_seq = seq // ROW_TILE
    n_tiles = batch * tiles_per_seq

    def mixer_tile(i):
        return jnp.minimum(i, n_tiles - 1)

    def mlp_tile(i):
        return jnp.maximum(i - 1, 0)

    x_spec = pl.BlockSpec((None, ROW_TILE, D_MODEL),
                          lambda i: (mixer_tile(i) // tiles_per_seq, mixer_tile(i) % tiles_per_seq, 0))
    y_spec = pl.BlockSpec((None, ROW_TILE, D_MODEL),
                          lambda i: (mlp_tile(i) // tiles_per_seq, mlp_tile(i) % tiles_per_seq, 0))
    tab_spec = pl.BlockSpec((ROW_TILE, B_DK), lambda i: (mixer_tile(i) % tiles_per_seq, 0))
    in_specs = [
        x_spec,
        _resident((1, D_MODEL)),
        _resident((D_MODEL, IN_COLS)),
        _resident((1, A_WIDTH)),
        _resident((1, A_WIDTH)),
        _resident((A_GROUPS, CHUNK, CHUNK)),
        _resident((CHUNK, A_GROUPS)),
        _resident((1, B_WIDTH)),
        _resident((D_MODEL, D_MODEL)),
        _resident((1, D_MODEL)),
        _resident((1, D_MODEL)),
        _resident((D_MODEL, D_FF)),
        _resident((D_FF, D_MODEL)),
        _resident((1, D_MODEL)),
        tab_spec,
        tab_spec,
        _resident((B_HEADS, CHUNK, CHUNK)),
        _resident((B_HEADS, CHUNK, B_DK)),
        _resident((B_HEADS, CHUNK, B_DK)),
        _resident((B_HEADS, 1, B_DV)),
    ]
    out_specs = [
        y_spec,
        pl.BlockSpec((None, B_HEADS, B_DK, B_DV),
                     lambda i: (mixer_tile(i) // tiles_per_seq, 0, 0, 0)),
    ]
    return pl.pallas_call(
        functools.partial(_prompt_kernel, tiles_per_seq=tiles_per_seq, n_tiles=n_tiles),
        grid=(n_tiles + 1,),
        in_specs=in_specs,
        out_specs=out_specs,
        out_shape=[
            jax.ShapeDtypeStruct(x.shape, F32),
            jax.ShapeDtypeStruct((batch, B_HEADS, B_DK, B_DV), F32),
        ],
        scratch_shapes=[pltpu.VMEM((ROW_TILE, D_MODEL), BF16),
                        pltpu.VMEM((2, ROW_TILE, D_MODEL), F32),
                        pltpu.VMEM((2, ROW_TILE, D_MODEL), BF16)],
        compiler_params=pltpu.CompilerParams(
            dimension_semantics=("arbitrary",),
            vmem_limit_bytes=VMEM_LIMIT_BYTES),
        name="prompt_layer",
    )(x, p["pre_mix_w"], p["w_in"], p["ln_v_w"], p["ln_v_b"], p["w_s"], p["b_s_t"], p["gn_w"],
      p["w_out"], p["post_mix_w"], p["pre_mlp_w"], p["w_up"], p["w_down"], p["post_mlp_w"],
      consts["cos"], consts["sin"], consts["mask"], consts["qdec"], consts["kdec"], consts["cdec"])


def _decode_in_kernel(x_ref, pre_mix_ref, w_in_ref, lnw_ref, lnb_ref, ws0_ref, bs0_ref,
                      cos_ref, sin_ref,
                      outa_ref, va_ref, q_ref, k_ref, v_ref, gate_ref):
    h = _rms(x_ref[...], pre_mix_ref[...]).astype(BF16)
    z = _dot(h, w_in_ref[...])
    cosf = cos_ref[...]
    sinf = sin_ref[...]
    for g in range(A_GROUPS):
        gc = slice(g * A_DIM, (g + 1) * A_DIM)
        ua = _gelu(z[:, g * A_DIM:(g + 1) * A_DIM])
        va = _gelu(z[:, A_WIDTH + g * A_DIM:A_WIDTH + (g + 1) * A_DIM])
        va = _center_norm(va) * lnw_ref[:, gc] + lnb_ref[:, gc]
        va_ref[:, gc] = va
        outa_ref[:, gc] = ua * (ws0_ref[:, gc] * va + bs0_ref[:, gc])
    for hh in range(B_HEADS):
        hc = slice(hh * B_DK, (hh + 1) * B_DK)
        base = 2 * A_WIDTH
        q_ref[:, hc] = _rope(z[:, base + hh * B_DK:base + (hh + 1) * B_DK], cosf, sinf)
        k_ref[:, hc] = _rope(z[:, base + B_WIDTH + hh * B_DK:base + B_WIDTH + (hh + 1) * B_DK],
                             cosf, sinf) * (B_DK ** -0.5)
    v_ref[...] = z[:, 2 * A_WIDTH + 2 * B_WIDTH:2 * A_WIDTH + 3 * B_WIDTH]
    gate_ref[...] = _silu(z[:, 2 * A_WIDTH + 3 * B_WIDTH:])


def _decode_state_kernel(q_ref, k_ref, v_ref, s_ref, gam_ref, o_ref, s_out_ref):
    rid = lax.broadcasted_iota(jnp.int32, (STATE_ROWS, B_DK), 0)
    for hh in range(B_HEADS):
        hc = slice(hh * B_DK, (hh + 1) * B_DK)
        q = q_ref[:, hc]
        k = k_ref[:, hc]
        v = v_ref[:, hc]
        gam = gam_ref[hh]
        qk = jnp.sum(q * k, axis=-1, keepdims=True)
        vb = v.astype(BF16)
        qg = (q * gam).astype(BF16)
        cross_rows = []
        for j in range(STATE_ROWS):
            s = s_ref[j, hh]
            cross_rows.append(_dot(qg, s.astype(BF16))[j:j + 1, :])
            kj = jnp.where(rid == j, k, 0.0).astype(BF16)
            s_out_ref[j, hh] = s * gam + _dot_tn(kj, vb)
        o_ref[:, hc] = qk * v + jnp.concatenate(cross_rows, axis=0)


def _decode_out_kernel(x_ref, outa_ref, o_ref, gate_ref, gn_ref, w_out_ref, post_mix_ref,
                       pre_mlp_ref, w_up_ref, w_down_ref, post_mlp_ref, y_ref):
    parts = [outa_ref[...].astype(BF16)]
    for hh in range(B_HEADS):
        hc = slice(hh * B_DV, (hh + 1) * B_DV)
        o = _center_norm(o_ref[:, hc]) * gn_ref[:, hc]
        parts.append((o * gate_ref[:, hc]).astype(BF16))
    mix = _dot(jnp.concatenate(parts, axis=-1), w_out_ref[...])
    x = x_ref[...]
    x1 = x + _rms(mix, post_mix_ref[...])
    y_ref[...] = _mlp(x1, pre_mlp_ref[...], w_up_ref, w_down_ref, post_mlp_ref[...])


def _full(shape):
    zeros = (0,) * len(shape)
    return pl.BlockSpec(shape, lambda *_: zeros)


def _decode_layer(x, s0, p, consts):
    n = x.shape[0]
    act = jax.ShapeDtypeStruct((n, A_WIDTH), F32)
    outa, va, q, k, v, gate = pl.pallas_call(
        _decode_in_kernel,
        grid=(1,),
        in_specs=[_full((n, D_MODEL)), _full((1, D_MODEL)), _full((D_MODEL, IN_COLS)),
                  _full((1, A_WIDTH)), _full((1, A_WIDTH)), _full((1, A_WIDTH)), _full((1, A_WIDTH)),
                  _full((1, B_DK)), _full((1, B_DK))],
        out_specs=[_full((n, A_WIDTH))] * 6,
        out_shape=[act] * 6,
        compiler_params=pltpu.CompilerParams(vmem_limit_bytes=VMEM_LIMIT_BYTES),
        name="decode_in",
    )(x, p["pre_mix_w"], p["w_in"], p["ln_v_w"], p["ln_v_b"], p["w_s0"], p["b_s0"],
      consts["cos1"], consts["sin1"])

    row_spec = pl.BlockSpec((STATE_ROWS, B_WIDTH), lambda i: (i, 0))
    st_spec = pl.BlockSpec((STATE_ROWS, B_HEADS, B_DK, B_DV), lambda i: (i, 0, 0, 0))
    o, s_new = pl.pallas_call(
        _decode_state_kernel,
        grid=(n // STATE_ROWS,),
        in_specs=[row_spec, row_spec, row_spec, st_spec, _full((B_HEADS, 1, B_DV))],
        out_specs=[row_spec, st_spec],
        out_shape=[act, jax.ShapeDtypeStruct(s0.shape, F32)],
        compiler_params=pltpu.CompilerParams(
            dimension_semantics=("arbitrary",), vmem_limit_bytes=VMEM_LIMIT_BYTES),
        name="decode_state",
    )(q, k, v, s0, consts["gam1"])

    y = pl.pallas_call(
        _decode_out_kernel,
        grid=(1,),
        in_specs=[_full((n, D_MODEL)), _full((n, A_WIDTH)), _full((n, B_WIDTH)), _full((n, B_WIDTH)),
                  _full((1, B_WIDTH)), _full((D_MODEL, D_MODEL)), _full((1, D_MODEL)),
                  _full((1, D_MODEL)), _full((D_MODEL, D_FF)), _full((D_FF, D_MODEL)),
                  _full((1, D_MODEL))],
        out_specs=_full((n, D_MODEL)),
        out_shape=jax.ShapeDtypeStruct((n, D_MODEL), F32),
        compiler_params=pltpu.CompilerParams(vmem_limit_bytes=VMEM_LIMIT_BYTES),
        name="decode_out",
    )(x, outa, o, gate, p["gn_w"], p["w_out"], p["post_mix_w"], p["pre_mlp_w"],
      p["w_up"], p["w_down"], p["post_mlp_w"])
    return y, s_new, va


def _rope_tables(pos):
    half = B_DK // 2
    inv = ROPE_BASE ** (-jnp.arange(half, dtype=F32) / half)
    ang = pos.astype(F32)[:, None] * inv[None, :]
    cos, sin = jnp.cos(ang), jnp.sin(ang)
    return jnp.concatenate([cos, cos], axis=-1), jnp.concatenate([-sin, sin], axis=-1)


def _decay_tables(c):
    lg = jnp.log(1.0 - 2.0 ** (-5.0 - jnp.arange(B_HEADS, dtype=F32)))
    idx = jnp.arange(c, dtype=F32)
    diff = idx[:, None] - idx[None, :]
    mask = jnp.where(diff[None] >= 0, jnp.exp(lg[:, None, None] * jnp.maximum(diff, 0.0)[None]), 0.0)
    qdec = jnp.exp(lg[:, None] * (idx[None, :] + 1.0))
    kdec = jnp.exp(lg[:, None] * (c - 1.0 - idx[None, :]))
    cdec = jnp.exp(lg * c)
    return mask, qdec, kdec, cdec


def kernel(x_prompt, x_sample, state_ret, pre_mix_w, w_in, ln_v_w, ln_v_b, w_s, b_s, gn_w,
           w_out, post_mix_w, pre_mlp_w, w_up, w_down, post_mlp_w):
    depth = w_in.shape[0]
    seq = x_prompt.shape[1]
    n_dec, dec_seq, _ = x_sample.shape
    assert dec_seq == 1 and seq % ROW_TILE == 0 and n_dec % STATE_ROWS == 0

    cos, sin = _rope_tables(jnp.arange(seq, dtype=jnp.int32))
    mask, qdec, kdec, cdec = _decay_tables(CHUNK)
    cos1, sin1 = _rope_tables(PAST_LEN + jnp.arange(1, dtype=jnp.int32))
    _, qdec1, _, _ = _decay_tables(1)
    consts = {
        "cos": cos, "sin": sin, "mask": mask,
        "qdec": jnp.broadcast_to(qdec[:, :, None], (B_HEADS, CHUNK, B_DK)),
        "kdec": jnp.broadcast_to(kdec[:, :, None], (B_HEADS, CHUNK, B_DK)),
        "cdec": jnp.broadcast_to(cdec[:, None, None], (B_HEADS, 1, B_DV)),
        "cos1": cos1, "sin1": sin1,
        "gam1": jnp.broadcast_to(qdec1[:, :, None], (B_HEADS, 1, B_DV)),
    }

    yp, ys = x_prompt, x_sample.reshape(n_dec, D_MODEL)
    sp_list, ss_list, vs_list = [], [], []
    for l in range(depth):
        p = {
            "pre_mix_w": pre_mix_w[l][None, :],
            "w_in": w_in[l].astype(BF16),
            "ln_v_w": ln_v_w[l].reshape(1, A_WIDTH),
            "ln_v_b": ln_v_b[l].reshape(1, A_WIDTH),
            "w_s": w_s[l],
            "b_s_t": b_s[l].T,
            "w_s0": jnp.repeat(w_s[l][:, 0, 0], A_DIM)[None, :],
            "b_s0": jnp.repeat(b_s[l][:, 0], A_DIM)[None, :],
            "gn_w": gn_w[l].reshape(1, B_WIDTH),
            "w_out": w_out[l].astype(BF16),
            "post_mix_w": post_mix_w[l][None, :],
            "pre_mlp_w": pre_mlp_w[l][None, :],
            "w_up": w_up[l].astype(BF16),
            "w_down": w_down[l].astype(BF16),
            "post_mlp_w": post_mlp_w[l][None, :],
        }
        yp, sp = _prompt_layer(yp, p, consts)
        ys, ss, vs = _decode_layer(ys, state_ret[l], p, consts)
        sp_list.append(sp)
        ss_list.append(ss)
        vs_list.append(vs.reshape(n_dec, 1, A_GROUPS, A_DIM))
    return (yp, ys.reshape(n_dec, 1, D_MODEL), jnp.stack(sp_list, axis=0),
            jnp.stack(ss_list, axis=0), jnp.stack(vs_list, axis=0))
```

```python
import functools
import math

import jax
import jax.numpy as jnp
from jax import lax
from jax.experimental import pallas as pl
from jax.experimental.pallas import tpu as pltpu

D_MODEL = 1024
A_WIDTH = 512
B_WIDTH = 512
A_GROUPS = 4
A_DIM = 128
CHUNK = 128
B_HEADS = 4
B_DK = 128
B_DV = 128
D_FF = 4096
IN_COLS = 2 * A_WIDTH + 4 * B_WIDTH
ROPE_BASE = 10000.0
EPS = 1e-6
PAST_LEN = 16384

ROW_TILE = 512
FF_TILE = 512
STATE_ROWS = 16
VMEM_LIMIT_BYTES = 56 * 1024 * 1024

F32 = jnp.float32
BF16 = jnp.bfloat16


def _rms(x, w):
    return x * lax.rsqrt(jnp.mean(x * x, axis=-1, keepdims=True) + EPS) * w


def _center_norm(x):
    mu = jnp.mean(x, axis=-1, keepdims=True)
    xc = x - mu
    return xc * lax.rsqrt(jnp.mean(xc * xc, axis=-1, keepdims=True) + EPS)


def _gelu(x):
    return jax.nn.gelu(x, approximate=True)


def _silu(x):
    return x * (1.0 / (1.0 + jnp.exp(-x)))


def _rope(x, cosf, sinf):
    return x * cosf + pltpu.roll(x, B_DK // 2, axis=1) * sinf


def _dot(a, b):
    return jnp.dot(a, b, preferred_element_type=F32)


def _dot_nt(a, b):
    return lax.dot_general(a, b, (((1,), (1,)), ((), ())), preferred_element_type=F32)


def _dot_tn(a, b):
    return lax.dot_general(a, b, (((0,), (0,)), ((), ())), preferred_element_type=F32)


def _mlp(x1, pre_mlp_w, w_up_ref, w_down_ref, post_mlp_w):
    hn = _rms(x1, pre_mlp_w).astype(BF16)
    f = None
    for j in range(D_FF // FF_TILE):
        cols = slice(j * FF_TILE, (j + 1) * FF_TILE)
        u = jnp.maximum(_dot(hn, w_up_ref[:, cols]), 0.0)
        part = _dot((u * u).astype(BF16), w_down_ref[cols, :])
        f = part if f is None else f + part
    return x1 + _rms(f, post_mlp_w)


def _prompt_kernel(x_ref, pre_mix_ref, w_in_ref, lnw_ref, lnb_ref, ws_ref, bs_ref, gn_ref,
                   w_out_ref, post_mix_ref, pre_mlp_ref, w_up_ref, w_down_ref, post_mlp_ref,
                   cos_ref, sin_ref, mask_ref, qdec_ref, kdec_ref, cdec_ref,
                   y_ref, s_ref, mix_scr, x1_scr, hn_scr, *, tiles_per_seq, n_tiles):
    i = pl.program_id(0)
    refs = (x_ref, pre_mix_ref, w_in_ref, lnw_ref, lnb_ref, ws_ref, bs_ref, gn_ref,
            w_out_ref, post_mix_ref, pre_mlp_ref, w_up_ref, w_down_ref, post_mlp_ref,
            cos_ref, sin_ref, mask_ref, qdec_ref, kdec_ref, cdec_ref,
            y_ref, s_ref, mix_scr, x1_scr, hn_scr)

    @pl.when(jnp.logical_and(lax.rem(i, tiles_per_seq) == 0, i < n_tiles))
    def _():
        s_ref[...] = jnp.zeros_like(s_ref)

    @pl.when(i == 0)
    def _():
        _prompt_step(*refs, with_mixer=True, with_mlp=False)

    @pl.when(jnp.logical_and(i > 0, i < n_tiles))
    def _():
        _prompt_step(*refs, with_mixer=True, with_mlp=True)

    @pl.when(i == n_tiles)
    def _():
        _prompt_step(*refs, with_mixer=False, with_mlp=True)


def _prompt_step(x_ref, pre_mix_ref, w_in_ref, lnw_ref, lnb_ref, ws_ref, bs_ref, gn_ref,
                 w_out_ref, post_mix_ref, pre_mlp_ref, w_up_ref, w_down_ref, post_mlp_ref,
                 cos_ref, sin_ref, mask_ref, qdec_ref, kdec_ref, cdec_ref,
                 y_ref, s_ref, mix_scr, x1_scr, hn_scr, *, with_mixer, with_mlp):
    slot = lax.rem(pl.program_id(0), 2)
    n_chunks = ROW_TILE // CHUNK
    chunk_rows = [slice(c * CHUNK, (c + 1) * CHUNK) for c in range(n_chunks)]
    hn_prev = hn_scr.at[1 - slot]
    acc = []

    def mlp_piece(j):
        if not with_mlp:
            return
        cols = slice(j * FF_TILE, (j + 1) * FF_TILE)
        u = jnp.maximum(_dot(hn_prev[...], w_up_ref[:, cols]), 0.0)
        part = _dot((u * u).astype(BF16), w_down_ref[cols, :])
        acc[:] = [part if not acc else acc[0] + part]

    def mlp_finish():
        if with_mlp:
            y_ref[...] = x1_scr[1 - slot] + _rms(acc[0], post_mlp_ref[...])

    if not with_mixer:
        for j in range(D_FF // FF_TILE):
            mlp_piece(j)
        mlp_finish()
        return

    x = x_ref[...]
    mlp_piece(0)
    h = _rms(x, pre_mix_ref[...]).astype(BF16)
    z = _dot(h, w_in_ref[...])
    cosf = cos_ref[...]
    sinf = sin_ref[...]

    def head_inputs(hh):
        base = 2 * A_WIDTH + hh * B_DK
        q = _rope(z[:, base:base + B_DK], cosf, sinf)
        k = _rope(z[:, base + B_WIDTH:base + B_WIDTH + B_DK], cosf, sinf) * (B_DK ** -0.5)
        v = z[:, base + 2 * B_WIDTH:base + 2 * B_WIDTH + B_DV].astype(BF16)
        return {
            "q": q.astype(BF16),
            "k": k.astype(BF16),
            "v": v,
            "qd": [(q[r] * qdec_ref[hh]).astype(BF16) for r in chunk_rows],
            "kd": [(k[r] * kdec_ref[hh]).astype(BF16) for r in chunk_rows],
        }

    def head_dots_a(hd):
        hd["scores"] = [_dot_nt(hd["q"][r], hd["k"][r]) for r in chunk_rows]
        hd["kv"] = [_dot_tn(hd["kd"][c], hd["v"][r]) for c, r in enumerate(chunk_rows)]

    def head_chain(hh, hd):
        s = s_ref[hh]
        hd["s"] = []
        for c in range(n_chunks):
            hd["s"].append(s.astype(BF16))
            s = s * cdec_ref[hh] + hd["kv"][c]
        s_ref[hh] = s
        hd["p"] = [(hd["scores"][c] * mask_ref[hh]).astype(BF16) for c in range(n_chunks)]

    def head_dots_b(hd):
        hd["o"] = [_dot(hd["p"][c], hd["v"][r]) + _dot(hd["qd"][c], hd["s"][c])
                   for c, r in enumerate(chunk_rows)]

    def head_finish(hh, hd):
        base = 2 * A_WIDTH + 3 * B_WIDTH + hh * B_DV
        gate = _silu(z[:, base:base + B_DV])
        gnw = gn_ref[:, hh * B_DV:(hh + 1) * B_DV]
        for c, r in enumerate(chunk_rows):
            o = _center_norm(hd["o"][c]) * gnw
            mix_scr[r, A_WIDTH + hh * B_DV:A_WIDTH + (hh + 1) * B_DV] = (o * gate[r]).astype(BF16)

    mlp_piece(1)
    ua, va = [], []
    for g in range(A_GROUPS):
        gc = slice(g * A_DIM, (g + 1) * A_DIM)
        ua.append(_gelu(z[:, g * A_DIM:(g + 1) * A_DIM]))
        vg = _gelu(z[:, A_WIDTH + g * A_DIM:A_WIDTH + (g + 1) * A_DIM])
        va.append((_center_norm(vg) * lnw_ref[:, gc] + lnb_ref[:, gc]).astype(BF16))
        if g == 1:
            mlp_piece(2)
    heads = [head_inputs(0)]

    row = lax.broadcasted_iota(jnp.int32, (CHUNK, CHUNK), 0)
    col = lax.broadcasted_iota(jnp.int32, (CHUNK, CHUNK), 1)
    mixed = []
    for g in range(A_GROUPS):
        w_tril = jnp.where(row >= col, ws_ref[g], 0.0).astype(BF16)
        mixed.append([_dot(w_tril, va[g][r]) for r in chunk_rows])
    head_dots_a(heads[0])

    for hh in range(B_HEADS):
        mlp_piece(3 + hh)
        if hh == 0:
            for g in range(A_GROUPS):
                gc = slice(g * A_DIM, (g + 1) * A_DIM)
                bias = jnp.broadcast_to(bs_ref[:, g:g + 1], (CHUNK, A_DIM))
                for c, r in enumerate(chunk_rows):
                    mix_scr[r, gc] = (ua[g][r] * (mixed[g][c] + bias)).astype(BF16)
        else:
            head_finish(hh - 1, heads[hh - 1])
        head_chain(hh, heads[hh])
        if hh + 1 < B_HEADS:
            heads.append(head_inputs(hh + 1))
        head_dots_b(heads[hh])
        if hh + 1 < B_HEADS:
            head_dots_a(heads[hh + 1])

    mlp_piece(7)
    head_finish(B_HEADS - 1, heads[B_HEADS - 1])
    half = ROW_TILE // 2
    mix_a = _dot(mix_scr[:half, :], w_out_ref[...])
    mlp_finish()
    mix_b = _dot(mix_scr[half:, :], w_out_ref[...])
    for rows, mix in ((slice(0, half), mix_a), (slice(half, ROW_TILE), mix_b)):
        x1 = x[rows] + _rms(mix, post_mix_ref[...])
        x1_scr[slot, rows, :] = x1
        hn_scr[slot, rows, :] = _rms(x1, pre_mlp_ref[...]).astype(BF16)


def _resident(shape):
    zeros = (0,) * len(shape)
    return pl.BlockSpec(shape, lambda i: zeros, pipeline_mode=pl.Buffered(1))


def _prompt_layer(x, p, consts):
    batch, seq, _ = x.shape
    tiles_per_seq = seq // ROW_TILE
    n_tiles = batch * tiles_per_seq

    def mixer_tile(i):
        return jnp.minimum(i, n_tiles - 1)

    def mlp_tile(i):
        return jnp.maximum(i - 1, 0)

    x_spec = pl.BlockSpec((None, ROW_TILE, D_MODEL),
                          lambda i: (mixer_tile(i) // tiles_per_seq, mixer_tile(i) % tiles_per_seq, 0))
    y_spec = pl.BlockSpec((None, ROW_TILE, D_MODEL),
                          lambda i: (mlp_tile(i) // tiles_per_seq, mlp_tile(i) % tiles_per_seq, 0))
    tab_spec = pl.BlockSpec((ROW_TILE, B_DK), lambda i: (mixer_tile(i) % tiles_per_seq, 0))
    in_specs = [
        x_spec,
        _resident((1, D_MODEL)),
        _resident((D_MODEL, IN_COLS)),
        _resident((1, A_WIDTH)),
        _resident((1, A_WIDTH)),
        _resident((A_GROUPS, CHUNK, CHUNK)),
        _resident((CHUNK, A_GROUPS)),
        _resident((1, B_WIDTH)),
        _resident((D_MODEL, D_MODEL)),
        _resident((1, D_MODEL)),
        _resident((1, D_MODEL)),
        _resident((D_MODEL, D_FF)),
        _resident((D_FF, D_MODEL)),
        _resident((1, D_MODEL)),
        tab_spec,
        tab_spec,
        _resident((B_HEADS, CHUNK, CHUNK)),
        _resident((B_HEADS, CHUNK, B_DK)),
        _resident((B_HEADS, CHUNK, B_DK)),
        _resident((B_HEADS, 1, B_DV)),
    ]
    out_specs = [
        y_spec,
        pl.BlockSpec((None, B_HEADS, B_DK, B_DV),
                     lambda i: (mixer_tile(i) // tiles_per_seq, 0, 0, 0)),
    ]
    return pl.pallas_call(
        functools.partial(_prompt_kernel, tiles_per_seq=tiles_per_seq, n_tiles=n_tiles),
        grid=(n_tiles + 1,),
        in_specs=in_specs,
        out_specs=out_specs,
        out_shape=[
            jax.ShapeDtypeStruct(x.shape, F32),
            jax.ShapeDtypeStruct((batch, B_HEADS, B_DK, B_DV), F32),
        ],
        scratch_shapes=[pltpu.VMEM((ROW_TILE, D_MODEL), BF16),
                        pltpu.VMEM((2, ROW_TILE, D_MODEL), F32),
                        pltpu.VMEM((2, ROW_TILE, D_MODEL), BF16)],
        compiler_params=pltpu.CompilerParams(
            dimension_semantics=("arbitrary",),
            vmem_limit_bytes=VMEM_LIMIT_BYTES),
        name="prompt_layer",
    )(x, p["pre_mix_w"], p["w_in"], p["ln_v_w"], p["ln_v_b"], p["w_s"], p["b_s_t"], p["gn_w"],
      p["w_out"], p["post_mix_w"], p["pre_mlp_w"], p["w_up"], p["w_down"], p["post_mlp_w"],
      consts["cos"], consts["sin"], consts["mask"], consts["qdec"], consts["kdec"], consts["cdec"])


def _decode_in_kernel(x_ref, pre_mix_ref, w_in_ref, lnw_ref, lnb_ref, ws0_ref, bs0_ref,
                      cos_ref, sin_ref,
                      outa_ref, va_ref, q_ref, k_ref, v_ref, gate_ref):
    h = _rms(x_ref[...], pre_mix_ref[...]).astype(BF16)
    z = _dot(h, w_in_ref[...])
    cosf = cos_ref[...]
    sinf = sin_ref[...]
    for g in range(A_GROUPS):
        gc = slice(g * A_DIM, (g + 1) * A_DIM)
        ua = _gelu(z[:, g * A_DIM:(g + 1) * A_DIM])
        va = _gelu(z[:, A_WIDTH + g * A_DIM:A_WIDTH + (g + 1) * A_DIM])
        va = _center_norm(va) * lnw_ref[:, gc] + lnb_ref[:, gc]
        va_ref[:, gc] = va
        outa_ref[:, gc] = ua * (ws0_ref[:, gc] * va + bs0_ref[:, gc])
    for hh in range(B_HEADS):
        hc = slice(hh * B_DK, (hh + 1) * B_DK)
        base = 2 * A_WIDTH
        q_ref[:, hc] = _rope(z[:, base + hh * B_DK:base + (hh + 1) * B_DK], cosf, sinf)
        k_ref[:, hc] = _rope(z[:, base + B_WIDTH + hh * B_DK:base + B_WIDTH + (hh + 1) * B_DK],
                             cosf, sinf) * (B_DK ** -0.5)
    v_ref[...] = z[:, 2 * A_WIDTH + 2 * B_WIDTH:2 * A_WIDTH + 3 * B_WIDTH]
    gate_ref[...] = _silu(z[:, 2 * A_WIDTH + 3 * B_WIDTH:])


def _decode_state_kernel(q_ref, k_ref, v_ref, s_ref, gam_ref, o_ref, s_out_ref):
    rid = lax.broadcasted_iota(jnp.int32, (STATE_ROWS, B_DK), 0)
    for hh in range(B_HEADS):
        hc = slice(hh * B_DK, (hh + 1) * B_DK)
        q = q_ref[:, hc]
        k = k_ref[:, hc]
        v = v_ref[:, hc]
        gam = gam_ref[hh]
        qk = jnp.sum(q * k, axis=-1, keepdims=True)
        vb = v.astype(BF16)
        qg = (q * gam).astype(BF16)
        cross_rows = []
        for j in range(STATE_ROWS):
            s = s_ref[j, hh]
            cross_rows.append(_dot(qg, s.astype(BF16))[j:j + 1, :])
            kj = jnp.where(rid == j, k, 0.0).astype(BF16)
            s_out_ref[j, hh] = s * gam + _dot_tn(kj, vb)
        o_ref[:, hc] = qk * v + jnp.concatenate(cross_rows, axis=0)


def _decode_out_kernel(x_ref, outa_ref, o_ref, gate_ref, gn_ref, w_out_ref, post_mix_ref,
                       pre_mlp_ref, w_up_ref, w_down_ref, post_mlp_ref, y_ref):
    parts = [outa_ref[...].astype(BF16)]
    for hh in range(B_HEADS):
        hc = slice(hh * B_DV, (hh + 1) * B_DV)
        o = _center_norm(o_ref[:, hc]) * gn_ref[:, hc]
        parts.append((o * gate_ref[:, hc]).astype(BF16))
    mix = _dot(jnp.concatenate(parts, axis=-1), w_out_ref[...])
    x = x_ref[...]
    x1 = x + _rms(mix, post_mix_ref[...])
    y_ref[...] = _mlp(x1, pre_mlp_ref[...], w_up_ref, w_down_ref, post_mlp_ref[...])


def _full(shape):
    zeros = (0,) * len(shape)
    return pl.BlockSpec(shape, lambda *_: zeros)


def _decode_layer(x, s0, p, consts):
    n = x.shape[0]
    act = jax.ShapeDtypeStruct((n, A_WIDTH), F32)
    outa, va, q, k, v, gate = pl.pallas_call(
        _decode_in_kernel,
        grid=(1,),
        in_specs=[_full((n, D_MODEL)), _full((1, D_MODEL)), _full((D_MODEL, IN_COLS)),
                  _full((1, A_WIDTH)), _full((1, A_WIDTH)), _full((1, A_WIDTH)), _full((1, A_WIDTH)),
                  _full((1, B_DK)), _full((1, B_DK))],
        out_specs=[_full((n, A_WIDTH))] * 6,
        out_shape=[act] * 6,
        compiler_params=pltpu.CompilerParams(vmem_limit_bytes=VMEM_LIMIT_BYTES),
        name="decode_in",
    )(x, p["pre_mix_w"], p["w_in"], p["ln_v_w"], p["ln_v_b"], p["w_s0"], p["b_s0"],
      consts["cos1"], consts["sin1"])

    row_spec = pl.BlockSpec((STATE_ROWS, B_WIDTH), lambda i: (i, 0))
    st_spec = pl.BlockSpec((STATE_ROWS, B_HEADS, B_DK, B_DV), lambda i: (i, 0, 0, 0))
    o, s_new = pl.pallas_call(
        _decode_state_kernel,
        grid=(n // STATE_ROWS,),
        in_specs=[row_spec, row_spec, row_spec, st_spec, _full((B_HEADS, 1, B_DV))],
        out_specs=[row_spec, st_spec],
        out_shape=[act, jax.ShapeDtypeStruct(s0.shape, F32)],
        compiler_params=pltpu.CompilerParams(
            dimension_semantics=("arbitrary",), vmem_limit_bytes=VMEM_LIMIT_BYTES),
        name="decode_state",
    )(q, k, v, s0, consts["gam1"])

    y = pl.pallas_call(
        _decode_out_kernel,
        grid=(1,),
        in_specs=[_full((n, D_MODEL)), _full((n, A_WIDTH)), _full((n, B_WIDTH)), _full((n, B_WIDTH)),
                  _full((1, B_WIDTH)), _full((D_MODEL, D_MODEL)), _full((1, D_MODEL)),
                  _full((1, D_MODEL)), _full((D_MODEL, D_FF)), _full((D_FF, D_MODEL)),
                  _full((1, D_MODEL))],
        out_specs=_full((n, D_MODEL)),
        out_shape=jax.ShapeDtypeStruct((n, D_MODEL), F32),
        compiler_params=pltpu.CompilerParams(vmem_limit_bytes=VMEM_LIMIT_BYTES),
        name="decode_out",
    )(x, outa, o, gate, p["gn_w"], p["w_out"], p["post_mix_w"], p["pre_mlp_w"],
      p["w_up"], p["w_down"], p["post_mlp_w"])
    return y, s_new, va


def _rope_tables(pos):
    half = B_DK // 2
    inv = ROPE_BASE ** (-jnp.arange(half, dtype=F32) / half)
    ang = pos.astype(F32)[:, None] * inv[None, :]
    cos, sin = jnp.cos(ang), jnp.sin(ang)
    return jnp.concatenate([cos, cos], axis=-1), jnp.concatenate([-sin, sin], axis=-1)


def _decay_tables(c):
    lg = jnp.log(1.0 - 2.0 ** (-5.0 - jnp.arange(B_HEADS, dtype=F32)))
    idx = jnp.arange(c, dtype=F32)
    diff = idx[:, None] - idx[None, :]
    mask = jnp.where(diff[None] >= 0, jnp.exp(lg[:, None, None] * jnp.maximum(diff, 0.0)[None]), 0.0)
    qdec = jnp.exp(lg[:, None] * (idx[None, :] + 1.0))
    kdec = jnp.exp(lg[:, None] * (c - 1.0 - idx[None, :]))
    cdec = jnp.exp(lg * c)
    return mask, qdec, kdec, cdec


def kernel(x_prompt, x_sample, state_ret, pre_mix_w, w_in, ln_v_w, ln_v_b, w_s, b_s, gn_w,
           w_out, post_mix_w, pre_mlp_w, w_up, w_down, post_mlp_w):
    depth = w_in.shape[0]
    seq = x_prompt.shape[1]
    n_dec, dec_seq, _ = x_sample.shape
    assert dec_seq == 1 and seq % ROW_TILE == 0 and n_dec % STATE_ROWS == 0

    cos, sin = _rope_tables(jnp.arange(seq, dtype=jnp.int32))
    mask, qdec, kdec, cdec = _decay_tables(CHUNK)
    cos1, sin1 = _rope_tables(PAST_LEN + jnp.arange(1, dtype=jnp.int32))
    _, qdec1, _, _ = _decay_tables(1)
    consts = {
        "cos": cos, "sin": sin, "mask": mask,
        "qdec": jnp.broadcast_to(qdec[:, :, None], (B_HEADS, CHUNK, B_DK)),
        "kdec": jnp.broadcast_to(kdec[:, :, None], (B_HEADS, CHUNK, B_DK)),
        "cdec": jnp.broadcast_to(cdec[:, None, None], (B_HEADS, 1, B_DV)),
        "cos1": cos1, "sin1": sin1,
        "gam1": jnp.broadcast_to(qdec1[:, :, None], (B_HEADS, 1, B_DV)),
    }

    yp, ys = x_prompt, x_sample.reshape(n_dec, D_MODEL)
    sp_list, ss_list, vs_list = [], [], []
    for l in range(depth):
        p = {
            "pre_mix_w": pre_mix_w[l][None, :],
            "w_in": w_in[l].astype(BF16),
            "ln_v_w": ln_v_w[l].reshape(1, A_WIDTH),
            "ln_v_b": ln_v_b[l].reshape(1, A_WIDTH),
            "w_s": w_s[l],
            "b_s_t": b_s[l].T,
            "w_s0": jnp.repeat(w_s[l][:, 0, 0], A_DIM)[None, :],
            "b_s0": jnp.repeat(b_s[l][:, 0], A_DIM)[None, :],
            "gn_w": gn_w[l].reshape(1, B_WIDTH),
            "w_out": w_out[l].astype(BF16),
            "post_mix_w": post_mix_w[l][None, :],
            "pre_mlp_w": pre_mlp_w[l][None, :],
            "w_up": w_up[l].astype(BF16),
            "w_down": w_down[l].astype(BF16),
            "post_mlp_w": post_mlp_w[l][None, :],
        }
        yp, sp = _prompt_layer(yp, p, consts)
        ys, ss, vs = _decode_layer(ys, state_ret[l], p, consts)
        sp_list.append(sp)
        ss_list.append(ss)
        vs_list.append(vs.reshape(n_dec, 1, A_GROUPS, A_DIM))
    return (yp, ys.reshape(n_dec, 1, D_MODEL), jnp.stack(sp_list, axis=0),
            jnp.stack(ss_list, axis=0), jnp.stack(vs_list, axis=0))
```

```python
import functools
import math

import jax
import jax.numpy as jnp
from jax import lax
from jax.experimental import pallas as pl
from jax.experimental.pallas import tpu as pltpu

D_MODEL = 1024
A_WIDTH = 512
B_WIDTH = 512
A_GROUPS = 4
A_DIM = 128
CHUNK = 128
B_HEADS = 4
B_DK = 128
B_DV = 128
D_FF = 4096
IN_COLS = 2 * A_WIDTH + 4 * B_WIDTH
ROPE_BASE = 10000.0
EPS = 1e-6
PAST_LEN = 16384
LOG_GAMMA = [math.log(1.0 - 2.0 ** (-5.0 - h)) for h in range(B_HEADS)]

ROW_TILE = 512
FF_TILE = 512
STATE_ROWS = 16
VMEM_LIMIT_BYTES = 56 * 1024 * 1024

F32 = jnp.float32
BF16 = jnp.bfloat16


def _rms(x, w):
    return x * lax.rsqrt(jnp.mean(x * x, axis=-1, keepdims=True) + EPS) * w


def _center_norm(x):
    mu = jnp.mean(x, axis=-1, keepdims=True)
    xc = x - mu
    return xc * lax.rsqrt(jnp.mean(xc * xc, axis=-1, keepdims=True) + EPS)


def _gelu(x):
    return jax.nn.gelu(x, approximate=True)


def _silu(x):
    return x * (1.0 / (1.0 + jnp.exp(-x)))


def _rope(x, cosf, sinf):
    return x * cosf + pltpu.roll(x, B_DK // 2, axis=1) * sinf


def _rope_rows(pos, inv):
    ang = pos * inv
    lane = lax.broadcasted_iota(jnp.int32, ang.shape, 1)
    sin = jnp.sin(ang)
    return jnp.cos(ang), jnp.where(lane < B_DK // 2, -sin, sin)


def _dot(a, b):
    return jnp.dot(a, b, preferred_element_type=F32)


def _dot_nt(a, b):
    return lax.dot_general(a, b, (((1,), (1,)), ((), ())), preferred_element_type=F32)


def _dot_tn(a, b):
    return lax.dot_general(a, b, (((0,), (0,)), ((), ())), preferred_element_type=F32)


def _mlp(x1, pre_mlp_w, w_up_ref, w_down_ref, post_mlp_w):
    hn = _rms(x1, pre_mlp_w).astype(BF16)
    f = None
    for j in range(D_FF // FF_TILE):
        cols = slice(j * FF_TILE, (j + 1) * FF_TILE)
        u = jnp.maximum(_dot(hn, w_up_ref[:, cols]), 0.0)
        part = _dot((u * u).astype(BF16), w_down_ref[cols, :])
        f = part if f is None else f + part
    return x1 + _rms(f, post_mlp_w)


def _fill_tables(inv_ref, cos_ref, sin_ref, mask_ref, qdec_ref, kdec_ref):
    seq = cos_ref.shape[0]
    for r0 in range(0, seq, ROW_TILE):
        pos = (lax.broadcasted_iota(jnp.int32, (ROW_TILE, B_DK), 0) + r0).astype(F32)
        cosf, sinf = _rope_rows(pos, inv_ref[...])
        cos_ref[r0:r0 + ROW_TILE, :] = cosf
        sin_ref[r0:r0 + ROW_TILE, :] = sinf
    row = lax.broadcasted_iota(jnp.int32, (CHUNK, CHUNK), 0).astype(F32)
    col = lax.broadcasted_iota(jnp.int32, (CHUNK, CHUNK), 1).astype(F32)
    diff = row - col
    for hh in range(B_HEADS):
        lg = LOG_GAMMA[hh]
        mask_ref[hh] = jnp.where(diff >= 0, jnp.exp(lg * jnp.maximum(diff, 0.0)), 0.0)
        qdec_ref[hh] = jnp.exp(lg * (row + 1.0))
        kdec_ref[hh] = jnp.exp(lg * (CHUNK - 1.0 - row))


def _prompt_kernel(x_ref, pre_mix_ref, w_in_ref, lnw_ref, lnb_ref, ws_ref, bs_ref, gn_ref,
                   w_out_ref, post_mix_ref, pre_mlp_ref, w_up_ref, w_down_ref, post_mlp_ref,
                   inv_ref, y_ref, s_ref, mix_scr, x1_scr, hn_scr,
                   cos_ref, sin_ref, mask_ref, qdec_ref, kdec_ref, *, tiles_per_seq, n_tiles):
    i = pl.program_id(0)
    refs = (x_ref, pre_mix_ref, w_in_ref, lnw_ref, lnb_ref, ws_ref, bs_ref, gn_ref,
            w_out_ref, post_mix_ref, pre_mlp_ref, w_up_ref, w_down_ref, post_mlp_ref,
            y_ref, s_ref, mix_scr, x1_scr, hn_scr,
            cos_ref, sin_ref, mask_ref, qdec_ref, kdec_ref)

    @pl.when(i == 0)
    def _():
        _fill_tables(inv_ref, cos_ref, sin_ref, mask_ref, qdec_ref, kdec_ref)

    @pl.when(jnp.logical_and(lax.rem(i, tiles_per_seq) == 0, i < n_tiles))
    def _():
        s_ref[...] = jnp.zeros_like(s_ref)

    @pl.when(i == 0)
    def _():
        _prompt_step(*refs, tiles_per_seq=tiles_per_seq, with_mixer=True, with_mlp=False)

    @pl.when(jnp.logical_and(i > 0, i < n_tiles))
    def _():
        _prompt_step(*refs, tiles_per_seq=tiles_per_seq, with_mixer=True, with_mlp=True)

    @pl.when(i == n_tiles)
    def _():
        _prompt_step(*refs, tiles_per_seq=tiles_per_seq, with_mixer=False, with_mlp=True)


def _prompt_step(x_ref, pre_mix_ref, w_in_ref, lnw_ref, lnb_ref, ws_ref, bs_ref, gn_ref,
                 w_out_ref, post_mix_ref, pre_mlp_ref, w_up_ref, w_down_ref, post_mlp_ref,
                 y_ref, s_ref, mix_scr, x1_scr, hn_scr,
                 cos_ref, sin_ref, mask_ref, qdec_ref, kdec_ref, *, tiles_per_seq, with_mixer, with_mlp):
    slot = lax.rem(pl.program_id(0), 2)
    n_chunks = ROW_TILE // CHUNK
    chunk_rows = [slice(c * CHUNK, (c + 1) * CHUNK) for c in range(n_chunks)]
    hn_prev = hn_scr.at[1 - slot]
    acc = []

    def mlp_piece(j):
        if not with_mlp:
            return
        cols = slice(j * FF_TILE, (j + 1) * FF_TILE)
        u = jnp.maximum(_dot(hn_prev[...], w_up_ref[:, cols]), 0.0)
        part = _dot((u * u).astype(BF16), w_down_ref[cols, :])
        acc[:] = [part if not acc else acc[0] + part]

    def mlp_finish():
        if with_mlp:
            y_ref[...] = x1_scr[1 - slot] + _rms(acc[0], post_mlp_ref[...])

    if not with_mixer:
        for j in range(D_FF // FF_TILE):
            mlp_piece(j)
        mlp_finish()
        return

    x = x_ref[...]
    mlp_piece(0)
    h = _rms(x, pre_mix_ref[...]).astype(BF16)
    z = _dot(h, w_in_ref[...])
    tile_pos = pl.multiple_of(lax.rem(pl.program_id(0), tiles_per_seq) * ROW_TILE, ROW_TILE)
    cosf = cos_ref[pl.ds(tile_pos, ROW_TILE), :]
    sinf = sin_ref[pl.ds(tile_pos, ROW_TILE), :]

    def head_inputs(hh):
        base = 2 * A_WIDTH + hh * B_DK
        q = _rope(z[:, base:base + B_DK], cosf, sinf)
        k = _rope(z[:, base + B_WIDTH:base + B_WIDTH + B_DK], cosf, sinf) * (B_DK ** -0.5)
        v = z[:, base + 2 * B_WIDTH:base + 2 * B_WIDTH + B_DV].astype(BF16)
        return {
            "q": q.astype(BF16),
            "k": k.astype(BF16),
            "v": v,
            "qd": [(q[r] * qdec_ref[hh]).astype(BF16) for r in chunk_rows],
            "kd": [(k[r] * kdec_ref[hh]).astype(BF16) for r in chunk_rows],
        }

    def head_dots_a(hd):
        hd["scores"] = [_dot_nt(hd["q"][r], hd["k"][r]) for r in chunk_rows]
        hd["kv"] = [_dot_tn(hd["kd"][c], hd["v"][r]) for c, r in enumerate(chunk_rows)]

    def head_chain(hh, hd):
        s = s_ref[hh]
        hd["s"] = []
        for c in range(n_chunks):
            hd["s"].append(s.astype(BF16))
            s = s * math.exp(LOG_GAMMA[hh] * CHUNK) + hd["kv"][c]
        s_ref[hh] = s
        hd["p"] = [(hd["scores"][c] * mask_ref[hh]).astype(BF16) for c in range(n_chunks)]

    def head_dots_b(hd):
        hd["o"] = [_dot(hd["p"][c], hd["v"][r]) + _dot(hd["qd"][c], hd["s"][c])
                   for c, r in enumerate(chunk_rows)]

    def head_finish(hh, hd):
        base = 2 * A_WIDTH + 3 * B_WIDTH + hh * B_DV
        gate = _silu(z[:, base:base + B_DV])
        gnw = gn_ref[:, hh * B_DV:(hh + 1) * B_DV]
        for c, r in enumerate(chunk_rows):
            o = _center_norm(hd["o"][c]) * gnw
            mix_scr[r, A_WIDTH + hh * B_DV:A_WIDTH + (hh + 1) * B_DV] = (o * gate[r]).astype(BF16)

    mlp_piece(1)
    ua, va = [], []
    for g in range(A_GROUPS):
        gc = slice(g * A_DIM, (g + 1) * A_DIM)
        ua.append(_gelu(z[:, g * A_DIM:(g + 1) * A_DIM]))
        vg = _gelu(z[:, A_WIDTH + g * A_DIM:A_WIDTH + (g + 1) * A_DIM])
        va.append((_center_norm(vg) * lnw_ref[:, gc] + lnb_ref[:, gc]).astype(BF16))
        if g == 1:
            mlp_piece(2)
    heads = [head_inputs(0)]

    row = lax.broadcasted_iota(jnp.int32, (CHUNK, CHUNK), 0)
    col = lax.broadcasted_iota(jnp.int32, (CHUNK, CHUNK), 1)
    mixed = []
    for g in range(A_GROUPS):
        w_tril = jnp.where(row >= col, ws_ref[g], 0.0).astype(BF16)
        mixed.append([_dot(w_tril, va[g][r]) for r in chunk_rows])
    head_dots_a(heads[0])

    for hh in range(B_HEADS):
        mlp_piece(3 + hh)
        if hh == 0:
            for g in range(A_GROUPS):
                gc = slice(g * A_DIM, (g + 1) * A_DIM)
                bias = jnp.broadcast_to(bs_ref[:, g:g + 1], (CHUNK, A_DIM))
                for c, r in enumerate(chunk_rows):
                    mix_scr[r, gc] = (ua[g][r] * (mixed[g][c] + bias)).astype(BF16)
        else:
            head_finish(hh - 1, heads[hh - 1])
        head_chain(hh, heads[hh])
        if hh + 1 < B_HEADS:
            heads.append(head_inputs(hh + 1))
        head_dots_b(heads[hh])
        if hh + 1 < B_HEADS:
            head_dots_a(heads[hh + 1])

    mlp_piece(7)
    head_finish(B_HEADS - 1, heads[B_HEADS - 1])
    half = ROW_TILE // 2
    mix_a = _dot(mix_scr[:half, :], w_out_ref[...])
    mlp_finish()
    mix_b = _dot(mix_scr[half:, :], w_out_ref[...])
    for rows, mix in ((slice(0, half), mix_a), (slice(half, ROW_TILE), mix_b)):
        x1 = x[rows] + _rms(mix, post_mix_ref[...])
        x1_scr[slot, rows, :] = x1
        hn_scr[slot, rows, :] = _rms(x1, pre_mlp_ref[...]).astype(BF16)


def _resident(shape):
    zeros = (0,) * len(shape)
    return pl.BlockSpec(shape, lambda i: zeros, pipeline_mode=pl.Buffered(1))


def _prompt_layer(x, p, inv_freq):
    batch, seq, _ = x.shape
    tiles_per_seq = seq // ROW_TILE
    n_tiles = batch * tiles_per_seq

    def mixer_tile(i):
        return jnp.minimum(i, n_tiles - 1)

    def mlp_tile(i):
        return jnp.maximum(i - 1, 0)

    x_spec = pl.BlockSpec((None, ROW_TILE, D_MODEL),
                          lambda i: (mixer_tile(i) // tiles_per_seq, mixer_tile(i) % tiles_per_seq, 0))
    y_spec = pl.BlockSpec((None, ROW_TILE, D_MODEL),
                          lambda i: (mlp_tile(i) // tiles_per_seq, mlp_tile(i) % tiles_per_seq, 0))
    in_specs = [
        x_spec,
        _resident((1, D_MODEL)),
        _resident((D_MODEL, IN_COLS)),
        _resident((1, A_WIDTH)),
        _resident((1, A_WIDTH)),
        _resident((A_GROUPS, CHUNK, CHUNK)),
        _resident((CHUNK, A_GROUPS)),
        _resident((1, B_WIDTH)),
        _resident((D_MODEL, D_MODEL)),
        _resident((1, D_MODEL)),
        _resident((1, D_MODEL)),
        _resident((D_MODEL, D_FF)),
        _resident((D_FF, D_MODEL)),
        _resident((1, D_MODEL)),
        _resident((1, B_DK)),
    ]
    out_specs = [
        y_spec,
        pl.BlockSpec((None, B_HEADS, B_DK, B_DV),
                     lambda i: (mixer_tile(i) // tiles_per_seq, 0, 0, 0)),
    ]
    return pl.pallas_call(
        functools.partial(_prompt_kernel, tiles_per_seq=tiles_per_seq, n_tiles=n_tiles),
        grid=(n_tiles + 1,),
        in_specs=in_specs,
        out_specs=out_specs,
        out_shape=[
            jax.ShapeDtypeStruct(x.shape, F32),
            jax.ShapeDtypeStruct((batch, B_HEADS, B_DK, B_DV), F32),
        ],
        scratch_shapes=[pltpu.VMEM((ROW_TILE, D_MODEL), BF16),
                        pltpu.VMEM((2, ROW_TILE, D_MODEL), F32),
                        pltpu.VMEM((2, ROW_TILE, D_MODEL), BF16),
                        pltpu.VMEM((seq, B_DK), F32),
                        pltpu.VMEM((seq, B_DK), F32),
                        pltpu.VMEM((B_HEADS, CHUNK, CHUNK), F32),
                        pltpu.VMEM((B_HEADS, CHUNK, B_DK), F32),
                        pltpu.VMEM((B_HEADS, CHUNK, B_DK), F32)],
        compiler_params=pltpu.CompilerParams(
            dimension_semantics=("arbitrary",),
            vmem_limit_bytes=VMEM_LIMIT_BYTES),
        name="prompt_layer",
    )(x, p["pre_mix_w"], p["w_in"], p["ln_v_w"], p["ln_v_b"], p["w_s"], p["b_s_t"], p["gn_w"],
      p["w_out"], p["post_mix_w"], p["pre_mlp_w"], p["w_up"], p["w_down"], p["post_mlp_w"],
      inv_freq)


def _decode_in_kernel(x_ref, pre_mix_ref, w_in_ref, lnw_ref, lnb_ref, ws0_ref, bs0_ref,
                      inv_ref,
                      outa_ref, va_ref, q_ref, k_ref, v_ref, gate_ref):
    h = _rms(x_ref[...], pre_mix_ref[...]).astype(BF16)
    z = _dot(h, w_in_ref[...])
    cosf, sinf = _rope_rows(jnp.full((1, B_DK), PAST_LEN, F32), inv_ref[...])
    for g in range(A_GROUPS):
        gc = slice(g * A_DIM, (g + 1) * A_DIM)
        ua = _gelu(z[:, g * A_DIM:(g + 1) * A_DIM])
        va = _gelu(z[:, A_WIDTH + g * A_DIM:A_WIDTH + (g + 1) * A_DIM])
        va = _center_norm(va) * lnw_ref[:, gc] + lnb_ref[:, gc]
        va_ref[:, gc] = va
        outa_ref[:, gc] = ua * (ws0_ref[:, gc] * va + bs0_ref[:, gc])
    for hh in range(B_HEADS):
        hc = slice(hh * B_DK, (hh + 1) * B_DK)
        base = 2 * A_WIDTH
        q_ref[:, hc] = _rope(z[:, base + hh * B_DK:base + (hh + 1) * B_DK], cosf, sinf)
        k_ref[:, hc] = _rope(z[:, base + B_WIDTH + hh * B_DK:base + B_WIDTH + (hh + 1) * B_DK],
                             cosf, sinf) * (B_DK ** -0.5)
    v_ref[...] = z[:, 2 * A_WIDTH + 2 * B_WIDTH:2 * A_WIDTH + 3 * B_WIDTH]
    gate_ref[...] = _silu(z[:, 2 * A_WIDTH + 3 * B_WIDTH:])


def _decode_state_kernel(q_ref, k_ref, v_ref, s_ref, o_ref, s_out_ref):
    rid = lax.broadcasted_iota(jnp.int32, (STATE_ROWS, B_DK), 0)
    for hh in range(B_HEADS):
        hc = slice(hh * B_DK, (hh + 1) * B_DK)
        q = q_ref[:, hc]
        k = k_ref[:, hc]
        v = v_ref[:, hc]
        gam = math.exp(LOG_GAMMA[hh])
        qk = jnp.sum(q * k, axis=-1, keepdims=True)
        vb = v.astype(BF16)
        qg = (q * gam).astype(BF16)
        cross_rows = []
        for j in range(STATE_ROWS):
            s = s_ref[j, hh]
            cross_rows.append(_dot(qg, s.astype(BF16))[j:j + 1, :])
            kj = jnp.where(rid == j, k, 0.0).astype(BF16)
            s_out_ref[j, hh] = s * gam + _dot_tn(kj, vb)
        o_ref[:, hc] = qk * v + jnp.concatenate(cross_rows, axis=0)


def _decode_out_kernel(x_ref, outa_ref, o_ref, gate_ref, gn_ref, w_out_ref, post_mix_ref,
                       pre_mlp_ref, w_up_ref, w_down_ref, post_mlp_ref, y_ref):
    parts = [outa_ref[...].astype(BF16)]
    for hh in range(B_HEADS):
        hc = slice(hh * B_DV, (hh + 1) * B_DV)
        o = _center_norm(o_ref[:, hc]) * gn_ref[:, hc]
        parts.append((o * gate_ref[:, hc]).astype(BF16))
    mix = _dot(jnp.concatenate(parts, axis=-1), w_out_ref[...])
    x = x_ref[...]
    x1 = x + _rms(mix, post_mix_ref[...])
    y_ref[...] = _mlp(x1, pre_mlp_ref[...], w_up_ref, w_down_ref, post_mlp_ref[...])


def _full(shape):
    zeros = (0,) * len(shape)
    return pl.BlockSpec(shape, lambda *_: zeros)


def _decode_layer(x, s0, p, inv_freq):
    n = x.shape[0]
    act = jax.ShapeDtypeStruct((n, A_WIDTH), F32)
    outa, va, q, k, v, gate = pl.pallas_call(
        _decode_in_kernel,
        grid=(1,),
        in_specs=[_full((n, D_MODEL)), _full((1, D_MODEL)), _full((D_MODEL, IN_COLS)),
                  _full((1, A_WIDTH)), _full((1, A_WIDTH)), _full((1, A_WIDTH)), _full((1, A_WIDTH)),
                  _full((1, B_DK))],
        out_specs=[_full((n, A_WIDTH))] * 6,
        out_shape=[act] * 6,
        compiler_params=pltpu.CompilerParams(vmem_limit_bytes=VMEM_LIMIT_BYTES),
        name="decode_in",
    )(x, p["pre_mix_w"], p["w_in"], p["ln_v_w"], p["ln_v_b"], p["w_s0"], p["b_s0"],
      inv_freq)

    row_spec = pl.BlockSpec((STATE_ROWS, B_WIDTH), lambda i: (i, 0))
    st_spec = pl.BlockSpec((STATE_ROWS, B_HEADS, B_DK, B_DV), lambda i: (i, 0, 0, 0))
    o, s_new = pl.pallas_call(
        _decode_state_kernel,
        grid=(n // STATE_ROWS,),
        in_specs=[row_spec, row_spec, row_spec, st_spec],
        out_specs=[row_spec, st_spec],
        out_shape=[act, jax.ShapeDtypeStruct(s0.shape, F32)],
        compiler_params=pltpu.CompilerParams(
            dimension_semantics=("arbitrary",), vmem_limit_bytes=VMEM_LIMIT_BYTES),
        name="decode_state",
    )(q, k, v, s0)

    y = pl.pallas_call(
        _decode_out_kernel,
        grid=(1,),
        in_specs=[_full((n, D_MODEL)), _full((n, A_WIDTH)), _full((n, B_WIDTH)), _full((n, B_WIDTH)),
                  _full((1, B_WIDTH)), _full((D_MODEL, D_MODEL)), _full((1, D_MODEL)),
                  _full((1, D_MODEL)), _full((D_MODEL, D_FF)), _full((D_FF, D_MODEL)),
                  _full((1, D_MODEL))],
        out_specs=_full((n, D_MODEL)),
        out_shape=jax.ShapeDtypeStruct((n, D_MODEL), F32),
        compiler_params=pltpu.CompilerParams(vmem_limit_bytes=VMEM_LIMIT_BYTES),
        name="decode_out",
    )(x, outa, o, gate, p["gn_w"], p["w_out"], p["post_mix_w"], p["pre_mlp_w"],
      p["w_up"], p["w_down"], p["post_mlp_w"])
    return y, s_new, va


def kernel(x_prompt, x_sample, state_ret, pre_mix_w, w_in, ln_v_w, ln_v_b, w_s, b_s, gn_w,
           w_out, post_mix_w, pre_mlp_w, w_up, w_down, post_mlp_w):
    depth = w_in.shape[0]
    seq = x_prompt.shape[1]
    n_dec, dec_seq, _ = x_sample.shape
    assert dec_seq == 1 and seq % ROW_TILE == 0 and n_dec % STATE_ROWS == 0

    half = B_DK // 2
    inv = ROPE_BASE ** (-jnp.arange(half, dtype=F32) / half)
    inv_freq = jnp.concatenate([inv, inv])[None, :]

    yp, ys = x_prompt, x_sample.reshape(n_dec, D_MODEL)
    sp_list, ss_list, vs_list = [], [], []
    for l in range(depth):
        p = {
            "pre_mix_w": pre_mix_w[l][None, :],
            "w_in": w_in[l].astype(BF16),
            "ln_v_w": ln_v_w[l].reshape(1, A_WIDTH),
            "ln_v_b": ln_v_b[l].reshape(1, A_WIDTH),
            "w_s": w_s[l],
            "b_s_t": b_s[l].T,
            "w_s0": jnp.repeat(w_s[l][:, 0, 0], A_DIM)[None, :],
            "b_s0": jnp.repeat(b_s[l][:, 0], A_DIM)[None, :],
            "gn_w": gn_w[l].reshape(1, B_WIDTH),
            "w_out": w_out[l].astype(BF16),
            "post_mix_w": post_mix_w[l][None, :],
            "pre_mlp_w": pre_mlp_w[l][None, :],
            "w_up": w_up[l].astype(BF16),
            "w_down": w_down[l].astype(BF16),
            "post_mlp_w": post_mlp_w[l][None, :],
        }
        yp, sp = _prompt_layer(yp, p, inv_freq)
        ys, ss, vs = _decode_layer(ys, state_ret[l], p, inv_freq)
        sp_list.append(sp)
        ss_list.append(ss)
        vs_list.append(vs.reshape(n_dec, 1, A_GROUPS, A_DIM))
    return (yp, ys.reshape(n_dec, 1, D_MODEL), jnp.stack(sp_list, axis=0),
            jnp.stack(ss_list, axis=0), jnp.stack(vs_list, axis=0))
```

```python
import functools
import math

import jax
import jax.numpy as jnp
from jax import lax
from jax.experimental import pallas as pl
from jax.experimental.pallas import tpu as pltpu

D_MODEL = 1024
A_WIDTH = 512
B_WIDTH = 512
A_GROUPS = 4
A_DIM = 128
CHUNK = 128
B_HEADS = 4
B_DK = 128
B_DV = 128
D_FF = 4096
IN_COLS = 2 * A_WIDTH + 4 * B_WIDTH
ROPE_BASE = 10000.0
EPS = 1e-6
PAST_LEN = 16384
LOG_GAMMA = [math.log(1.0 - 2.0 ** (-5.0 - h)) for h in range(B_HEADS)]

ROW_TILE = 512
FF_TILE = 512
STATE_ROWS = 16
W_IN_ROWS = 256
DEC_FF_TILE = 512
VMEM_LIMIT_BYTES = 56 * 1024 * 1024

F32 = jnp.float32
BF16 = jnp.bfloat16


def _rms(x, w):
    return x * lax.rsqrt(jnp.mean(x * x, axis=-1, keepdims=True) + EPS) * w


def _center_norm(x):
    mu = jnp.mean(x, axis=-1, keepdims=True)
    xc = x - mu
    return xc * lax.rsqrt(jnp.mean(xc * xc, axis=-1, keepdims=True) + EPS)


def _gelu(x):
    return jax.nn.gelu(x, approximate=True)


def _silu(x):
    return x * (1.0 / (1.0 + jnp.exp(-x)))


def _rope(x, cosf, sinf):
    return x * cosf + pltpu.roll(x, B_DK // 2, axis=1) * sinf


def _rope_rows(pos, inv):
    ang = pos * inv
    lane = lax.broadcasted_iota(jnp.int32, ang.shape, 1)
    sin = jnp.sin(ang)
    return jnp.cos(ang), jnp.where(lane < B_DK // 2, -sin, sin)


def _dot(a, b):
    return jnp.dot(a, b, preferred_element_type=F32)


def _dot_nt(a, b):
    return lax.dot_general(a, b, (((1,), (1,)), ((), ())), preferred_element_type=F32)


def _dot_tn(a, b):
    return lax.dot_general(a, b, (((0,), (0,)), ((), ())), preferred_element_type=F32)


def _fill_tables(inv_ref, cos_ref, sin_ref, mask_ref, qdec_ref, kdec_ref):
    seq = cos_ref.shape[0]
    for r0 in range(0, seq, ROW_TILE):
        pos = (lax.broadcasted_iota(jnp.int32, (ROW_TILE, B_DK), 0) + r0).astype(F32)
        cosf, sinf = _rope_rows(pos, inv_ref[...])
        cos_ref[r0:r0 + ROW_TILE, :] = cosf
        sin_ref[r0:r0 + ROW_TILE, :] = sinf
    row = lax.broadcasted_iota(jnp.int32, (CHUNK, CHUNK), 0).astype(F32)
    col = lax.broadcasted_iota(jnp.int32, (CHUNK, CHUNK), 1).astype(F32)
    diff = row - col
    for hh in range(B_HEADS):
        lg = LOG_GAMMA[hh]
        mask_ref[hh] = jnp.where(diff >= 0, jnp.exp(lg * jnp.maximum(diff, 0.0)), 0.0)
        qdec_ref[hh] = jnp.exp(lg * (row + 1.0))
        kdec_ref[hh] = jnp.exp(lg * (CHUNK - 1.0 - row))


def _prompt_kernel(x_ref, pre_mix_ref, w_in_ref, lnw_ref, lnb_ref, ws_ref, bs_ref, gn_ref,
                   w_out_ref, post_mix_ref, pre_mlp_ref, w_up_ref, w_down_ref, post_mlp_ref,
                   inv_ref, y_ref, s_ref, mix_scr, x1_scr, hn_scr,
                   cos_ref, sin_ref, mask_ref, qdec_ref, kdec_ref, *, tiles_per_seq, n_tiles):
    i = pl.program_id(0)
    refs = (x_ref, pre_mix_ref, w_in_ref, lnw_ref, lnb_ref, ws_ref, bs_ref, gn_ref,
            w_out_ref, post_mix_ref, pre_mlp_ref, w_up_ref, w_down_ref, post_mlp_ref,
            y_ref, s_ref, mix_scr, x1_scr, hn_scr,
            cos_ref, sin_ref, mask_ref, qdec_ref, kdec_ref)

    @pl.when(i == 0)
    def _():
        _fill_tables(inv_ref, cos_ref, sin_ref, mask_ref, qdec_ref, kdec_ref)

    @pl.when(jnp.logical_and(lax.rem(i, tiles_per_seq) == 0, i < n_tiles))
    def _():
        s_ref[...] = jnp.zeros_like(s_ref)

    @pl.when(i == 0)
    def _():
        _prompt_step(*refs, tiles_per_seq=tiles_per_seq, with_mixer=True, with_mlp=False)

    @pl.when(jnp.logical_and(i > 0, i < n_tiles))
    def _():
        _prompt_step(*refs, tiles_per_seq=tiles_per_seq, with_mixer=True, with_mlp=True)

    @pl.when(i == n_tiles)
    def _():
        _prompt_step(*refs, tiles_per_seq=tiles_per_seq, with_mixer=False, with_mlp=True)


def _prompt_step(x_ref, pre_mix_ref, w_in_ref, lnw_ref, lnb_ref, ws_ref, bs_ref, gn_ref,
                 w_out_ref, post_mix_ref, pre_mlp_ref, w_up_ref, w_down_ref, post_mlp_ref,
                 y_ref, s_ref, mix_scr, x1_scr, hn_scr,
                 cos_ref, sin_ref, mask_ref, qdec_ref, kdec_ref, *, tiles_per_seq, with_mixer, with_mlp):
    slot = lax.rem(pl.program_id(0), 2)
    n_chunks = ROW_TILE // CHUNK
    chunk_rows = [slice(c * CHUNK, (c + 1) * CHUNK) for c in range(n_chunks)]
    hn_prev = hn_scr.at[1 - slot]
    acc = []

    def mlp_piece(j):
        if not with_mlp:
            return
        cols = slice(j * FF_TILE, (j + 1) * FF_TILE)
        u = jnp.maximum(_dot(hn_prev[...], w_up_ref[:, cols]), 0.0)
        part = _dot((u * u).astype(BF16), w_down_ref[cols, :])
        acc[:] = [part if not acc else acc[0] + part]

    def mlp_finish():
        if with_mlp:
            y_ref[...] = x1_scr[1 - slot] + _rms(acc[0], post_mlp_ref[...])

    if not with_mixer:
        for j in range(D_FF // FF_TILE):
            mlp_piece(j)
        mlp_finish()
        return

    x = x_ref[...]
    mlp_piece(0)
    h = _rms(x, pre_mix_ref[...]).astype(BF16)
    z = _dot(h, w_in_ref[...])
    tile_pos = pl.multiple_of(lax.rem(pl.program_id(0), tiles_per_seq) * ROW_TILE, ROW_TILE)
    cosf = cos_ref[pl.ds(tile_pos, ROW_TILE), :]
    sinf = sin_ref[pl.ds(tile_pos, ROW_TILE), :]

    def head_inputs(hh):
        base = 2 * A_WIDTH + hh * B_DK
        q = _rope(z[:, base:base + B_DK], cosf, sinf)
        k = _rope(z[:, base + B_WIDTH:base + B_WIDTH + B_DK], cosf, sinf) * (B_DK ** -0.5)
        v = z[:, base + 2 * B_WIDTH:base + 2 * B_WIDTH + B_DV].astype(BF16)
        return {
            "q": q.astype(BF16),
            "k": k.astype(BF16),
            "v": v,
            "qd": [(q[r] * qdec_ref[hh]).astype(BF16) for r in chunk_rows],
            "kd": [(k[r] * kdec_ref[hh]).astype(BF16) for r in chunk_rows],
        }

    def head_dots_a(hd):
        hd["scores"] = [_dot_nt(hd["q"][r], hd["k"][r]) for r in chunk_rows]
        hd["kv"] = [_dot_tn(hd["kd"][c], hd["v"][r]) for c, r in enumerate(chunk_rows)]

    def head_chain(hh, hd):
        s = s_ref[hh]
        hd["s"] = []
        for c in range(n_chunks):
            hd["s"].append(s.astype(BF16))
            s = s * math.exp(LOG_GAMMA[hh] * CHUNK) + hd["kv"][c]
        s_ref[hh] = s
        hd["p"] = [(hd["scores"][c] * mask_ref[hh]).astype(BF16) for c in range(n_chunks)]

    def head_dots_b(hd):
        hd["o"] = [_dot(hd["p"][c], hd["v"][r]) + _dot(hd["qd"][c], hd["s"][c])
                   for c, r in enumerate(chunk_rows)]

    def head_finish(hh, hd):
        base = 2 * A_WIDTH + 3 * B_WIDTH + hh * B_DV
        gate = _silu(z[:, base:base + B_DV])
        gnw = gn_ref[:, hh * B_DV:(hh + 1) * B_DV]
        for c, r in enumerate(chunk_rows):
            o = _center_norm(hd["o"][c]) * gnw
            mix_scr[r, A_WIDTH + hh * B_DV:A_WIDTH + (hh + 1) * B_DV] = (o * gate[r]).astype(BF16)

    mlp_piece(1)
    ua, va = [], []
    for g in range(A_GROUPS):
        gc = slice(g * A_DIM, (g + 1) * A_DIM)
        ua.append(_gelu(z[:, g * A_DIM:(g + 1) * A_DIM]))
        vg = _gelu(z[:, A_WIDTH + g * A_DIM:A_WIDTH + (g + 1) * A_DIM])
        va.append((_center_norm(vg) * lnw_ref[:, gc] + lnb_ref[:, gc]).astype(BF16))
        if g == 1:
            mlp_piece(2)
    heads = [head_inputs(0)]

    row = lax.broadcasted_iota(jnp.int32, (CHUNK, CHUNK), 0)
    col = lax.broadcasted_iota(jnp.int32, (CHUNK, CHUNK), 1)
    mixed = []
    for g in range(A_GROUPS):
        w_tril = jnp.where(row >= col, ws_ref[g], 0.0).astype(BF16)
        mixed.append([_dot(w_tril, va[g][r]) for r in chunk_rows])
    head_dots_a(heads[0])

    for hh in range(B_HEADS):
        mlp_piece(3 + hh)
        if hh == 0:
            for g in range(A_GROUPS):
                gc = slice(g * A_DIM, (g + 1) * A_DIM)
                bias = jnp.broadcast_to(bs_ref[:, g:g + 1], (CHUNK, A_DIM))
                for c, r in enumerate(chunk_rows):
                    mix_scr[r, gc] = (ua[g][r] * (mixed[g][c] + bias)).astype(BF16)
        else:
            head_finish(hh - 1, heads[hh - 1])
        head_chain(hh, heads[hh])
        if hh + 1 < B_HEADS:
            heads.append(head_inputs(hh + 1))
        head_dots_b(heads[hh])
        if hh + 1 < B_HEADS:
            head_dots_a(heads[hh + 1])

    mlp_piece(7)
    head_finish(B_HEADS - 1, heads[B_HEADS - 1])
    half = ROW_TILE // 2
    mix_a = _dot(mix_scr[:half, :], w_out_ref[...])
    mlp_finish()
    mix_b = _dot(mix_scr[half:, :], w_out_ref[...])
    for rows, mix in ((slice(0, half), mix_a), (slice(half, ROW_TILE), mix_b)):
        x1 = x[rows] + _rms(mix, post_mix_ref[...])
        x1_scr[slot, rows, :] = x1
        hn_scr[slot, rows, :] = _rms(x1, pre_mlp_ref[...]).astype(BF16)


def _resident(shape):
    zeros = (0,) * len(shape)
    return pl.BlockSpec(shape, lambda i: zeros, pipeline_mode=pl.Buffered(1))


def _prompt_layer(x, p, inv_freq):
    batch, seq, _ = x.shape
    tiles_per_seq = seq // ROW_TILE
    n_tiles = batch * tiles_per_seq

    def mixer_tile(i):
        return jnp.minimum(i, n_tiles - 1)

    def mlp_tile(i):
        return jnp.maximum(i - 1, 0)

    x_spec = pl.BlockSpec((None, ROW_TILE, D_MODEL),
                          lambda i: (mixer_tile(i) // tiles_per_seq, mixer_tile(i) % tiles_per_seq, 0))
    y_spec = pl.BlockSpec((None, ROW_TILE, D_MODEL),
                          lambda i: (mlp_tile(i) // tiles_per_seq, mlp_tile(i) % tiles_per_seq, 0))
    in_specs = [
        x_spec,
        _resident((1, D_MODEL)),
        _resident((D_MODEL, IN_COLS)),
        _resident((1, A_WIDTH)),
        _resident((1, A_WIDTH)),
        _resident((A_GROUPS, CHUNK, CHUNK)),
        _resident((CHUNK, A_GROUPS)),
        _resident((1, B_WIDTH)),
        _resident((D_MODEL, D_MODEL)),
        _resident((1, D_MODEL)),
        _resident((1, D_MODEL)),
        _resident((D_MODEL, D_FF)),
        _resident((D_FF, D_MODEL)),
        _resident((1, D_MODEL)),
        _resident((1, B_DK)),
    ]
    out_specs = [
        y_spec,
        pl.BlockSpec((None, B_HEADS, B_DK, B_DV),
                     lambda i: (mixer_tile(i) // tiles_per_seq, 0, 0, 0)),
    ]
    return pl.pallas_call(
        functools.partial(_prompt_kernel, tiles_per_seq=tiles_per_seq, n_tiles=n_tiles),
        grid=(n_tiles + 1,),
        in_specs=in_specs,
        out_specs=out_specs,
        out_shape=[
            jax.ShapeDtypeStruct(x.shape, F32),
            jax.ShapeDtypeStruct((batch, B_HEADS, B_DK, B_DV), F32),
        ],
        scratch_shapes=[pltpu.VMEM((ROW_TILE, D_MODEL), BF16),
                        pltpu.VMEM((2, ROW_TILE, D_MODEL), F32),
                        pltpu.VMEM((2, ROW_TILE, D_MODEL), BF16),
                        pltpu.VMEM((seq, B_DK), F32),
                        pltpu.VMEM((seq, B_DK), F32),
                        pltpu.VMEM((B_HEADS, CHUNK, CHUNK), F32),
                        pltpu.VMEM((B_HEADS, CHUNK, B_DK), F32),
                        pltpu.VMEM((B_HEADS, CHUNK, B_DK), F32)],
        compiler_params=pltpu.CompilerParams(
            dimension_semantics=("arbitrary",),
            vmem_limit_bytes=VMEM_LIMIT_BYTES),
        name="prompt_layer",
    )(x, p["pre_mix_w"], p["w_in"], p["ln_v_w"], p["ln_v_b"], p["w_s"], p["b_s_t"], p["gn_w"],
      p["w_out"], p["post_mix_w"], p["pre_mlp_w"], p["w_up"], p["w_down"], p["post_mlp_w"],
      inv_freq)


def _decode_in_kernel(x_ref, pre_mix_ref, w_in_ref, lnw_ref, lnb_ref, ws0_ref, bs0_ref, inv_ref,
                      outa_ref, va_ref, q_ref, k_ref, v_ref, gate_ref, w_in_bf_ref,
                      h_scr, z_scr):
    kk = pl.program_id(0)

    @pl.when(kk == 0)
    def _():
        h = _rms(x_ref[...], pre_mix_ref[...]).astype(BF16)
        for c in range(D_MODEL // W_IN_ROWS):
            h_scr[c] = h[:, c * W_IN_ROWS:(c + 1) * W_IN_ROWS]
        z_scr[...] = jnp.zeros_like(z_scr)

    w = w_in_ref[...].astype(BF16)
    w_in_bf_ref[...] = w
    z_scr[...] += _dot(h_scr[kk], w)

    @pl.when(kk == pl.num_programs(0) - 1)
    def _():
        z = z_scr[...]
        cosf, sinf = _rope_rows(jnp.full((1, B_DK), PAST_LEN, F32), inv_ref[...])
        for g in range(A_GROUPS):
            gc = slice(g * A_DIM, (g + 1) * A_DIM)
            ua = _gelu(z[:, g * A_DIM:(g + 1) * A_DIM])
            va = _gelu(z[:, A_WIDTH + g * A_DIM:A_WIDTH + (g + 1) * A_DIM])
            va = _center_norm(va) * lnw_ref[:, gc] + lnb_ref[:, gc]
            va_ref[:, gc] = va
            outa_ref[:, gc] = ua * (ws0_ref[:, gc] * va + bs0_ref[:, gc])
        for hh in range(B_HEADS):
            hc = slice(hh * B_DK, (hh + 1) * B_DK)
            base = 2 * A_WIDTH
            q_ref[:, hc] = _rope(z[:, base + hh * B_DK:base + (hh + 1) * B_DK], cosf, sinf)
            k_ref[:, hc] = _rope(z[:, base + B_WIDTH + hh * B_DK:base + B_WIDTH + (hh + 1) * B_DK],
                                 cosf, sinf) * (B_DK ** -0.5)
        v_ref[...] = z[:, 2 * A_WIDTH + 2 * B_WIDTH:2 * A_WIDTH + 3 * B_WIDTH]
        gate_ref[...] = _silu(z[:, 2 * A_WIDTH + 3 * B_WIDTH:])


def _decode_state_kernel(q_ref, k_ref, v_ref, s_ref, o_ref, s_out_ref):
    rid = lax.broadcasted_iota(jnp.int32, (STATE_ROWS, B_DK), 0)
    for hh in range(B_HEADS):
        hc = slice(hh * B_DK, (hh + 1) * B_DK)
        q = q_ref[:, hc]
        k = k_ref[:, hc]
        v = v_ref[:, hc]
        gam = math.exp(LOG_GAMMA[hh])
        qk = jnp.sum(q * k, axis=-1, keepdims=True)
        vb = v.astype(BF16)
        qg = (q * gam).astype(BF16)
        cross_rows = []
        for j in range(STATE_ROWS):
            s = s_ref[j, hh]
            cross_rows.append(_dot(qg, s.astype(BF16))[j:j + 1, :])
            kj = jnp.where(rid == j, k, 0.0).astype(BF16)
            s_out_ref[j, hh] = s * gam + _dot_tn(kj, vb)
        o_ref[:, hc] = qk * v + jnp.concatenate(cross_rows, axis=0)


def _decode_out_kernel(x_ref, outa_ref, o_ref, gate_ref, gn_ref, w_out_ref, post_mix_ref,
                       pre_mlp_ref, w_up_ref, w_down_ref, post_mlp_ref,
                       y_ref, w_out_bf_ref, w_up_bf_ref, w_down_bf_ref,
                       x1_scr, hn_scr, f_scr):
    j = pl.program_id(0)

    @pl.when(j == 0)
    def _():
        w_out = w_out_ref[...].astype(BF16)
        w_out_bf_ref[...] = w_out
        parts = [outa_ref[...].astype(BF16)]
        for hh in range(B_HEADS):
            hc = slice(hh * B_DV, (hh + 1) * B_DV)
            o = _center_norm(o_ref[:, hc]) * gn_ref[:, hc]
            parts.append((o * gate_ref[:, hc]).astype(BF16))
        mix = _dot(jnp.concatenate(parts, axis=-1), w_out)
        x1 = x_ref[...] + _rms(mix, post_mix_ref[...])
        x1_scr[...] = x1
        hn_scr[...] = _rms(x1, pre_mlp_ref[...]).astype(BF16)
        f_scr[...] = jnp.zeros_like(f_scr)

    w_up = w_up_ref[...].astype(BF16)
    w_down = w_down_ref[...].astype(BF16)
    w_up_bf_ref[...] = w_up
    w_down_bf_ref[...] = w_down
    u = jnp.maximum(_dot(hn_scr[...], w_up), 0.0)
    f_scr[...] += _dot((u * u).astype(BF16), w_down)

    @pl.when(j == pl.num_programs(0) - 1)
    def _():
        y_ref[...] = x1_scr[...] + _rms(f_scr[...], post_mlp_ref[...])


def _full(shape):
    zeros = (0,) * len(shape)
    return pl.BlockSpec(shape, lambda *_: zeros)


def _decode_layer(x, s0, p, inv_freq):
    n = x.shape[0]
    act = jax.ShapeDtypeStruct((n, A_WIDTH), F32)
    n_k = D_MODEL // W_IN_ROWS
    w_in_blk = pl.BlockSpec((W_IN_ROWS, IN_COLS), lambda kk: (kk, 0))
    outa, va, q, k, v, gate, w_in_bf = pl.pallas_call(
        _decode_in_kernel,
        grid=(n_k,),
        in_specs=[_full((n, D_MODEL)), _full((1, D_MODEL)), w_in_blk,
                  _full((1, A_WIDTH)), _full((1, A_WIDTH)), _full((1, A_WIDTH)), _full((1, A_WIDTH)),
                  _full((1, B_DK))],
        out_specs=[_full((n, A_WIDTH))] * 6 + [w_in_blk],
        out_shape=[act] * 6 + [jax.ShapeDtypeStruct((D_MODEL, IN_COLS), BF16)],
        scratch_shapes=[pltpu.VMEM((n_k, n, W_IN_ROWS), BF16), pltpu.VMEM((n, IN_COLS), F32)],
        compiler_params=pltpu.CompilerParams(
            dimension_semantics=("arbitrary",), vmem_limit_bytes=VMEM_LIMIT_BYTES),
        name="decode_in",
    )(x, p["pre_mix_w"], p["w_in"], p["ln_v_w"], p["ln_v_b"], p["w_s0"], p["b_s0"], inv_freq)

    row_spec = pl.BlockSpec((STATE_ROWS, B_WIDTH), lambda i: (i, 0))
    st_spec = pl.BlockSpec((STATE_ROWS, B_HEADS, B_DK, B_DV), lambda i: (i, 0, 0, 0))
    o, s_new = pl.pallas_call(
        _decode_state_kernel,
        grid=(n // STATE_ROWS,),
        in_specs=[row_spec, row_spec, row_spec, st_spec],
        out_specs=[row_spec, st_spec],
        out_shape=[act, jax.ShapeDtypeStruct(s0.shape, F32)],
        compiler_params=pltpu.CompilerParams(
            dimension_semantics=("arbitrary",), vmem_limit_bytes=VMEM_LIMIT_BYTES),
        name="decode_state",
    )(q, k, v, s0)

    up_blk = pl.BlockSpec((D_MODEL, DEC_FF_TILE), lambda j: (0, j))
    down_blk = pl.BlockSpec((DEC_FF_TILE, D_MODEL), lambda j: (j, 0))
    y, w_out_bf, w_up_bf, w_down_bf = pl.pallas_call(
        _decode_out_kernel,
        grid=(D_FF // DEC_FF_TILE,),
        in_specs=[_full((n, D_MODEL)), _full((n, A_WIDTH)), _full((n, B_WIDTH)), _full((n, B_WIDTH)),
                  _full((1, B_WIDTH)), _full((D_MODEL, D_MODEL)), _full((1, D_MODEL)),
                  _full((1, D_MODEL)), up_blk, down_blk, _full((1, D_MODEL))],
        out_specs=[_full((n, D_MODEL)), _full((D_MODEL, D_MODEL)), up_blk, down_blk],
        out_shape=[jax.ShapeDtypeStruct((n, D_MODEL), F32),
                   jax.ShapeDtypeStruct((D_MODEL, D_MODEL), BF16),
                   jax.ShapeDtypeStruct((D_MODEL, D_FF), BF16),
                   jax.ShapeDtypeStruct((D_FF, D_MODEL), BF16)],
        scratch_shapes=[pltpu.VMEM((n, D_MODEL), F32), pltpu.VMEM((n, D_MODEL), BF16),
                        pltpu.VMEM((n, D_MODEL), F32)],
        compiler_params=pltpu.CompilerParams(
            dimension_semantics=("arbitrary",), vmem_limit_bytes=VMEM_LIMIT_BYTES),
        name="decode_out",
    )(x, outa, o, gate, p["gn_w"], p["w_out"], p["post_mix_w"], p["pre_mlp_w"],
      p["w_up"], p["w_down"], p["post_mlp_w"])
    weights_bf = {"w_in": w_in_bf, "w_out": w_out_bf, "w_up": w_up_bf, "w_down": w_down_bf}
    return y, s_new, va, weights_bf


def kernel(x_prompt, x_sample, state_ret, pre_mix_w, w_in, ln_v_w, ln_v_b, w_s, b_s, gn_w,
           w_out, post_mix_w, pre_mlp_w, w_up, w_down, post_mlp_w):
    depth = w_in.shape[0]
    seq = x_prompt.shape[1]
    n_dec, dec_seq, _ = x_sample.shape
    assert dec_seq == 1 and seq % ROW_TILE == 0 and n_dec % STATE_ROWS == 0

    half = B_DK // 2
    inv = ROPE_BASE ** (-jnp.arange(half, dtype=F32) / half)
    inv_freq = jnp.concatenate([inv, inv])[None, :]

    yp, ys = x_prompt, x_sample.reshape(n_dec, D_MODEL)
    sp_list, ss_list, vs_list = [], [], []
    for l in range(depth):
        p = {
            "pre_mix_w": pre_mix_w[l][None, :],
            "w_in": w_in[l],
            "ln_v_w": ln_v_w[l].reshape(1, A_WIDTH),
            "ln_v_b": ln_v_b[l].reshape(1, A_WIDTH),
            "w_s": w_s[l],
            "b_s_t": b_s[l].T,
            "w_s0": jnp.repeat(w_s[l][:, 0, 0], A_DIM)[None, :],
            "b_s0": jnp.repeat(b_s[l][:, 0], A_DIM)[None, :],
            "gn_w": gn_w[l].reshape(1, B_WIDTH),
            "w_out": w_out[l],
            "post_mix_w": post_mix_w[l][None, :],
            "pre_mlp_w": pre_mlp_w[l][None, :],
            "w_up": w_up[l],
            "w_down": w_down[l],
            "post_mlp_w": post_mlp_w[l][None, :],
        }
        ys, ss, vs, weights_bf = _decode_layer(ys, state_ret[l], p, inv_freq)
        yp, sp = _prompt_layer(yp, {**p, **weights_bf}, inv_freq)
        sp_list.append(sp)
        ss_list.append(ss)
        vs_list.append(vs.reshape(n_dec, 1, A_GROUPS, A_DIM))
    return (yp, ys.reshape(n_dec, 1, D_MODEL), jnp.stack(sp_list, axis=0),
            jnp.stack(ss_list, axis=0), jnp.stack(vs_list, axis=0))
```

```python
import functools
import math

import jax
import jax.numpy as jnp
from jax import lax
from jax.experimental import pallas as pl
from jax.experimental.pallas import tpu as pltpu

D_MODEL = 1024
A_WIDTH = 512
B_WIDTH = 512
A_GROUPS = 4
A_DIM = 128
CHUNK = 128
B_HEADS = 4
B_DK = 128
B_DV = 128
D_FF = 4096
IN_COLS = 2 * A_WIDTH + 4 * B_WIDTH
ROPE_BASE = 10000.0
EPS = 1e-6
PAST_LEN = 16384
LOG_GAMMA = [math.log(1.0 - 2.0 ** (-5.0 - h)) for h in range(B_HEADS)]

ROW_TILE = 512
FF_TILE = 512
STATE_ROWS = 16
W_IN_ROWS = 128
DEC_FF_TILE = 256
VMEM_LIMIT_BYTES = 56 * 1024 * 1024

F32 = jnp.float32
BF16 = jnp.bfloat16


def _rms(x, w):
    return x * lax.rsqrt(jnp.mean(x * x, axis=-1, keepdims=True) + EPS) * w


def _center_norm(x):
    mu = jnp.mean(x, axis=-1, keepdims=True)
    xc = x - mu
    return xc * lax.rsqrt(jnp.mean(xc * xc, axis=-1, keepdims=True) + EPS)


def _gelu(x):
    return jax.nn.gelu(x, approximate=True)


def _silu(x):
    return x * (1.0 / (1.0 + jnp.exp(-x)))


def _rope(x, cosf, sinf):
    return x * cosf + pltpu.roll(x, B_DK // 2, axis=1) * sinf


def _rope_rows(pos, inv):
    ang = pos * inv
    lane = lax.broadcasted_iota(jnp.int32, ang.shape, 1)
    sin = jnp.sin(ang)
    return jnp.cos(ang), jnp.where(lane < B_DK // 2, -sin, sin)


def _dot(a, b):
    return jnp.dot(a, b, preferred_element_type=F32)


def _dot_nt(a, b):
    return lax.dot_general(a, b, (((1,), (1,)), ((), ())), preferred_element_type=F32)


def _dot_tn(a, b):
    return lax.dot_general(a, b, (((0,), (0,)), ((), ())), preferred_element_type=F32)


def _fill_tables(inv_ref, cos_ref, sin_ref, mask_ref, qdec_ref, kdec_ref):
    seq = cos_ref.shape[0]
    for r0 in range(0, seq, ROW_TILE):
        pos = (lax.broadcasted_iota(jnp.int32, (ROW_TILE, B_DK), 0) + r0).astype(F32)
        cosf, sinf = _rope_rows(pos, inv_ref[...])
        cos_ref[r0:r0 + ROW_TILE, :] = cosf
        sin_ref[r0:r0 + ROW_TILE, :] = sinf
    row = lax.broadcasted_iota(jnp.int32, (CHUNK, CHUNK), 0).astype(F32)
    col = lax.broadcasted_iota(jnp.int32, (CHUNK, CHUNK), 1).astype(F32)
    diff = row - col
    for hh in range(B_HEADS):
        lg = LOG_GAMMA[hh]
        mask_ref[hh] = jnp.where(diff >= 0, jnp.exp(lg * jnp.maximum(diff, 0.0)), 0.0)
        qdec_ref[hh] = jnp.exp(lg * (row + 1.0))
        kdec_ref[hh] = jnp.exp(lg * (CHUNK - 1.0 - row))


def _prompt_kernel(x_ref, pre_mix_ref, w_in_ref, lnw_ref, lnb_ref, ws_ref, bs_ref, gn_ref,
                   w_out_ref, post_mix_ref, pre_mlp_ref, w_up_ref, w_down_ref, post_mlp_ref,
                   inv_ref, y_ref, s_ref, mix_scr, x1_scr, hn_scr,
                   cos_ref, sin_ref, mask_ref, qdec_ref, kdec_ref, *, tiles_per_seq, n_tiles):
    i = pl.program_id(0)
    refs = (x_ref, pre_mix_ref, w_in_ref, lnw_ref, lnb_ref, ws_ref, bs_ref, gn_ref,
            w_out_ref, post_mix_ref, pre_mlp_ref, w_up_ref, w_down_ref, post_mlp_ref,
            y_ref, s_ref, mix_scr, x1_scr, hn_scr,
            cos_ref, sin_ref, mask_ref, qdec_ref, kdec_ref)

    @pl.when(i == 0)
    def _():
        _fill_tables(inv_ref, cos_ref, sin_ref, mask_ref, qdec_ref, kdec_ref)

    @pl.when(jnp.logical_and(lax.rem(i, tiles_per_seq) == 0, i < n_tiles))
    def _():
        s_ref[...] = jnp.zeros_like(s_ref)

    @pl.when(i == 0)
    def _():
        _prompt_step(*refs, tiles_per_seq=tiles_per_seq, with_mixer=True, with_mlp=False)

    @pl.when(jnp.logical_and(i > 0, i < n_tiles))
    def _():
        _prompt_step(*refs, tiles_per_seq=tiles_per_seq, with_mixer=True, with_mlp=True)

    @pl.when(i == n_tiles)
    def _():
        _prompt_step(*refs, tiles_per_seq=tiles_per_seq, with_mixer=False, with_mlp=True)


def _prompt_step(x_ref, pre_mix_ref, w_in_ref, lnw_ref, lnb_ref, ws_ref, bs_ref, gn_ref,
                 w_out_ref, post_mix_ref, pre_mlp_ref, w_up_ref, w_down_ref, post_mlp_ref,
                 y_ref, s_ref, mix_scr, x1_scr, hn_scr,
                 cos_ref, sin_ref, mask_ref, qdec_ref, kdec_ref, *, tiles_per_seq, with_mixer, with_mlp):
    slot = lax.rem(pl.program_id(0), 2)
    n_chunks = ROW_TILE // CHUNK
    chunk_rows = [slice(c * CHUNK, (c + 1) * CHUNK) for c in range(n_chunks)]
    hn_prev = hn_scr.at[1 - slot]
    half = ROW_TILE // 2
    halves = (slice(0, half), slice(half, ROW_TILE))
    acc = [None, None]

    def mlp_up(j):
        cols = slice(j * FF_TILE, (j + 1) * FF_TILE)
        u = jnp.maximum(_dot(hn_prev[...], w_up_ref[:, cols]), 0.0)
        return (u * u).astype(BF16)

    def mlp_down(j, u, which):
        cols = slice(j * FF_TILE, (j + 1) * FF_TILE)
        if len(which) == 2:
            part = _dot(u, w_down_ref[cols, :])
            parts = [part[rows] for rows in halves]
        else:
            parts = [_dot(u[halves[hi]], w_down_ref[cols, :]) for hi in which]
        for hi, part in zip(which, parts):
            acc[hi] = part if acc[hi] is None else acc[hi] + part

    def mlp_piece(j):
        if with_mlp:
            mlp_down(j, mlp_up(j), (0, 1))

    def mlp_finish(hi):
        rows = halves[hi]
        y_ref[rows, :] = x1_scr[1 - slot, rows, :] + _rms(acc[hi], post_mlp_ref[...])

    if not with_mixer:
        for j in range(D_FF // FF_TILE):
            mlp_piece(j)
        mlp_finish(0)
        mlp_finish(1)
        return

    x = x_ref[...]
    mlp_piece(0)
    h = _rms(x, pre_mix_ref[...]).astype(BF16)
    za = _dot(h, w_in_ref[:, :2 * A_WIDTH])
    zb = _dot(h, w_in_ref[:, 2 * A_WIDTH:])
    tile_pos = pl.multiple_of(lax.rem(pl.program_id(0), tiles_per_seq) * ROW_TILE, ROW_TILE)
    cosf = cos_ref[pl.ds(tile_pos, ROW_TILE), :]
    sinf = sin_ref[pl.ds(tile_pos, ROW_TILE), :]

    def head_inputs(hh):
        base = hh * B_DK
        q = _rope(zb[:, base:base + B_DK], cosf, sinf)
        k = _rope(zb[:, base + B_WIDTH:base + B_WIDTH + B_DK], cosf, sinf) * (B_DK ** -0.5)
        v = zb[:, base + 2 * B_WIDTH:base + 2 * B_WIDTH + B_DV].astype(BF16)
        return {
            "q": q.astype(BF16),
            "k": k.astype(BF16),
            "v": v,
            "qd": [(q[r] * qdec_ref[hh]).astype(BF16) for r in chunk_rows],
            "kd": [(k[r] * kdec_ref[hh]).astype(BF16) for r in chunk_rows],
        }

    def head_dots_a(hd):
        hd["scores"] = [_dot_nt(hd["q"][r], hd["k"][r]) for r in chunk_rows]
        hd["kv"] = [_dot_tn(hd["kd"][c], hd["v"][r]) for c, r in enumerate(chunk_rows)]

    def head_chain(hh, hd):
        s = s_ref[hh]
        hd["s"] = []
        for c in range(n_chunks):
            hd["s"].append(s.astype(BF16))
            s = s * math.exp(LOG_GAMMA[hh] * CHUNK) + hd["kv"][c]
        s_ref[hh] = s
        hd["p"] = [(hd["scores"][c] * mask_ref[hh]).astype(BF16) for c in range(n_chunks)]

    def head_dots_b(hd):
        hd["o"] = [_dot(hd["p"][c], hd["v"][r]) + _dot(hd["qd"][c], hd["s"][c])
                   for c, r in enumerate(chunk_rows)]

    def head_finish(hh, hd):
        base = 3 * B_WIDTH + hh * B_DV
        gate = _silu(zb[:, base:base + B_DV])
        gnw = gn_ref[:, hh * B_DV:(hh + 1) * B_DV]
        for c, r in enumerate(chunk_rows):
            o = _center_norm(hd["o"][c]) * gnw
            mix_scr[r, A_WIDTH + hh * B_DV:A_WIDTH + (hh + 1) * B_DV] = (o * gate[r]).astype(BF16)

    ua, va = [], []
    for g in range(A_GROUPS):
        gc = slice(g * A_DIM, (g + 1) * A_DIM)
        ua.append(_gelu(za[:, g * A_DIM:(g + 1) * A_DIM]))
        vg = _gelu(za[:, A_WIDTH + g * A_DIM:A_WIDTH + (g + 1) * A_DIM])
        va.append((_center_norm(vg) * lnw_ref[:, gc] + lnb_ref[:, gc]).astype(BF16))
    mlp_piece(1)
    heads = [head_inputs(0)]

    row = lax.broadcasted_iota(jnp.int32, (CHUNK, CHUNK), 0)
    col = lax.broadcasted_iota(jnp.int32, (CHUNK, CHUNK), 1)
    mixed = []
    for g in range(A_GROUPS):
        w_tril = jnp.where(row >= col, ws_ref[g], 0.0).astype(BF16)
        mixed.append([_dot(w_tril, va[g][r]) for r in chunk_rows])
    head_dots_a(heads[0])

    for hh in range(B_HEADS):
        mlp_piece(2 + hh)
        if hh == 0:
            for g in range(A_GROUPS):
                gc = slice(g * A_DIM, (g + 1) * A_DIM)
                bias = jnp.broadcast_to(bs_ref[:, g:g + 1], (CHUNK, A_DIM))
                for c, r in enumerate(chunk_rows):
                    mix_scr[r, gc] = (ua[g][r] * (mixed[g][c] + bias)).astype(BF16)
        else:
            head_finish(hh - 1, heads[hh - 1])
        head_chain(hh, heads[hh])
        if hh + 1 < B_HEADS:
            heads.append(head_inputs(hh + 1))
        head_dots_b(heads[hh])
        if hh + 1 < B_HEADS:
            head_dots_a(heads[hh + 1])

    def mixer_finish(hi, mix):
        rows = halves[hi]
        x1 = x[rows] + _rms(mix, post_mix_ref[...])
        x1_scr[slot, rows, :] = x1
        hn_scr[slot, rows, :] = _rms(x1, pre_mlp_ref[...]).astype(BF16)

    mlp_piece(6)
    head_finish(B_HEADS - 1, heads[B_HEADS - 1])
    mix_a = _dot(mix_scr[:half, :], w_out_ref[...])
    mix_b = _dot(mix_scr[half:, :], w_out_ref[...])
    mixer_finish(0, mix_a)
    u_last = mlp_up(7) if with_mlp else None
    mixer_finish(1, mix_b)
    if with_mlp:
        mlp_down(7, u_last, (0,))
        mlp_down(7, u_last, (1,))
        mlp_finish(0)
        mlp_finish(1)


def _resident(shape):
    zeros = (0,) * len(shape)
    return pl.BlockSpec(shape, lambda i: zeros, pipeline_mode=pl.Buffered(1))


def _prompt_layer(x, p, inv_freq):
    batch, seq, _ = x.shape
    tiles_per_seq = seq // ROW_TILE
    n_tiles = batch * tiles_per_seq

    def mixer_tile(i):
        return jnp.minimum(i, n_tiles - 1)

    def mlp_tile(i):
        return jnp.maximum(i - 1, 0)

    x_spec = pl.BlockSpec((None, ROW_TILE, D_MODEL),
                          lambda i: (mixer_tile(i) // tiles_per_seq, mixer_tile(i) % tiles_per_seq, 0))
    y_spec = pl.BlockSpec((None, ROW_TILE, D_MODEL),
                          lambda i: (mlp_tile(i) // tiles_per_seq, mlp_tile(i) % tiles_per_seq, 0))
    in_specs = [
        x_spec,
        _resident((1, D_MODEL)),
        _resident((D_MODEL, IN_COLS)),
        _resident((1, A_WIDTH)),
        _resident((1, A_WIDTH)),
        _resident((A_GROUPS, CHUNK, CHUNK)),
        _resident((CHUNK, A_GROUPS)),
        _resident((1, B_WIDTH)),
        _resident((D_MODEL, D_MODEL)),
        _resident((1, D_MODEL)),
        _resident((1, D_MODEL)),
        _resident((D_MODEL, D_FF)),
        _resident((D_FF, D_MODEL)),
        _resident((1, D_MODEL)),
        _resident((1, B_DK)),
    ]
    out_specs = [
        y_spec,
        pl.BlockSpec((None, B_HEADS, B_DK, B_DV),
                     lambda i: (mixer_tile(i) // tiles_per_seq, 0, 0, 0)),
    ]
    return pl.pallas_call(
        functools.partial(_prompt_kernel, tiles_per_seq=tiles_per_seq, n_tiles=n_tiles),
        grid=(n_tiles + 1,),
        in_specs=in_specs,
        out_specs=out_specs,
        out_shape=[
            jax.ShapeDtypeStruct(x.shape, F32),
            jax.ShapeDtypeStruct((batch, B_HEADS, B_DK, B_DV), F32),
        ],
        scratch_shapes=[pltpu.VMEM((ROW_TILE, D_MODEL), BF16),
                        pltpu.VMEM((2, ROW_TILE, D_MODEL), F32),
                        pltpu.VMEM((2, ROW_TILE, D_MODEL), BF16),
                        pltpu.VMEM((seq, B_DK), F32),
                        pltpu.VMEM((seq, B_DK), F32),
                        pltpu.VMEM((B_HEADS, CHUNK, CHUNK), F32),
                        pltpu.VMEM((B_HEADS, CHUNK, B_DK), F32),
                        pltpu.VMEM((B_HEADS, CHUNK, B_DK), F32)],
        compiler_params=pltpu.CompilerParams(
            dimension_semantics=("arbitrary",),
            vmem_limit_bytes=VMEM_LIMIT_BYTES),
        name="prompt_layer",
    )(x, p["pre_mix_w"], p["w_in"], p["ln_v_w"], p["ln_v_b"], p["w_s"], p["b_s_t"], p["gn_w"],
      p["w_out"], p["post_mix_w"], p["pre_mlp_w"], p["w_up"], p["w_down"], p["post_mlp_w"],
      inv_freq)


def _decode_in_kernel(x_ref, pre_mix_ref, w_in_ref, lnw_ref, lnb_ref, ws0_ref, bs0_ref, inv_ref,
                      outa_ref, va_ref, q_ref, k_ref, v_ref, gate_ref, w_in_bf_ref,
                      h_scr, z_scr):
    kk = pl.program_id(0)

    @pl.when(kk == 0)
    def _():
        h = _rms(x_ref[...], pre_mix_ref[...]).astype(BF16)
        for c in range(D_MODEL // W_IN_ROWS):
            h_scr[c] = h[:, c * W_IN_ROWS:(c + 1) * W_IN_ROWS]
        z_scr[...] = jnp.zeros_like(z_scr)

    w = w_in_ref[...].astype(BF16)
    w_in_bf_ref[...] = w
    z_scr[...] += _dot(h_scr[kk], w)

    @pl.when(kk == pl.num_programs(0) - 1)
    def _():
        z = z_scr[...]
        cosf, sinf = _rope_rows(jnp.full((1, B_DK), PAST_LEN, F32), inv_ref[...])
        for g in range(A_GROUPS):
            gc = slice(g * A_DIM, (g + 1) * A_DIM)
            ua = _gelu(z[:, g * A_DIM:(g + 1) * A_DIM])
            va = _gelu(z[:, A_WIDTH + g * A_DIM:A_WIDTH + (g + 1) * A_DIM])
            va = _center_norm(va) * lnw_ref[:, gc] + lnb_ref[:, gc]
            va_ref[:, gc] = va
            outa_ref[:, gc] = ua * (ws0_ref[:, gc] * va + bs0_ref[:, gc])
        for hh in range(B_HEADS):
            hc = slice(hh * B_DK, (hh + 1) * B_DK)
            base = 2 * A_WIDTH
            q_ref[:, hc] = _rope(z[:, base + hh * B_DK:base + (hh + 1) * B_DK], cosf, sinf)
            k_ref[:, hc] = _rope(z[:, base + B_WIDTH + hh * B_DK:base + B_WIDTH + (hh + 1) * B_DK],
                                 cosf, sinf) * (B_DK ** -0.5)
        v_ref[...] = z[:, 2 * A_WIDTH + 2 * B_WIDTH:2 * A_WIDTH + 3 * B_WIDTH]
        gate_ref[...] = _silu(z[:, 2 * A_WIDTH + 3 * B_WIDTH:])


def _decode_state_kernel(q_ref, k_ref, v_ref, s_ref, o_ref, s_out_ref):
    rid = lax.broadcasted_iota(jnp.int32, (STATE_ROWS, B_DK), 0)
    for hh in range(B_HEADS):
        hc = slice(hh * B_DK, (hh + 1) * B_DK)
        q = q_ref[:, hc]
        k = k_ref[:, hc]
        v = v_ref[:, hc]
        gam = math.exp(LOG_GAMMA[hh])
        qk = jnp.sum(q * k, axis=-1, keepdims=True)
        vb = v.astype(BF16)
        qg = (q * gam).astype(BF16)
        cross_rows = []
        for j in range(STATE_ROWS):
            s = s_ref[j, hh]
            cross_rows.append(_dot(qg, s.astype(BF16))[j:j + 1, :])
            kj = jnp.where(rid == j, k, 0.0).astype(BF16)
            s_out_ref[j, hh] = s * gam + _dot_tn(kj, vb)
        o_ref[:, hc] = qk * v + jnp.concatenate(cross_rows, axis=0)


def _decode_out_kernel(x_ref, outa_ref, o_ref, gate_ref, gn_ref, w_out_ref, post_mix_ref,
                       pre_mlp_ref, w_up_ref, w_down_ref, post_mlp_ref,
                       y_ref, w_out_bf_ref, w_up_bf_ref, w_down_bf_ref,
                       x1_scr, hn_scr, f_scr):
    j = pl.program_id(0)

    @pl.when(j == 0)
    def _():
        w_out = w_out_ref[...].astype(BF16)
        w_out_bf_ref[...] = w_out
        parts = [outa_ref[...].astype(BF16)]
        for hh in range(B_HEADS):
            hc = slice(hh * B_DV, (hh + 1) * B_DV)
            o = _center_norm(o_ref[:, hc]) * gn_ref[:, hc]
            parts.append((o * gate_ref[:, hc]).astype(BF16))
        mix = _dot(jnp.concatenate(parts, axis=-1), w_out)
        x1 = x_ref[...] + _rms(mix, post_mix_ref[...])
        x1_scr[...] = x1
        hn_scr[...] = _rms(x1, pre_mlp_ref[...]).astype(BF16)
        f_scr[...] = jnp.zeros_like(f_scr)

    w_up = w_up_ref[...].astype(BF16)
    w_down = w_down_ref[...].astype(BF16)
    w_up_bf_ref[...] = w_up
    w_down_bf_ref[...] = w_down
    u = jnp.maximum(_dot(hn_scr[...], w_up), 0.0)
    f_scr[...] += _dot((u * u).astype(BF16), w_down)

    @pl.when(j == pl.num_programs(0) - 1)
    def _():
        y_ref[...] = x1_scr[...] + _rms(f_scr[...], post_mlp_ref[...])


def _full(shape):
    zeros = (0,) * len(shape)
    return pl.BlockSpec(shape, lambda *_: zeros)


def _decode_layer(x, s0, p, inv_freq):
    n = x.shape[0]
    act = jax.ShapeDtypeStruct((n, A_WIDTH), F32)
    n_k = D_MODEL // W_IN_ROWS
    w_in_blk = pl.BlockSpec((W_IN_ROWS, IN_COLS), lambda kk: (kk, 0))
    outa, va, q, k, v, gate, w_in_bf = pl.pallas_call(
        _decode_in_kernel,
        grid=(n_k,),
        in_specs=[_full((n, D_MODEL)), _full((1, D_MODEL)), w_in_blk,
                  _full((1, A_WIDTH)), _full((1, A_WIDTH)), _full((1, A_WIDTH)), _full((1, A_WIDTH)),
                  _full((1, B_DK))],
        out_specs=[_full((n, A_WIDTH))] * 6 + [w_in_blk],
        out_shape=[act] * 6 + [jax.ShapeDtypeStruct((D_MODEL, IN_COLS), BF16)],
        scratch_shapes=[pltpu.VMEM((n_k, n, W_IN_ROWS), BF16), pltpu.VMEM((n, IN_COLS), F32)],
        compiler_params=pltpu.CompilerParams(
            dimension_semantics=("arbitrary",), vmem_limit_bytes=VMEM_LIMIT_BYTES),
        name="decode_in",
    )(x, p["pre_mix_w"], p["w_in"], p["ln_v_w"], p["ln_v_b"], p["w_s0"], p["b_s0"], inv_freq)

    row_spec = pl.BlockSpec((STATE_ROWS, B_WIDTH), lambda i: (i, 0))
    st_spec = pl.BlockSpec((STATE_ROWS, B_HEADS, B_DK, B_DV), lambda i: (i, 0, 0, 0))
    o, s_new = pl.pallas_call(
        _decode_state_kernel,
        grid=(n // STATE_ROWS,),
        in_specs=[row_spec, row_spec, row_spec, st_spec],
        out_specs=[row_spec, st_spec],
        out_shape=[act, jax.ShapeDtypeStruct(s0.shape, F32)],
        compiler_params=pltpu.CompilerParams(
            dimension_semantics=("arbitrary",), vmem_limit_bytes=VMEM_LIMIT_BYTES),
        name="decode_state",
    )(q, k, v, s0)

    up_blk = pl.BlockSpec((D_MODEL, DEC_FF_TILE), lambda j: (0, j))
    down_blk = pl.BlockSpec((DEC_FF_TILE, D_MODEL), lambda j: (j, 0))
    y, w_out_bf, w_up_bf, w_down_bf = pl.pallas_call(
        _decode_out_kernel,
        grid=(D_FF // DEC_FF_TILE,),
        in_specs=[_full((n, D_MODEL)), _full((n, A_WIDTH)), _full((n, B_WIDTH)), _full((n, B_WIDTH)),
                  _full((1, B_WIDTH)), _full((D_MODEL, D_MODEL)), _full((1, D_MODEL)),
                  _full((1, D_MODEL)), up_blk, down_blk, _full((1, D_MODEL))],
        out_specs=[_full((n, D_MODEL)), _full((D_MODEL, D_MODEL)), up_blk, down_blk],
        out_shape=[jax.ShapeDtypeStruct((n, D_MODEL), F32),
                   jax.ShapeDtypeStruct((D_MODEL, D_MODEL), BF16),
                   jax.ShapeDtypeStruct((D_MODEL, D_FF), BF16),
                   jax.ShapeDtypeStruct((D_FF, D_MODEL), BF16)],
        scratch_shapes=[pltpu.VMEM((n, D_MODEL), F32), pltpu.VMEM((n, D_MODEL), BF16),
                        pltpu.VMEM((n, D_MODEL), F32)],
        compiler_params=pltpu.CompilerParams(
            dimension_semantics=("arbitrary",), vmem_limit_bytes=VMEM_LIMIT_BYTES),
        name="decode_out",
    )(x, outa, o, gate, p["gn_w"], p["w_out"], p["post_mix_w"], p["pre_mlp_w"],
      p["w_up"], p["w_down"], p["post_mlp_w"])
    weights_bf = {"w_in": w_in_bf, "w_out": w_out_bf, "w_up": w_up_bf, "w_down": w_down_bf}
    return y, s_new, va, weights_bf


def kernel(x_prompt, x_sample, state_ret, pre_mix_w, w_in, ln_v_w, ln_v_b, w_s, b_s, gn_w,
           w_out, post_mix_w, pre_mlp_w, w_up, w_down, post_mlp_w):
    depth = w_in.shape[0]
    seq = x_prompt.shape[1]
    n_dec, dec_seq, _ = x_sample.shape
    assert dec_seq == 1 and seq % ROW_TILE == 0 and n_dec % STATE_ROWS == 0

    half = B_DK // 2
    inv = ROPE_BASE ** (-jnp.arange(half, dtype=F32) / half)
    inv_freq = jnp.concatenate([inv, inv])[None, :]

    yp, ys = x_prompt, x_sample.reshape(n_dec, D_MODEL)
    sp_list, ss_list, vs_list = [], [], []
    for l in range(depth):
        p = {
            "pre_mix_w": pre_mix_w[l][None, :],
            "w_in": w_in[l],
            "ln_v_w": ln_v_w[l].reshape(1, A_WIDTH),
            "ln_v_b": ln_v_b[l].reshape(1, A_WIDTH),
            "w_s": w_s[l],
            "b_s_t": b_s[l].T,
            "w_s0": jnp.repeat(w_s[l][:, 0, 0], A_DIM)[None, :],
            "b_s0": jnp.repeat(b_s[l][:, 0], A_DIM)[None, :],
            "gn_w": gn_w[l].reshape(1, B_WIDTH),
            "w_out": w_out[l],
            "post_mix_w": post_mix_w[l][None, :],
            "pre_mlp_w": pre_mlp_w[l][None, :],
            "w_up": w_up[l],
            "w_down": w_down[l],
            "post_mlp_w": post_mlp_w[l][None, :],
        }
        ys, ss, vs, weights_bf = _decode_layer(ys, state_ret[l], p, inv_freq)
        yp, sp = _prompt_layer(yp, {**p, **weights_bf}, inv_freq)
        sp_list.append(sp)
        ss_list.append(ss)
        vs_list.append(vs.reshape(n_dec, 1, A_GROUPS, A_DIM))
    return (yp, ys.reshape(n_dec, 1, D_MODEL), jnp.stack(sp_list, axis=0),
            jnp.stack(ss_list, axis=0), jnp.stack(vs_list, axis=0))
```

```python
import functools
import math

import jax
import jax.numpy as jnp
from jax import lax
from jax.experimental import pallas as pl
from jax.experimental.pallas import tpu as pltpu

D_MODEL = 1024
A_WIDTH = 512
B_WIDTH = 512
A_GROUPS = 4
A_DIM = 128
CHUNK = 128
B_HEADS = 4
B_DK = 128
B_DV = 128
D_FF = 4096
IN_COLS = 2 * A_WIDTH + 4 * B_WIDTH
ROPE_BASE = 10000.0
EPS = 1e-6
PAST_LEN = 16384
LOG_GAMMA = [math.log(1.0 - 2.0 ** (-5.0 - h)) for h in range(B_HEADS)]

ROW_TILE = 512
FF_TILE = 512
STATE_ROWS = 16
W_IN_ROWS = 512
DEC_FF_TILE = 1024
VMEM_LIMIT_BYTES = 56 * 1024 * 1024

F32 = jnp.float32
BF16 = jnp.bfloat16


def _rms(x, w):
    return x * lax.rsqrt(jnp.mean(x * x, axis=-1, keepdims=True) + EPS) * w


def _center_norm(x):
    mu = jnp.mean(x, axis=-1, keepdims=True)
    xc = x - mu
    return xc * lax.rsqrt(jnp.mean(xc * xc, axis=-1, keepdims=True) + EPS)


def _gelu(x):
    return jax.nn.gelu(x, approximate=True)


def _silu(x):
    return x * (1.0 / (1.0 + jnp.exp(-x)))


def _rope(x, cosf, sinf):
    return x * cosf + pltpu.roll(x, B_DK // 2, axis=1) * sinf


def _rope_rows(pos, inv):
    ang = pos * inv
    lane = lax.broadcasted_iota(jnp.int32, ang.shape, 1)
    sin = jnp.sin(ang)
    return jnp.cos(ang), jnp.where(lane < B_DK // 2, -sin, sin)


def _dot(a, b):
    return jnp.dot(a, b, preferred_element_type=F32)


def _dot_nt(a, b):
    return lax.dot_general(a, b, (((1,), (1,)), ((), ())), preferred_element_type=F32)


def _dot_tn(a, b):
    return lax.dot_general(a, b, (((0,), (0,)), ((), ())), preferred_element_type=F32)


def _fill_tables(inv_ref, cos_ref, sin_ref, mask_ref, qdec_ref, kdec_ref):
    seq = cos_ref.shape[0]
    for r0 in range(0, seq, ROW_TILE):
        pos = (lax.broadcasted_iota(jnp.int32, (ROW_TILE, B_DK), 0) + r0).astype(F32)
        cosf, sinf = _rope_rows(pos, inv_ref[...])
        cos_ref[r0:r0 + ROW_TILE, :] = cosf
        sin_ref[r0:r0 + ROW_TILE, :] = sinf
    row = lax.broadcasted_iota(jnp.int32, (CHUNK, CHUNK), 0).astype(F32)
    col = lax.broadcasted_iota(jnp.int32, (CHUNK, CHUNK), 1).astype(F32)
    diff = row - col
    for hh in range(B_HEADS):
        lg = LOG_GAMMA[hh]
        mask_ref[hh] = jnp.where(diff >= 0, jnp.exp(lg * jnp.maximum(diff, 0.0)), 0.0)
        qdec_ref[hh] = jnp.exp(lg * (row + 1.0))
        kdec_ref[hh] = jnp.exp(lg * (CHUNK - 1.0 - row))


def _prompt_kernel(x_ref, pre_mix_ref, w_in_ref, lnw_ref, lnb_ref, ws_ref, bs_ref, gn_ref,
                   w_out_ref, post_mix_ref, pre_mlp_ref, w_up_ref, w_down_ref, post_mlp_ref,
                   inv_ref, y_ref, s_ref, mix_scr, x1_scr, hn_scr,
                   cos_ref, sin_ref, mask_ref, qdec_ref, kdec_ref, *, tiles_per_seq, n_tiles):
    i = pl.program_id(0)
    refs = (x_ref, pre_mix_ref, w_in_ref, lnw_ref, lnb_ref, ws_ref, bs_ref, gn_ref,
            w_out_ref, post_mix_ref, pre_mlp_ref, w_up_ref, w_down_ref, post_mlp_ref,
            y_ref, s_ref, mix_scr, x1_scr, hn_scr,
            cos_ref, sin_ref, mask_ref, qdec_ref, kdec_ref)

    @pl.when(i == 0)
    def _():
        _fill_tables(inv_ref, cos_ref, sin_ref, mask_ref, qdec_ref, kdec_ref)

    @pl.when(jnp.logical_and(lax.rem(i, tiles_per_seq) == 0, i < n_tiles))
    def _():
        s_ref[...] = jnp.zeros_like(s_ref)

    @pl.when(i == 0)
    def _():
        _prompt_step(*refs, tiles_per_seq=tiles_per_seq, with_mixer=True, with_mlp=False)

    @pl.when(jnp.logical_and(i > 0, i < n_tiles))
    def _():
        _prompt_step(*refs, tiles_per_seq=tiles_per_seq, with_mixer=True, with_mlp=True)

    @pl.when(i == n_tiles)
    def _():
        _prompt_step(*refs, tiles_per_seq=tiles_per_seq, with_mixer=False, with_mlp=True)


def _prompt_step(x_ref, pre_mix_ref, w_in_ref, lnw_ref, lnb_ref, ws_ref, bs_ref, gn_ref,
                 w_out_ref, post_mix_ref, pre_mlp_ref, w_up_ref, w_down_ref, post_mlp_ref,
                 y_ref, s_ref, mix_scr, x1_scr, hn_scr,
                 cos_ref, sin_ref, mask_ref, qdec_ref, kdec_ref, *, tiles_per_seq, with_mixer, with_mlp):
    slot = lax.rem(pl.program_id(0), 2)
    n_chunks = ROW_TILE // CHUNK
    chunk_rows = [slice(c * CHUNK, (c + 1) * CHUNK) for c in range(n_chunks)]
    hn_prev = hn_scr.at[1 - slot]
    acc = []

    def mlp_piece(j):
        if not with_mlp:
            return
        cols = slice(j * FF_TILE, (j + 1) * FF_TILE)
        u = jnp.maximum(_dot(hn_prev[...], w_up_ref[:, cols]), 0.0)
        part = _dot((u * u).astype(BF16), w_down_ref[cols, :])
        acc[:] = [part if not acc else acc[0] + part]

    def mlp_finish():
        if with_mlp:
            y_ref[...] = x1_scr[1 - slot] + _rms(acc[0], post_mlp_ref[...])

    if not with_mixer:
        for j in range(D_FF // FF_TILE):
            mlp_piece(j)
        mlp_finish()
        return

    x = x_ref[...]
    mlp_piece(0)
    h = _rms(x, pre_mix_ref[...]).astype(BF16)
    z = _dot(h, w_in_ref[...])
    tile_pos = pl.multiple_of(lax.rem(pl.program_id(0), tiles_per_seq) * ROW_TILE, ROW_TILE)
    cosf = cos_ref[pl.ds(tile_pos, ROW_TILE), :]
    sinf = sin_ref[pl.ds(tile_pos, ROW_TILE), :]

    def head_inputs(hh):
        base = 2 * A_WIDTH + hh * B_DK
        q = _rope(z[:, base:base + B_DK], cosf, sinf)
        k = _rope(z[:, base + B_WIDTH:base + B_WIDTH + B_DK], cosf, sinf) * (B_DK ** -0.5)
        v = z[:, base + 2 * B_WIDTH:base + 2 * B_WIDTH + B_DV].astype(BF16)
        return {
            "q": q.astype(BF16),
            "k": k.astype(BF16),
            "v": v,
            "qd": [(q[r] * qdec_ref[hh]).astype(BF16) for r in chunk_rows],
            "kd": [(k[r] * kdec_ref[hh]).astype(BF16) for r in chunk_rows],
        }

    def head_dots_a(hd):
        hd["scores"] = [_dot_nt(hd["q"][r], hd["k"][r]) for r in chunk_rows]
        hd["kv"] = [_dot_tn(hd["kd"][c], hd["v"][r]) for c, r in enumerate(chunk_rows)]

    def head_chain(hh, hd):
        s = s_ref[hh]
        hd["s"] = []
        for c in range(n_chunks):
            hd["s"].append(s.astype(BF16))
            s = s * math.exp(LOG_GAMMA[hh] * CHUNK) + hd["kv"][c]
        s_ref[hh] = s
        hd["p"] = [(hd["scores"][c] * mask_ref[hh]).astype(BF16) for c in range(n_chunks)]

    def head_dots_b(hd):
        hd["o"] = [_dot(hd["p"][c], hd["v"][r]) + _dot(hd["qd"][c], hd["s"][c])
                   for c, r in enumerate(chunk_rows)]

    def head_finish(hh, hd):
        base = 2 * A_WIDTH + 3 * B_WIDTH + hh * B_DV
        gate = _silu(z[:, base:base + B_DV])
        gnw = gn_ref[:, hh * B_DV:(hh + 1) * B_DV]
        for c, r in enumerate(chunk_rows):
            o = _center_norm(hd["o"][c]) * gnw
            mix_scr[r, A_WIDTH + hh * B_DV:A_WIDTH + (hh + 1) * B_DV] = (o * gate[r]).astype(BF16)

    mlp_piece(1)
    ua, va = [], []
    for g in range(A_GROUPS):
        gc = slice(g * A_DIM, (g + 1) * A_DIM)
        ua.append(_gelu(z[:, g * A_DIM:(g + 1) * A_DIM]))
        vg = _gelu(z[:, A_WIDTH + g * A_DIM:A_WIDTH + (g + 1) * A_DIM])
        va.append((_center_norm(vg) * lnw_ref[:, gc] + lnb_ref[:, gc]).astype(BF16))
        if g == 1:
            mlp_piece(2)
    heads = [head_inputs(0)]

    row = lax.broadcasted_iota(jnp.int32, (CHUNK, CHUNK), 0)
    col = lax.broadcasted_iota(jnp.int32, (CHUNK, CHUNK), 1)
    mixed = []
    for g in range(A_GROUPS):
        w_tril = jnp.where(row >= col, ws_ref[g], 0.0).astype(BF16)
        mixed.append([_dot(w_tril, va[g][r]) for r in chunk_rows])
    head_dots_a(heads[0])

    for hh in range(B_HEADS):
        mlp_piece(3 + hh)
        if hh == 0:
            for g in range(A_GROUPS):
                gc = slice(g * A_DIM, (g + 1) * A_DIM)
                bias = jnp.broadcast_to(bs_ref[:, g:g + 1], (CHUNK, A_DIM))
                for c, r in enumerate(chunk_rows):
                    mix_scr[r, gc] = (ua[g][r] * (mixed[g][c] + bias)).astype(BF16)
        else:
            head_finish(hh - 1, heads[hh - 1])
        head_chain(hh, heads[hh])
        if hh + 1 < B_HEADS:
            heads.append(head_inputs(hh + 1))
        head_dots_b(heads[hh])
        if hh + 1 < B_HEADS:
            head_dots_a(heads[hh + 1])

    mlp_piece(7)
    head_finish(B_HEADS - 1, heads[B_HEADS - 1])
    half = ROW_TILE // 2
    mix_a = _dot(mix_scr[:half, :], w_out_ref[...])
    mlp_finish()
    mix_b = _dot(mix_scr[half:, :], w_out_ref[...])
    for rows, mix in ((slice(0, half), mix_a), (slice(half, ROW_TILE), mix_b)):
        x1 = x[rows] + _rms(mix, post_mix_ref[...])
        x1_scr[slot, rows, :] = x1
        hn_scr[slot, rows, :] = _rms(x1, pre_mlp_ref[...]).astype(BF16)


def _resident(shape):
    zeros = (0,) * len(shape)
    return pl.BlockSpec(shape, lambda i: zeros, pipeline_mode=pl.Buffered(1))


def _prompt_layer(x, p, inv_freq):
    batch, seq, _ = x.shape
    tiles_per_seq = seq // ROW_TILE
    n_tiles = batch * tiles_per_seq

    def mixer_tile(i):
        return jnp.minimum(i, n_tiles - 1)

    def mlp_tile(i):
        return jnp.maximum(i - 1, 0)

    x_spec = pl.BlockSpec((None, ROW_TILE, D_MODEL),
                          lambda i: (mixer_tile(i) // tiles_per_seq, mixer_tile(i) % tiles_per_seq, 0))
    y_spec = pl.BlockSpec((None, ROW_TILE, D_MODEL),
                          lambda i: (mlp_tile(i) // tiles_per_seq, mlp_tile(i) % tiles_per_seq, 0))
    in_specs = [
        x_spec,
        _resident((1, D_MODEL)),
        _resident((D_MODEL, IN_COLS)),
        _resident((1, A_WIDTH)),
        _resident((1, A_WIDTH)),
        _resident((A_GROUPS, CHUNK, CHUNK)),
        _resident((CHUNK, A_GROUPS)),
        _resident((1, B_WIDTH)),
        _resident((D_MODEL, D_MODEL)),
        _resident((1, D_MODEL)),
        _resident((1, D_MODEL)),
        _resident((D_MODEL, D_FF)),
        _resident((D_FF, D_MODEL)),
        _resident((1, D_MODEL)),
        _resident((1, B_DK)),
    ]
    out_specs = [
        y_spec,
        pl.BlockSpec((None, B_HEADS, B_DK, B_DV),
                     lambda i: (mixer_tile(i) // tiles_per_seq, 0, 0, 0)),
    ]
    return pl.pallas_call(
        functools.partial(_prompt_kernel, tiles_per_seq=tiles_per_seq, n_tiles=n_tiles),
        grid=(n_tiles + 1,),
        in_specs=in_specs,
        out_specs=out_specs,
        out_shape=[
            jax.ShapeDtypeStruct(x.shape, F32),
            jax.ShapeDtypeStruct((batch, B_HEADS, B_DK, B_DV), F32),
        ],
        scratch_shapes=[pltpu.VMEM((ROW_TILE, D_MODEL), BF16),
                        pltpu.VMEM((2, ROW_TILE, D_MODEL), F32),
                        pltpu.VMEM((2, ROW_TILE, D_MODEL), BF16),
                        pltpu.VMEM((seq, B_DK), F32),
                        pltpu.VMEM((seq, B_DK), F32),
                        pltpu.VMEM((B_HEADS, CHUNK, CHUNK), F32),
                        pltpu.VMEM((B_HEADS, CHUNK, B_DK), F32),
                        pltpu.VMEM((B_HEADS, CHUNK, B_DK), F32)],
        compiler_params=pltpu.CompilerParams(
            dimension_semantics=("arbitrary",),
            vmem_limit_bytes=VMEM_LIMIT_BYTES),
        name="prompt_layer",
    )(x, p["pre_mix_w"], p["w_in"], p["ln_v_w"], p["ln_v_b"], p["w_s"], p["b_s_t"], p["gn_w"],
      p["w_out"], p["post_mix_w"], p["pre_mlp_w"], p["w_up"], p["w_down"], p["post_mlp_w"],
      inv_freq)


def _decode_in_kernel(x_ref, pre_mix_ref, w_in_ref, lnw_ref, lnb_ref, ws0_ref, bs0_ref, inv_ref,
                      outa_ref, va_ref, q_ref, k_ref, v_ref, gate_ref, w_in_bf_ref,
                      h_scr, z_scr):
    kk = pl.program_id(0)

    @pl.when(kk == 0)
    def _():
        h = _rms(x_ref[...], pre_mix_ref[...]).astype(BF16)
        for c in range(D_MODEL // W_IN_ROWS):
            h_scr[c] = h[:, c * W_IN_ROWS:(c + 1) * W_IN_ROWS]
        z_scr[...] = jnp.zeros_like(z_scr)

    w = w_in_ref[...].astype(BF16)
    w_in_bf_ref[...] = w
    z_scr[...] += _dot(h_scr[kk], w)

    @pl.when(kk == pl.num_programs(0) - 1)
    def _():
        z = z_scr[...]
        cosf, sinf = _rope_rows(jnp.full((1, B_DK), PAST_LEN, F32), inv_ref[...])
        for g in range(A_GROUPS):
            gc = slice(g * A_DIM, (g + 1) * A_DIM)
            ua = _gelu(z[:, g * A_DIM:(g + 1) * A_DIM])
            va = _gelu(z[:, A_WIDTH + g * A_DIM:A_WIDTH + (g + 1) * A_DIM])
            va = _center_norm(va) * lnw_ref[:, gc] + lnb_ref[:, gc]
            va_ref[:, gc] = va
            outa_ref[:, gc] = ua * (ws0_ref[:, gc] * va + bs0_ref[:, gc])
        for hh in range(B_HEADS):
            hc = slice(hh * B_DK, (hh + 1) * B_DK)
            base = 2 * A_WIDTH
            q_ref[:, hc] = _rope(z[:, base + hh * B_DK:base + (hh + 1) * B_DK], cosf, sinf)
            k_ref[:, hc] = _rope(z[:, base + B_WIDTH + hh * B_DK:base + B_WIDTH + (hh + 1) * B_DK],
                                 cosf, sinf) * (B_DK ** -0.5)
        v_ref[...] = z[:, 2 * A_WIDTH + 2 * B_WIDTH:2 * A_WIDTH + 3 * B_WIDTH]
        gate_ref[...] = _silu(z[:, 2 * A_WIDTH + 3 * B_WIDTH:])


def _decode_state_kernel(q_ref, k_ref, v_ref, s_ref, o_ref, s_out_ref):
    rid = lax.broadcasted_iota(jnp.int32, (STATE_ROWS, B_DK), 0)
    for hh in range(B_HEADS):
        hc = slice(hh * B_DK, (hh + 1) * B_DK)
        q = q_ref[:, hc]
        k = k_ref[:, hc]
        v = v_ref[:, hc]
        gam = math.exp(LOG_GAMMA[hh])
        qk = jnp.sum(q * k, axis=-1, keepdims=True)
        vb = v.astype(BF16)
        qg = (q * gam).astype(BF16)
        cross_rows = []
        for j in range(STATE_ROWS):
            s = s_ref[j, hh]
            cross_rows.append(_dot(qg, s.astype(BF16))[j:j + 1, :])
            kj = jnp.where(rid == j, k, 0.0).astype(BF16)
            s_out_ref[j, hh] = s * gam + _dot_tn(kj, vb)
        o_ref[:, hc] = qk * v + jnp.concatenate(cross_rows, axis=0)


def _decode_out_kernel(x_ref, outa_ref, o_ref, gate_ref, gn_ref, w_out_ref, post_mix_ref,
                       pre_mlp_ref, w_up_ref, w_down_ref, post_mlp_ref,
                       y_ref, w_out_bf_ref, w_up_bf_ref, w_down_bf_ref,
                       x1_scr, hn_scr, f_scr):
    j = pl.program_id(0)

    @pl.when(j == 0)
    def _():
        w_out = w_out_ref[...].astype(BF16)
        w_out_bf_ref[...] = w_out
        parts = [outa_ref[...].astype(BF16)]
        for hh in range(B_HEADS):
            hc = slice(hh * B_DV, (hh + 1) * B_DV)
            o = _center_norm(o_ref[:, hc]) * gn_ref[:, hc]
            parts.append((o * gate_ref[:, hc]).astype(BF16))
        mix = _dot(jnp.concatenate(parts, axis=-1), w_out)
        x1 = x_ref[...] + _rms(mix, post_mix_ref[...])
        x1_scr[...] = x1
        hn_scr[...] = _rms(x1, pre_mlp_ref[...]).astype(BF16)
        f_scr[...] = jnp.zeros_like(f_scr)

    w_up = w_up_ref[...].astype(BF16)
    w_down = w_down_ref[...].astype(BF16)
    w_up_bf_ref[...] = w_up
    w_down_bf_ref[...] = w_down
    u = jnp.maximum(_dot(hn_scr[...], w_up), 0.0)
    f_scr[...] += _dot((u * u).astype(BF16), w_down)

    @pl.when(j == pl.num_programs(0) - 1)
    def _():
        y_ref[...] = x1_scr[...] + _rms(f_scr[...], post_mlp_ref[...])


def _full(shape):
    zeros = (0,) * len(shape)
    return pl.BlockSpec(shape, lambda *_: zeros)


def _decode_layer(x, s0, p, inv_freq):
    n = x.shape[0]
    act = jax.ShapeDtypeStruct((n, A_WIDTH), F32)
    n_k = D_MODEL // W_IN_ROWS
    w_in_blk = pl.BlockSpec((W_IN_ROWS, IN_COLS), lambda kk: (kk, 0))
    outa, va, q, k, v, gate, w_in_bf = pl.pallas_call(
        _decode_in_kernel,
        grid=(n_k,),
        in_specs=[_full((n, D_MODEL)), _full((1, D_MODEL)), w_in_blk,
                  _full((1, A_WIDTH)), _full((1, A_WIDTH)), _full((1, A_WIDTH)), _full((1, A_WIDTH)),
                  _full((1, B_DK))],
        out_specs=[_full((n, A_WIDTH))] * 6 + [w_in_blk],
        out_shape=[act] * 6 + [jax.ShapeDtypeStruct((D_MODEL, IN_COLS), BF16)],
        scratch_shapes=[pltpu.VMEM((n_k, n, W_IN_ROWS), BF16), pltpu.VMEM((n, IN_COLS), F32)],
        compiler_params=pltpu.CompilerParams(
            dimension_semantics=("arbitrary",), vmem_limit_bytes=VMEM_LIMIT_BYTES),
        name="decode_in",
    )(x, p["pre_mix_w"], p["w_in"], p["ln_v_w"], p["ln_v_b"], p["w_s0"], p["b_s0"], inv_freq)

    row_spec = pl.BlockSpec((STATE_ROWS, B_WIDTH), lambda i: (i, 0))
    st_spec = pl.BlockSpec((STATE_ROWS, B_HEADS, B_DK, B_DV), lambda i: (i, 0, 0, 0))
    o, s_new = pl.pallas_call(
        _decode_state_kernel,
        grid=(n // STATE_ROWS,),
        in_specs=[row_spec, row_spec, row_spec, st_spec],
        out_specs=[row_spec, st_spec],
        out_shape=[act, jax.ShapeDtypeStruct(s0.shape, F32)],
        compiler_params=pltpu.CompilerParams(
            dimension_semantics=("arbitrary",), vmem_limit_bytes=VMEM_LIMIT_BYTES),
        name="decode_state",
    )(q, k, v, s0)

    up_blk = pl.BlockSpec((D_MODEL, DEC_FF_TILE), lambda j: (0, j))
    down_blk = pl.BlockSpec((DEC_FF_TILE, D_MODEL), lambda j: (j, 0))
    y, w_out_bf, w_up_bf, w_down_bf = pl.pallas_call(
        _decode_out_kernel,
        grid=(D_FF // DEC_FF_TILE,),
        in_specs=[_full((n, D_MODEL)), _full((n, A_WIDTH)), _full((n, B_WIDTH)), _full((n, B_WIDTH)),
                  _full((1, B_WIDTH)), _full((D_MODEL, D_MODEL)), _full((1, D_MODEL)),
                  _full((1, D_MODEL)), up_blk, down_blk, _full((1, D_MODEL))],
        out_specs=[_full((n, D_MODEL)), _full((D_MODEL, D_MODEL)), up_blk, down_blk],
        out_shape=[jax.ShapeDtypeStruct((n, D_MODEL), F32),
                   jax.ShapeDtypeStruct((D_MODEL, D_MODEL), BF16),
                   jax.ShapeDtypeStruct((D_MODEL, D_FF), BF16),
                   jax.ShapeDtypeStruct((D_FF, D_MODEL), BF16)],
        scratch_shapes=[pltpu.VMEM((n, D_MODEL), F32), pltpu.VMEM((n, D_MODEL), BF16),
                        pltpu.VMEM((n, D_MODEL), F32)],
        compiler_params=pltpu.CompilerParams(
            dimension_semantics=("arbitrary",), vmem_limit_bytes=VMEM_LIMIT_BYTES),
        name="decode_out",
    )(x, outa, o, gate, p["gn_w"], p["w_out"], p["post_mix_w"], p["pre_mlp_w"],
      p["w_up"], p["w_down"], p["post_mlp_w"])
    weights_bf = {"w_in": w_in_bf, "w_out": w_out_bf, "w_up": w_up_bf, "w_down": w_down_bf}
    return y, s_new, va, weights_bf


def kernel(x_prompt, x_sample, state_ret, pre_mix_w, w_in, ln_v_w, ln_v_b, w_s, b_s, gn_w,
           w_out, post_mix_w, pre_mlp_w, w_up, w_down, post_mlp_w):
    depth = w_in.shape[0]
    seq = x_prompt.shape[1]
    n_dec, dec_seq, _ = x_sample.shape
    assert dec_seq == 1 and seq % ROW_TILE == 0 and n_dec % STATE_ROWS == 0

    half = B_DK // 2
    inv = ROPE_BASE ** (-jnp.arange(half, dtype=F32) / half)
    inv_freq = jnp.concatenate([inv, inv])[None, :]

    yp, ys = x_prompt, x_sample.reshape(n_dec, D_MODEL)
    sp_list, ss_list, vs_list = [], [], []
    for l in range(depth):
        p = {
            "pre_mix_w": pre_mix_w[l][None, :],
            "w_in": w_in[l],
            "ln_v_w": ln_v_w[l].reshape(1, A_WIDTH),
            "ln_v_b": ln_v_b[l].reshape(1, A_WIDTH),
            "w_s": w_s[l],
            "b_s_t": b_s[l].T,
            "w_s0": jnp.repeat(w_s[l][:, 0, 0], A_DIM)[None, :],
            "b_s0": jnp.repeat(b_s[l][:, 0], A_DIM)[None, :],
            "gn_w": gn_w[l].reshape(1, B_WIDTH),
            "w_out": w_out[l],
            "post_mix_w": post_mix_w[l][None, :],
            "pre_mlp_w": pre_mlp_w[l][None, :],
            "w_up": w_up[l],
            "w_down": w_down[l],
            "post_mlp_w": post_mlp_w[l][None, :],
        }
        ys, ss, vs, weights_bf = _decode_layer(ys, state_ret[l], p, inv_freq)
        yp, sp = _prompt_layer(yp, {**p, **weights_bf}, inv_freq)
        sp_list.append(sp)
        ss_list.append(ss)
        vs_list.append(vs.reshape(n_dec, 1, A_GROUPS, A_DIM))
    return (yp, ys.reshape(n_dec, 1, D_MODEL), jnp.stack(sp_list, axis=0),
            jnp.stack(ss_list, axis=0), jnp.stack(vs_list, axis=0))
```

```python
import functools
import math

import jax
import jax.numpy as jnp
from jax import lax
from jax.experimental import pallas as pl
from jax.experimental.pallas import tpu as pltpu

D_MODEL = 1024
A_WIDTH = 512
B_WIDTH = 512
A_GROUPS = 4
A_DIM = 128
CHUNK = 128
B_HEADS = 4
B_DK = 128
B_DV = 128
D_FF = 4096
IN_COLS = 2 * A_WIDTH + 4 * B_WIDTH
ROPE_BASE = 10000.0
EPS = 1e-6
PAST_LEN = 16384
LOG_GAMMA = [math.log(1.0 - 2.0 ** (-5.0 - h)) for h in range(B_HEADS)]

ROW_TILE = 512
FF_TILE = 512
STATE_ROWS = 16
W_IN_ROWS = 512
DEC_FF_TILE = 1024
VMEM_LIMIT_BYTES = 56 * 1024 * 1024

F32 = jnp.float32
BF16 = jnp.bfloat16


def _rms(x, w):
    return x * lax.rsqrt(jnp.mean(x * x, axis=-1, keepdims=True) + EPS) * w


def _center_norm(x):
    mu = jnp.mean(x, axis=-1, keepdims=True)
    xc = x - mu
    return xc * lax.rsqrt(jnp.mean(xc * xc, axis=-1, keepdims=True) + EPS)


def _gelu(x):
    return jax.nn.gelu(x, approximate=True)


def _silu(x):
    return x * (1.0 / (1.0 + jnp.exp(-x)))


def _rope(x, cosf, sinf):
    return x * cosf + pltpu.roll(x, B_DK // 2, axis=1) * sinf


def _rope_rows(pos, inv):
    ang = pos * inv
    lane = lax.broadcasted_iota(jnp.int32, ang.shape, 1)
    sin = jnp.sin(ang)
    return jnp.cos(ang), jnp.where(lane < B_DK // 2, -sin, sin)


def _dot(a, b):
    return jnp.dot(a, b, preferred_element_type=F32)


def _dot_nt(a, b):
    return lax.dot_general(a, b, (((1,), (1,)), ((), ())), preferred_element_type=F32)


def _dot_tn(a, b):
    return lax.dot_general(a, b, (((0,), (0,)), ((), ())), preferred_element_type=F32)


def _fill_tables(inv_ref, cos_ref, sin_ref, mask_ref, qdec_ref, kdec_ref):
    seq = cos_ref.shape[0]
    for r0 in range(0, seq, ROW_TILE):
        pos = (lax.broadcasted_iota(jnp.int32, (ROW_TILE, B_DK), 0) + r0).astype(F32)
        cosf, sinf = _rope_rows(pos, inv_ref[...])
        cos_ref[r0:r0 + ROW_TILE, :] = cosf
        sin_ref[r0:r0 + ROW_TILE, :] = sinf
    row = lax.broadcasted_iota(jnp.int32, (CHUNK, CHUNK), 0).astype(F32)
    col = lax.broadcasted_iota(jnp.int32, (CHUNK, CHUNK), 1).astype(F32)
    diff = row - col
    for hh in range(B_HEADS):
        lg = LOG_GAMMA[hh]
        mask_ref[hh] = jnp.where(diff >= 0, jnp.exp(lg * jnp.maximum(diff, 0.0)), 0.0)
        qdec_ref[hh] = jnp.exp(lg * (row + 1.0))
        kdec_ref[hh] = jnp.exp(lg * (CHUNK - 1.0 - row))


def _prompt_kernel(x_ref, pre_mix_ref, w_in_ref, lnw_ref, lnb_ref, ws_ref, bs_ref, gn_ref,
                   w_out_ref, post_mix_ref, pre_mlp_ref, w_up_ref, w_down_ref, post_mlp_ref,
                   inv_ref, y_ref, s_ref, mix_scr, x1_scr, hn_scr,
                   cos_ref, sin_ref, mask_ref, qdec_ref, kdec_ref, *, tiles_per_seq, n_tiles):
    i = pl.program_id(0)
    refs = (x_ref, pre_mix_ref, w_in_ref, lnw_ref, lnb_ref, ws_ref, bs_ref, gn_ref,
            w_out_ref, post_mix_ref, pre_mlp_ref, w_up_ref, w_down_ref, post_mlp_ref,
            y_ref, s_ref, mix_scr, x1_scr, hn_scr,
            cos_ref, sin_ref, mask_ref, qdec_ref, kdec_ref)

    @pl.when(i == 0)
    def _():
        _fill_tables(inv_ref, cos_ref, sin_ref, mask_ref, qdec_ref, kdec_ref)

    @pl.when(jnp.logical_and(lax.rem(i, tiles_per_seq) == 0, i < n_tiles))
    def _():
        s_ref[...] = jnp.zeros_like(s_ref)

    @pl.when(i == 0)
    def _():
        _prompt_step(*refs, tiles_per_seq=tiles_per_seq, with_mixer=True, with_mlp=False)

    @pl.when(jnp.logical_and(i > 0, i < n_tiles))
    def _():
        _prompt_step(*refs, tiles_per_seq=tiles_per_seq, with_mixer=True, with_mlp=True)

    @pl.when(i == n_tiles)
    def _():
        _prompt_step(*refs, tiles_per_seq=tiles_per_seq, with_mixer=False, with_mlp=True)


def _prompt_step(x_ref, pre_mix_ref, w_in_ref, lnw_ref, lnb_ref, ws_ref, bs_ref, gn_ref,
                 w_out_ref, post_mix_ref, pre_mlp_ref, w_up_ref, w_down_ref, post_mlp_ref,
                 y_ref, s_ref, mix_scr, x1_scr, hn_scr,
                 cos_ref, sin_ref, mask_ref, qdec_ref, kdec_ref, *, tiles_per_seq, with_mixer, with_mlp):
    slot = lax.rem(pl.program_id(0), 2)
    n_chunks = ROW_TILE // CHUNK
    chunk_rows = [slice(c * CHUNK, (c + 1) * CHUNK) for c in range(n_chunks)]
    hn_prev = hn_scr.at[1 - slot]
    acc = []

    def mlp_piece(j):
        if not with_mlp:
            return
        cols = slice(j * FF_TILE, (j + 1) * FF_TILE)
        u = jnp.maximum(_dot(hn_prev[...], w_up_ref[:, cols]), 0.0)
        part = _dot((u * u).astype(BF16), w_down_ref[cols, :])
        acc[:] = [part if not acc else acc[0] + part]

    def mlp_finish():
        if with_mlp:
            y_ref[...] = x1_scr[1 - slot] + _rms(acc[0], post_mlp_ref[...])

    if not with_mixer:
        for j in range(D_FF // FF_TILE):
            mlp_piece(j)
        mlp_finish()
        return

    x = x_ref[...]
    mlp_piece(0)
    h = _rms(x, pre_mix_ref[...]).astype(BF16)
    z = _dot(h, w_in_ref[...])
    tile_pos = pl.multiple_of(lax.rem(pl.program_id(0), tiles_per_seq) * ROW_TILE, ROW_TILE)
    cosf = cos_ref[pl.ds(tile_pos, ROW_TILE), :]
    sinf = sin_ref[pl.ds(tile_pos, ROW_TILE), :]

    def head_inputs(hh):
        base = 2 * A_WIDTH + hh * B_DK
        q = _rope(z[:, base:base + B_DK], cosf, sinf)
        k = _rope(z[:, base + B_WIDTH:base + B_WIDTH + B_DK], cosf, sinf) * (B_DK ** -0.5)
        v = z[:, base + 2 * B_WIDTH:base + 2 * B_WIDTH + B_DV].astype(BF16)
        return {
            "q": q.astype(BF16),
            "k": k.astype(BF16),
            "v": v,
            "qd": [(q[r] * qdec_ref[hh]).astype(BF16) for r in chunk_rows],
            "kd": [(k[r] * kdec_ref[hh]).astype(BF16) for r in chunk_rows],
        }

    def head_dots_a(hd):
        hd["scores"] = [_dot_nt(hd["q"][r], hd["k"][r]) for r in chunk_rows]
        hd["kv"] = [_dot_tn(hd["kd"][c], hd["v"][r]) for c, r in enumerate(chunk_rows)]

    def head_chain(hh, hd):
        s = s_ref[hh]
        hd["s"] = []
        for c in range(n_chunks):
            hd["s"].append(s.astype(BF16))
            s = s * math.exp(LOG_GAMMA[hh] * CHUNK) + hd["kv"][c]
        s_ref[hh] = s
        hd["p"] = [(hd["scores"][c] * mask_ref[hh]).astype(BF16) for c in range(n_chunks)]

    def head_dots_b(hd):
        hd["o"] = [_dot(hd["p"][c], hd["v"][r]) + _dot(hd["qd"][c], hd["s"][c])
                   for c, r in enumerate(chunk_rows)]

    def head_finish(hh, hd):
        base = 2 * A_WIDTH + 3 * B_WIDTH + hh * B_DV
        gate = _silu(z[:, base:base + B_DV])
        gnw = gn_ref[:, hh * B_DV:(hh + 1) * B_DV]
        for c, r in enumerate(chunk_rows):
            o = _center_norm(hd["o"][c]) * gnw
            mix_scr[r, A_WIDTH + hh * B_DV:A_WIDTH + (hh + 1) * B_DV] = (o * gate[r]).astype(BF16)

    mlp_piece(1)
    ua, va = [], []
    for g in range(A_GROUPS):
        gc = slice(g * A_DIM, (g + 1) * A_DIM)
        ua.append(_gelu(z[:, g * A_DIM:(g + 1) * A_DIM]))
        vg = _gelu(z[:, A_WIDTH + g * A_DIM:A_WIDTH + (g + 1) * A_DIM])
        va.append((_center_norm(vg) * lnw_ref[:, gc] + lnb_ref[:, gc]).astype(BF16))
        if g == 1:
            mlp_piece(2)
    heads = [head_inputs(0)]

    row = lax.broadcasted_iota(jnp.int32, (CHUNK, CHUNK), 0)
    col = lax.broadcasted_iota(jnp.int32, (CHUNK, CHUNK), 1)
    mixed = []
    for g in range(A_GROUPS):
        w_tril = jnp.where(row >= col, ws_ref[g], 0.0).astype(BF16)
        mixed.append([_dot(w_tril, va[g][r]) for r in chunk_rows])
    head_dots_a(heads[0])

    for hh in range(B_HEADS):
        mlp_piece(3 + hh)
        if hh == 0:
            for g in range(A_GROUPS):
                gc = slice(g * A_DIM, (g + 1) * A_DIM)
                bias = jnp.broadcast_to(bs_ref[:, g:g + 1], (CHUNK, A_DIM))
                for c, r in enumerate(chunk_rows):
                    mix_scr[r, gc] = (ua[g][r] * (mixed[g][c] + bias)).astype(BF16)
        else:
            head_finish(hh - 1, heads[hh - 1])
        head_chain(hh, heads[hh])
        if hh + 1 < B_HEADS:
            heads.append(head_inputs(hh + 1))
        head_dots_b(heads[hh])
        if hh + 1 < B_HEADS:
            head_dots_a(heads[hh + 1])

    mlp_piece(7)
    head_finish(B_HEADS - 1, heads[B_HEADS - 1])
    half = ROW_TILE // 2
    mix_a = _dot(mix_scr[:half, :], w_out_ref[...])
    mlp_finish()
    mix_b = _dot(mix_scr[half:, :], w_out_ref[...])
    for rows, mix in ((slice(0, half), mix_a), (slice(half, ROW_TILE), mix_b)):
        x1 = x[rows] + _rms(mix, post_mix_ref[...])
        x1_scr[slot, rows, :] = x1
        hn_scr[slot, rows, :] = _rms(x1, pre_mlp_ref[...]).astype(BF16)


def _resident(shape):
    zeros = (0,) * len(shape)
    return pl.BlockSpec(shape, lambda i: zeros, pipeline_mode=pl.Buffered(1))


def _prompt_layer(x, p, inv_freq):
    batch, seq, _ = x.shape
    tiles_per_seq = seq // ROW_TILE
    n_tiles = batch * tiles_per_seq

    def mixer_tile(i):
        return jnp.minimum(i, n_tiles - 1)

    def mlp_tile(i):
        return jnp.maximum(i - 1, 0)

    x_spec = pl.BlockSpec((None, ROW_TILE, D_MODEL),
                          lambda i: (mixer_tile(i) // tiles_per_seq, mixer_tile(i) % tiles_per_seq, 0))
    y_spec = pl.BlockSpec((None, ROW_TILE, D_MODEL),
                          lambda i: (mlp_tile(i) // tiles_per_seq, mlp_tile(i) % tiles_per_seq, 0))
    in_specs = [
        x_spec,
        _resident((1, D_MODEL)),
        _resident((D_MODEL, IN_COLS)),
        _resident((1, A_WIDTH)),
        _resident((1, A_WIDTH)),
        _resident((A_GROUPS, CHUNK, CHUNK)),
        _resident((CHUNK, A_GROUPS)),
        _resident((1, B_WIDTH)),
        _resident((D_MODEL, D_MODEL)),
        _resident((1, D_MODEL)),
        _resident((1, D_MODEL)),
        _resident((D_MODEL, D_FF)),
        _resident((D_FF, D_MODEL)),
        _resident((1, D_MODEL)),
        _resident((1, B_DK)),
    ]
    out_specs = [
        y_spec,
        pl.BlockSpec((None, B_HEADS, B_DK, B_DV),
                     lambda i: (mixer_tile(i) // tiles_per_seq, 0, 0, 0)),
    ]
    return pl.pallas_call(
        functools.partial(_prompt_kernel, tiles_per_seq=tiles_per_seq, n_tiles=n_tiles),
        grid=(n_tiles + 1,),
        in_specs=in_specs,
        out_specs=out_specs,
        out_shape=[
            jax.ShapeDtypeStruct(x.shape, F32),
            jax.ShapeDtypeStruct((batch, B_HEADS, B_DK, B_DV), F32),
        ],
        scratch_shapes=[pltpu.VMEM((ROW_TILE, D_MODEL), BF16),
                        pltpu.VMEM((2, ROW_TILE, D_MODEL), F32),
                        pltpu.VMEM((2, ROW_TILE, D_MODEL), BF16),
                        pltpu.VMEM((seq, B_DK), F32),
                        pltpu.VMEM((seq, B_DK), F32),
                        pltpu.VMEM((B_HEADS, CHUNK, CHUNK), F32),
                        pltpu.VMEM((B_HEADS, CHUNK, B_DK), F32),
                        pltpu.VMEM((B_HEADS, CHUNK, B_DK), F32)],
        compiler_params=pltpu.CompilerParams(
            dimension_semantics=("arbitrary",),
            vmem_limit_bytes=VMEM_LIMIT_BYTES),
        name="prompt_layer",
    )(x, p["pre_mix_w"], p["w_in"], p["ln_v_w"], p["ln_v_b"], p["w_s"], p["b_s_t"], p["gn_w"],
      p["w_out"], p["post_mix_w"], p["pre_mlp_w"], p["w_up"], p["w_down"], p["post_mlp_w"],
      inv_freq)


def _decode_in_kernel(x_ref, pre_mix_ref, w_in_ref, lnw_ref, lnb_ref, ws0_ref, bs0_ref, inv_ref,
                      outa_ref, va_ref, q_ref, k_ref, v_ref, gate_ref, w_in_bf_ref,
                      h_scr, z_scr):
    kk = pl.program_id(0)

    @pl.when(kk == 0)
    def _():
        h = _rms(x_ref[...], pre_mix_ref[...]).astype(BF16)
        for c in range(D_MODEL // W_IN_ROWS):
            h_scr[c] = h[:, c * W_IN_ROWS:(c + 1) * W_IN_ROWS]
        z_scr[...] = jnp.zeros_like(z_scr)

    w = w_in_ref[...].astype(BF16)
    w_in_bf_ref[...] = w
    z_scr[...] += _dot(h_scr[kk], w)

    @pl.when(kk == pl.num_programs(0) - 1)
    def _():
        z = z_scr[...]
        cosf, sinf = _rope_rows(jnp.full((1, B_DK), PAST_LEN, F32), inv_ref[...])
        for g in range(A_GROUPS):
            gc = slice(g * A_DIM, (g + 1) * A_DIM)
            ua = _gelu(z[:, g * A_DIM:(g + 1) * A_DIM])
            va = _gelu(z[:, A_WIDTH + g * A_DIM:A_WIDTH + (g + 1) * A_DIM])
            va = _center_norm(va) * lnw_ref[:, gc] + lnb_ref[:, gc]
            va_ref[:, g, :] = va
            outa_ref[:, gc] = ua * (ws0_ref[:, gc] * va + bs0_ref[:, gc])
        for hh in range(B_HEADS):
            hc = slice(hh * B_DK, (hh + 1) * B_DK)
            base = 2 * A_WIDTH
            q_ref[:, hc] = _rope(z[:, base + hh * B_DK:base + (hh + 1) * B_DK], cosf, sinf)
            k_ref[:, hc] = _rope(z[:, base + B_WIDTH + hh * B_DK:base + B_WIDTH + (hh + 1) * B_DK],
                                 cosf, sinf) * (B_DK ** -0.5)
        v_ref[...] = z[:, 2 * A_WIDTH + 2 * B_WIDTH:2 * A_WIDTH + 3 * B_WIDTH]
        gate_ref[...] = _silu(z[:, 2 * A_WIDTH + 3 * B_WIDTH:])


def _decode_state_kernel(q_ref, k_ref, v_ref, s_ref, o_ref, s_out_ref):
    rid = lax.broadcasted_iota(jnp.int32, (STATE_ROWS, B_DK), 0)
    for hh in range(B_HEADS):
        hc = slice(hh * B_DK, (hh + 1) * B_DK)
        q = q_ref[:, hc]
        k = k_ref[:, hc]
        v = v_ref[:, hc]
        gam = math.exp(LOG_GAMMA[hh])
        qk = jnp.sum(q * k, axis=-1, keepdims=True)
        vb = v.astype(BF16)
        qg = (q * gam).astype(BF16)
        cross_rows = []
        for j in range(STATE_ROWS):
            s = s_ref[j, hh]
            cross_rows.append(_dot(qg, s.astype(BF16))[j:j + 1, :])
            kj = jnp.where(rid == j, k, 0.0).astype(BF16)
            s_out_ref[j, hh] = s * gam + _dot_tn(kj, vb)
        o_ref[:, hc] = qk * v + jnp.concatenate(cross_rows, axis=0)


def _decode_out_kernel(x_ref, outa_ref, o_ref, gate_ref, gn_ref, w_out_ref, post_mix_ref,
                       pre_mlp_ref, w_up_ref, w_down_ref, post_mlp_ref,
                       y_ref, w_out_bf_ref, w_up_bf_ref, w_down_bf_ref,
                       x1_scr, hn_scr, f_scr):
    j = pl.program_id(0)

    @pl.when(j == 0)
    def _():
        w_out = w_out_ref[...].astype(BF16)
        w_out_bf_ref[...] = w_out
        parts = [outa_ref[...].astype(BF16)]
        for hh in range(B_HEADS):
            hc = slice(hh * B_DV, (hh + 1) * B_DV)
            o = _center_norm(o_ref[:, hc]) * gn_ref[:, hc]
            parts.append((o * gate_ref[:, hc]).astype(BF16))
        mix = _dot(jnp.concatenate(parts, axis=-1), w_out)
        x1 = x_ref[...] + _rms(mix, post_mix_ref[...])
        x1_scr[...] = x1
        hn_scr[...] = _rms(x1, pre_mlp_ref[...]).astype(BF16)
        f_scr[...] = jnp.zeros_like(f_scr)

    w_up = w_up_ref[...].astype(BF16)
    w_down = w_down_ref[...].astype(BF16)
    w_up_bf_ref[...] = w_up
    w_down_bf_ref[...] = w_down
    u = jnp.maximum(_dot(hn_scr[...], w_up), 0.0)
    f_scr[...] += _dot((u * u).astype(BF16), w_down)

    @pl.when(j == pl.num_programs(0) - 1)
    def _():
        y_ref[...] = x1_scr[...] + _rms(f_scr[...], post_mlp_ref[...])


def _full(shape):
    zeros = (0,) * len(shape)
    return pl.BlockSpec(shape, lambda *_: zeros)


def _decode_layer(x, s0, p, inv_freq):
    n = x.shape[0]
    act = jax.ShapeDtypeStruct((n, A_WIDTH), F32)
    n_k = D_MODEL // W_IN_ROWS
    w_in_blk = pl.BlockSpec((W_IN_ROWS, IN_COLS), lambda kk: (kk, 0))
    tok_spec = pl.BlockSpec((n, None, D_MODEL), lambda *_: (0, 0, 0))
    va_spec = pl.BlockSpec((n, None, A_GROUPS, A_DIM), lambda *_: (0, 0, 0, 0))
    outa, va, q, k, v, gate, w_in_bf = pl.pallas_call(
        _decode_in_kernel,
        grid=(n_k,),
        in_specs=[tok_spec, _full((1, D_MODEL)), w_in_blk,
                  _full((1, A_WIDTH)), _full((1, A_WIDTH)), _full((1, A_WIDTH)), _full((1, A_WIDTH)),
                  _full((1, B_DK))],
        out_specs=[_full((n, A_WIDTH)), va_spec] + [_full((n, A_WIDTH))] * 4 + [w_in_blk],
        out_shape=[act, jax.ShapeDtypeStruct((n, 1, A_GROUPS, A_DIM), F32)] + [act] * 4
                  + [jax.ShapeDtypeStruct((D_MODEL, IN_COLS), BF16)],
        scratch_shapes=[pltpu.VMEM((n_k, n, W_IN_ROWS), BF16), pltpu.VMEM((n, IN_COLS), F32)],
        compiler_params=pltpu.CompilerParams(
            dimension_semantics=("arbitrary",), vmem_limit_bytes=VMEM_LIMIT_BYTES),
        name="decode_in",
    )(x, p["pre_mix_w"], p["w_in"], p["ln_v_w"], p["ln_v_b"], p["w_s0"], p["b_s0"], inv_freq)

    row_spec = pl.BlockSpec((STATE_ROWS, B_WIDTH), lambda i: (i, 0))
    st_spec = pl.BlockSpec((STATE_ROWS, B_HEADS, B_DK, B_DV), lambda i: (i, 0, 0, 0))
    o, s_new = pl.pallas_call(
        _decode_state_kernel,
        grid=(n // STATE_ROWS,),
        in_specs=[row_spec, row_spec, row_spec, st_spec],
        out_specs=[row_spec, st_spec],
        out_shape=[act, jax.ShapeDtypeStruct(s0.shape, F32)],
        compiler_params=pltpu.CompilerParams(
            dimension_semantics=("arbitrary",), vmem_limit_bytes=VMEM_LIMIT_BYTES),
        name="decode_state",
    )(q, k, v, s0)

    up_blk = pl.BlockSpec((D_MODEL, DEC_FF_TILE), lambda j: (0, j))
    down_blk = pl.BlockSpec((DEC_FF_TILE, D_MODEL), lambda j: (j, 0))
    y, w_out_bf, w_up_bf, w_down_bf = pl.pallas_call(
        _decode_out_kernel,
        grid=(D_FF // DEC_FF_TILE,),
        in_specs=[tok_spec, _full((n, A_WIDTH)), _full((n, B_WIDTH)), _full((n, B_WIDTH)),
                  _full((1, B_WIDTH)), _full((D_MODEL, D_MODEL)), _full((1, D_MODEL)),
                  _full((1, D_MODEL)), up_blk, down_blk, _full((1, D_MODEL))],
        out_specs=[tok_spec, _full((D_MODEL, D_MODEL)), up_blk, down_blk],
        out_shape=[jax.ShapeDtypeStruct((n, 1, D_MODEL), F32),
                   jax.ShapeDtypeStruct((D_MODEL, D_MODEL), BF16),
                   jax.ShapeDtypeStruct((D_MODEL, D_FF), BF16),
                   jax.ShapeDtypeStruct((D_FF, D_MODEL), BF16)],
        scratch_shapes=[pltpu.VMEM((n, D_MODEL), F32), pltpu.VMEM((n, D_MODEL), BF16),
                        pltpu.VMEM((n, D_MODEL), F32)],
        compiler_params=pltpu.CompilerParams(
            dimension_semantics=("arbitrary",), vmem_limit_bytes=VMEM_LIMIT_BYTES),
        name="decode_out",
    )(x, outa, o, gate, p["gn_w"], p["w_out"], p["post_mix_w"], p["pre_mlp_w"],
      p["w_up"], p["w_down"], p["post_mlp_w"])
    weights_bf = {"w_in": w_in_bf, "w_out": w_out_bf, "w_up": w_up_bf, "w_down": w_down_bf}
    return y, s_new, va, weights_bf


def kernel(x_prompt, x_sample, state_ret, pre_mix_w, w_in, ln_v_w, ln_v_b, w_s, b_s, gn_w,
           w_out, post_mix_w, pre_mlp_w, w_up, w_down, post_mlp_w):
    depth = w_in.shape[0]
    seq = x_prompt.shape[1]
    n_dec, dec_seq, _ = x_sample.shape
    assert dec_seq == 1 and seq % ROW_TILE == 0 and n_dec % STATE_ROWS == 0

    half = B_DK // 2
    inv = ROPE_BASE ** (-jnp.arange(half, dtype=F32) / half)
    inv_freq = jnp.concatenate([inv, inv])[None, :]

    yp, ys = x_prompt, x_sample
    sp_list, ss_list, vs_list = [], [], []
    for l in range(depth):
        p = {
            "pre_mix_w": pre_mix_w[l][None, :],
            "w_in": w_in[l],
            "ln_v_w": ln_v_w[l].reshape(1, A_WIDTH),
            "ln_v_b": ln_v_b[l].reshape(1, A_WIDTH),
            "w_s": w_s[l],
            "b_s_t": b_s[l].T,
            "w_s0": jnp.repeat(w_s[l][:, 0, 0], A_DIM)[None, :],
            "b_s0": jnp.repeat(b_s[l][:, 0], A_DIM)[None, :],
            "gn_w": gn_w[l].reshape(1, B_WIDTH),
            "w_out": w_out[l],
            "post_mix_w": post_mix_w[l][None, :],
            "pre_mlp_w": pre_mlp_w[l][None, :],
            "w_up": w_up[l],
            "w_down": w_down[l],
            "post_mlp_w": post_mlp_w[l][None, :],
        }
        ys, ss, vs, weights_bf = _decode_layer(ys, state_ret[l], p, inv_freq)
        yp, sp = _prompt_layer(yp, {**p, **weights_bf}, inv_freq)
        sp_list.append(sp)
        ss_list.append(ss)
        vs_list.append(vs)
    return (yp, ys, jnp.stack(sp_list, axis=0),
            jnp.stack(ss_list, axis=0), jnp.stack(vs_list, axis=0))
```

```python
import functools
import math
from typing import Any, NamedTuple

import jax
import jax.numpy as jnp
from jax import lax
from jax.experimental import pallas as pl
from jax.experimental.pallas import tpu as pltpu

D_MODEL = 1024
A_WIDTH = 512
B_WIDTH = 512
A_GROUPS = 4
A_DIM = 128
CHUNK = 128
B_HEADS = 4
B_DK = 128
B_DV = 128
D_FF = 4096
IN_COLS = 2 * A_WIDTH + 4 * B_WIDTH
ROPE_BASE = 10000.0
EPS = 1e-6
PAST_LEN = 16384
LOG_GAMMA = [math.log(1.0 - 2.0 ** (-5.0 - h)) for h in range(B_HEADS)]

ROW_TILE = 512
FF_TILE = 512
DEC_ROWS = 4
DEC_GROUP = 16
STAGE_BYTES = 1 << 20
VMEM_LIMIT_BYTES = 60 * 1024 * 1024

F32 = jnp.float32
BF16 = jnp.bfloat16


def _rms(x, w):
    return x * lax.rsqrt(jnp.mean(x * x, axis=-1, keepdims=True) + EPS) * w


def _center_norm(x):
    mu = jnp.mean(x, axis=-1, keepdims=True)
    xc = x - mu
    return xc * lax.rsqrt(jnp.mean(xc * xc, axis=-1, keepdims=True) + EPS)


def _gelu(x):
    return jax.nn.gelu(x, approximate=True)


def _silu(x):
    return x * (1.0 / (1.0 + jnp.exp(-x)))


def _rope(x, cosf, sinf):
    return x * cosf + pltpu.roll(x, B_DK // 2, axis=1) * sinf


def _rope_rows(pos, inv):
    ang = pos * inv
    lane = lax.broadcasted_iota(jnp.int32, ang.shape, 1)
    sin = jnp.sin(ang)
    return jnp.cos(ang), jnp.where(lane < B_DK // 2, -sin, sin)


def _dot(a, b):
    return jnp.dot(a, b, preferred_element_type=F32)


def _dot_nt(a, b):
    return lax.dot_general(a, b, (((1,), (1,)), ((), ())), preferred_element_type=F32)


def _dot_tn(a, b):
    return lax.dot_general(a, b, (((0,), (0,)), ((), ())), preferred_element_type=F32)


class _Refs(NamedTuple):
    x: Any
    xs: Any
    s0: Any
    pre_mix: Any
    lnw: Any
    lnb: Any
    ws: Any
    bs: Any
    ws0: Any
    bs0: Any
    gn: Any
    post_mix: Any
    pre_mlp: Any
    post_mlp: Any
    inv: Any
    w_in_hbm: Any
    w_out_hbm: Any
    w_up_hbm: Any
    w_down_hbm: Any
    y: Any
    s: Any
    ys: Any
    s_dec: Any
    va: Any
    w_in: Any
    w_out: Any
    w_up: Any
    w_down: Any
    mix: Any
    x1: Any
    hn: Any
    cos: Any
    sin: Any
    mask: Any
    qdec: Any
    kdec: Any
    dq: Any
    dk: Any
    dv: Any
    dgate: Any
    douta: Any
    do: Any


def _convert_weight(w_hbm, w_bf):
    n_rows, n_cols = w_hbm.shape
    rows_per = 1 << int(math.log2(STAGE_BYTES // (4 * n_cols)))
    assert n_rows % rows_per == 0 and rows_per % 16 == 0
    n_chunks = n_rows // rows_per

    def body(stage, sem):
        def copy(c, slot):
            return pltpu.make_async_copy(
                w_hbm.at[pl.ds(c * rows_per, rows_per), :], stage.at[slot], sem.at[slot])

        copy(0, 0).start()

        def step(c, carry):
            slot = lax.rem(c, 2)

            @pl.when(c + 1 < n_chunks)
            def _():
                copy(c + 1, 1 - slot).start()

            copy(c, slot).wait()
            rows = pl.ds(pl.multiple_of(c * rows_per, rows_per), rows_per)
            w_bf[rows, :] = stage[slot].astype(BF16)
            return carry

        lax.fori_loop(0, n_chunks, step, 0)

    pl.run_scoped(body, pltpu.VMEM((2, rows_per, n_cols), F32), pltpu.SemaphoreType.DMA((2,)))


def _fill_tables(r):
    seq = r.cos.shape[0]
    for r0 in range(0, seq, ROW_TILE):
        pos = (lax.broadcasted_iota(jnp.int32, (ROW_TILE, B_DK), 0) + r0).astype(F32)
        cosf, sinf = _rope_rows(pos, r.inv[...])
        r.cos[r0:r0 + ROW_TILE, :] = cosf
        r.sin[r0:r0 + ROW_TILE, :] = sinf
    row = lax.broadcasted_iota(jnp.int32, (CHUNK, CHUNK), 0).astype(F32)
    col = lax.broadcasted_iota(jnp.int32, (CHUNK, CHUNK), 1).astype(F32)
    diff = row - col
    for hh in range(B_HEADS):
        lg = LOG_GAMMA[hh]
        r.mask[hh] = jnp.where(diff >= 0, jnp.exp(lg * jnp.maximum(diff, 0.0)), 0.0)
        r.qdec[hh] = jnp.exp(lg * (row + 1.0))
        r.kdec[hh] = jnp.exp(lg * (CHUNK - 1.0 - row))


def _decode_projections(r):
    h = _rms(r.xs[...], r.pre_mix[...]).astype(BF16)
    z = _dot(h, r.w_in[...])
    cosf, sinf = _rope_rows(jnp.full((1, B_DK), PAST_LEN, F32), r.inv[...])
    for g in range(A_GROUPS):
        gc = slice(g * A_DIM, (g + 1) * A_DIM)
        ua = _gelu(z[:, g * A_DIM:(g + 1) * A_DIM])
        va = _gelu(z[:, A_WIDTH + g * A_DIM:A_WIDTH + (g + 1) * A_DIM])
        va = _center_norm(va) * r.lnw[:, gc] + r.lnb[:, gc]
        r.va[:, g, :] = va
        r.douta[:, gc] = ua * (r.ws0[:, gc] * va + r.bs0[:, gc])
    base = 2 * A_WIDTH
    for hh in range(B_HEADS):
        hc = slice(hh * B_DK, (hh + 1) * B_DK)
        r.dq[:, hc] = _rope(z[:, base + hh * B_DK:base + (hh + 1) * B_DK], cosf, sinf)
        r.dk[:, hc] = _rope(z[:, base + B_WIDTH + hh * B_DK:base + B_WIDTH + (hh + 1) * B_DK],
                            cosf, sinf) * (B_DK ** -0.5)
    r.dv[...] = z[:, base + 2 * B_WIDTH:base + 3 * B_WIDTH]
    r.dgate[...] = _silu(z[:, base + 3 * B_WIDTH:])
    r.do[...] = jnp.zeros_like(r.do)


def _decode_state_step(r):
    i = pl.program_id(0)
    per_group = DEC_GROUP // DEC_ROWS
    rows = pl.ds(pl.multiple_of((i // per_group) * DEC_GROUP, DEC_GROUP), DEC_GROUP)
    first = lax.rem(i, per_group) * DEC_ROWS
    rid = lax.broadcasted_iota(jnp.int32, (DEC_GROUP, B_DK), 0)
    live = jnp.logical_and(rid >= first, rid < first + DEC_ROWS)
    for hh in range(B_HEADS):
        hc = slice(hh * B_DK, (hh + 1) * B_DK)
        q = r.dq[rows, hc]
        k = r.dk[rows, hc]
        v = r.dv[rows, hc]
        gam = math.exp(LOG_GAMMA[hh])
        qk = jnp.sum(q * k, axis=-1, keepdims=True)
        vb = v.astype(BF16)
        qg = (q * gam).astype(BF16)
        cross = jnp.zeros((DEC_GROUP, B_DV), F32)
        for j in range(DEC_ROWS):
            sel = rid == first + j
            s = r.s0[j, hh]
            cross = jnp.where(sel, _dot(qg, s.astype(BF16)), cross)
            kj = jnp.where(sel, k, 0.0).astype(BF16)
            r.s_dec[j, hh] = s * gam + _dot_tn(kj, vb)
        r.do[rows, hc] = jnp.where(live, qk * v + cross, r.do[rows, hc])


def _decode_output(r):
    parts = [r.douta[...].astype(BF16)]
    for hh in range(B_HEADS):
        hc = slice(hh * B_DV, (hh + 1) * B_DV)
        o = _center_norm(r.do[:, hc]) * r.gn[:, hc]
        parts.append((o * r.dgate[:, hc]).astype(BF16))
    mix = _dot(jnp.concatenate(parts, axis=-1), r.w_out[...])
    x1 = r.xs[...] + _rms(mix, r.post_mix[...])
    hn = _rms(x1, r.pre_mlp[...]).astype(BF16)
    f = None
    for j in range(D_FF // FF_TILE):
        cols = slice(j * FF_TILE, (j + 1) * FF_TILE)
        u = jnp.maximum(_dot(hn, r.w_up[:, cols]), 0.0)
        part = _dot((u * u).astype(BF16), r.w_down[cols, :])
        f = part if f is None else f + part
    r.ys[...] = x1 + _rms(f, r.post_mlp[...])


def _layer_kernel(*refs, tiles_per_seq, n_tiles):
    r = _Refs(*refs)
    i = pl.program_id(0)

    @pl.when(i == 0)
    def _():
        _convert_weight(r.w_in_hbm, r.w_in)
        _convert_weight(r.w_out_hbm, r.w_out)
        _convert_weight(r.w_up_hbm, r.w_up)
        _convert_weight(r.w_down_hbm, r.w_down)
        _fill_tables(r)
        _decode_projections(r)

    @pl.when(jnp.logical_and(lax.rem(i, tiles_per_seq) == 0, i < n_tiles))
    def _():
        r.s[...] = jnp.zeros_like(r.s)

    @pl.when(i == 0)
    def _():
        _prompt_step(r, tiles_per_seq=tiles_per_seq, with_mixer=True, with_mlp=False)

    @pl.when(jnp.logical_and(i > 0, i < n_tiles))
    def _():
        _prompt_step(r, tiles_per_seq=tiles_per_seq, with_mixer=True, with_mlp=True)

    @pl.when(i == n_tiles)
    def _():
        _prompt_step(r, tiles_per_seq=tiles_per_seq, with_mixer=False, with_mlp=True)
        _decode_output(r)


def _prompt_step(r, *, tiles_per_seq, with_mixer, with_mlp):
    n_chunks = ROW_TILE // CHUNK
    chunk_rows = [slice(c * CHUNK, (c + 1) * CHUNK) for c in range(n_chunks)]
    acc = []

    def mlp_piece(j):
        if not with_mlp:
            return
        cols = slice(j * FF_TILE, (j + 1) * FF_TILE)
        u = jnp.maximum(_dot(r.hn[...], r.w_up[:, cols]), 0.0)
        part = _dot((u * u).astype(BF16), r.w_down[cols, :])
        acc[:] = [part if not acc else acc[0] + part]

    def mlp_finish():
        if with_mlp:
            r.y[...] = r.x1[...] + _rms(acc[0], r.post_mlp[...])

    if not with_mixer:
        for j in range(D_FF // FF_TILE):
            mlp_piece(j)
        mlp_finish()
        return

    x = r.x[...]
    mlp_piece(0)
    h = _rms(x, r.pre_mix[...]).astype(BF16)
    z = _dot(h, r.w_in[...])
    _decode_state_step(r)
    tile_pos = pl.multiple_of(lax.rem(pl.program_id(0), tiles_per_seq) * ROW_TILE, ROW_TILE)
    cosf = r.cos[pl.ds(tile_pos, ROW_TILE), :]
    sinf = r.sin[pl.ds(tile_pos, ROW_TILE), :]

    def head_inputs(hh):
        base = 2 * A_WIDTH + hh * B_DK
        q = _rope(z[:, base:base + B_DK], cosf, sinf)
        k = _rope(z[:, base + B_WIDTH:base + B_WIDTH + B_DK], cosf, sinf) * (B_DK ** -0.5)
        v = z[:, base + 2 * B_WIDTH:base + 2 * B_WIDTH + B_DV].astype(BF16)
        return {
            "q": q.astype(BF16),
            "k": k.astype(BF16),
            "v": v,
            "qd": [(q[c] * r.qdec[hh]).astype(BF16) for c in chunk_rows],
            "kd": [(k[c] * r.kdec[hh]).astype(BF16) for c in chunk_rows],
        }

    def head_dots_a(hd):
        hd["scores"] = [_dot_nt(hd["q"][c], hd["k"][c]) for c in chunk_rows]
        hd["kv"] = [_dot_tn(hd["kd"][ci], hd["v"][c]) for ci, c in enumerate(chunk_rows)]

    def head_chain(hh, hd):
        s = r.s[hh]
        hd["s"] = []
        for ci in range(n_chunks):
            hd["s"].append(s.astype(BF16))
            s = s * math.exp(LOG_GAMMA[hh] * CHUNK) + hd["kv"][ci]
        r.s[hh] = s
        hd["p"] = [(hd["scores"][ci] * r.mask[hh]).astype(BF16) for ci in range(n_chunks)]

    def head_dots_b(hd):
        hd["o"] = [_dot(hd["p"][ci], hd["v"][c]) + _dot(hd["qd"][ci], hd["s"][ci])
                   for ci, c in enumerate(chunk_rows)]

    def head_finish(hh, hd):
        base = 2 * A_WIDTH + 3 * B_WIDTH + hh * B_DV
        gate = _silu(z[:, base:base + B_DV])
        gnw = r.gn[:, hh * B_DV:(hh + 1) * B_DV]
        for ci, c in enumerate(chunk_rows):
            o = _center_norm(hd["o"][ci]) * gnw
            r.mix[c, A_WIDTH + hh * B_DV:A_WIDTH + (hh + 1) * B_DV] = (o * gate[c]).astype(BF16)

    mlp_piece(1)
    ua, va = [], []
    for g in range(A_GROUPS):
        gc = slice(g * A_DIM, (g + 1) * A_DIM)
        ua.append(_gelu(z[:, g * A_DIM:(g + 1) * A_DIM]))
        vg = _gelu(z[:, A_WIDTH + g * A_DIM:A_WIDTH + (g + 1) * A_DIM])
        va.append((_center_norm(vg) * r.lnw[:, gc] + r.lnb[:, gc]).astype(BF16))
        if g == 1:
            mlp_piece(2)
    heads = [head_inputs(0)]

    row = lax.broadcasted_iota(jnp.int32, (CHUNK, CHUNK), 0)
    col = lax.broadcasted_iota(jnp.int32, (CHUNK, CHUNK), 1)
    mixed = []
    for g in range(A_GROUPS):
        w_tril = jnp.where(row >= col, r.ws[g], 0.0).astype(BF16)
        mixed.append([_dot(w_tril, va[g][c]) for c in chunk_rows])
    head_dots_a(heads[0])

    for hh in range(B_HEADS):
        mlp_piece(3 + hh)
        if hh == 0:
            for g in range(A_GROUPS):
                gc = slice(g * A_DIM, (g + 1) * A_DIM)
                bias = jnp.broadcast_to(r.bs[:, g:g + 1], (CHUNK, A_DIM))
                for ci, c in enumerate(chunk_rows):
                    r.mix[c, gc] = (ua[g][c] * (mixed[g][ci] + bias)).astype(BF16)
        else:
            head_finish(hh - 1, heads[hh - 1])
        head_chain(hh, heads[hh])
        if hh + 1 < B_HEADS:
            heads.append(head_inputs(hh + 1))
        head_dots_b(heads[hh])
        if hh + 1 < B_HEADS:
            head_dots_a(heads[hh + 1])

    mlp_piece(7)
    head_finish(B_HEADS - 1, heads[B_HEADS - 1])
    half = ROW_TILE // 2
    mix_a = _dot(r.mix[:half, :], r.w_out[...])
    mlp_finish()
    mix_b = _dot(r.mix[half:, :], r.w_out[...])
    for rows, mix in ((slice(0, half), mix_a), (slice(half, ROW_TILE), mix_b)):
        x1 = x[rows] + _rms(mix, r.post_mix[...])
        r.x1[rows, :] = x1
        r.hn[rows, :] = _rms(x1, r.pre_mlp[...]).astype(BF16)


def _resident(shape):
    zeros = (0,) * len(shape)
    return pl.BlockSpec(shape, lambda i: zeros, pipeline_mode=pl.Buffered(1))


def _layer(x, xs, s0, p, inv_freq):
    batch, seq, _ = x.shape
    n_dec = xs.shape[0]
    tiles_per_seq = seq // ROW_TILE
    n_tiles = batch * tiles_per_seq
    assert n_dec == n_tiles * DEC_ROWS and DEC_GROUP % DEC_ROWS == 0

    def mixer_tile(i):
        return jnp.minimum(i, n_tiles - 1)

    def mlp_tile(i):
        return jnp.maximum(i - 1, 0)

    x_spec = pl.BlockSpec((None, ROW_TILE, D_MODEL),
                          lambda i: (mixer_tile(i) // tiles_per_seq, mixer_tile(i) % tiles_per_seq, 0))
    y_spec = pl.BlockSpec((None, ROW_TILE, D_MODEL),
                          lambda i: (mlp_tile(i) // tiles_per_seq, mlp_tile(i) % tiles_per_seq, 0))
    tok_spec = pl.BlockSpec((n_dec, None, D_MODEL), lambda i: (0, 0, 0), pipeline_mode=pl.Buffered(1))
    ys_spec = pl.BlockSpec((n_dec, None, D_MODEL), lambda i: (0, 0, 0))
    va_spec = pl.BlockSpec((n_dec, None, A_GROUPS, A_DIM), lambda i: (0, 0, 0, 0))
    dec_state_spec = pl.BlockSpec((DEC_ROWS, B_HEADS, B_DK, B_DV), lambda i: (mixer_tile(i), 0, 0, 0))
    hbm = pl.BlockSpec(memory_space=pl.ANY)
    in_specs = [
        x_spec, tok_spec, dec_state_spec,
        _resident((1, D_MODEL)),
        _resident((1, A_WIDTH)), _resident((1, A_WIDTH)),
        _resident((A_GROUPS, CHUNK, CHUNK)), _resident((CHUNK, A_GROUPS)),
        _resident((1, A_WIDTH)), _resident((1, A_WIDTH)),
        _resident((1, B_WIDTH)),
        _resident((1, D_MODEL)), _resident((1, D_MODEL)), _resident((1, D_MODEL)),
        _resident((1, B_DK)),
        hbm, hbm, hbm, hbm,
    ]
    out_specs = [
        y_spec,
        pl.BlockSpec((None, B_HEADS, B_DK, B_DV), lambda i: (mixer_tile(i) // tiles_per_seq, 0, 0, 0)),
        ys_spec, dec_state_spec, va_spec,
    ]
    out_shape = [
        jax.ShapeDtypeStruct(x.shape, F32),
        jax.ShapeDtypeStruct((batch, B_HEADS, B_DK, B_DV), F32),
        jax.ShapeDtypeStruct((n_dec, 1, D_MODEL), F32),
        jax.ShapeDtypeStruct(s0.shape, F32),
        jax.ShapeDtypeStruct((n_dec, 1, A_GROUPS, A_DIM), F32),
    ]
    scratch_shapes = [
        pltpu.VMEM((D_MODEL, IN_COLS), BF16),
        pltpu.VMEM((D_MODEL, D_MODEL), BF16),
        pltpu.VMEM((D_MODEL, D_FF), BF16),
        pltpu.VMEM((D_FF, D_MODEL), BF16),
        pltpu.VMEM((ROW_TILE, D_MODEL), BF16),
        pltpu.VMEM((ROW_TILE, D_MODEL), F32),
        pltpu.VMEM((ROW_TILE, D_MODEL), BF16),
        pltpu.VMEM((seq, B_DK), F32),
        pltpu.VMEM((seq, B_DK), F32),
        pltpu.VMEM((B_HEADS, CHUNK, CHUNK), F32),
        pltpu.VMEM((B_HEADS, CHUNK, B_DK), F32),
        pltpu.VMEM((B_HEADS, CHUNK, B_DK), F32),
        pltpu.VMEM((n_dec, B_WIDTH), F32),
        pltpu.VMEM((n_dec, B_WIDTH), F32),
        pltpu.VMEM((n_dec, B_WIDTH), F32),
        pltpu.VMEM((n_dec, B_WIDTH), F32),
        pltpu.VMEM((n_dec, A_WIDTH), F32),
        pltpu.VMEM((n_dec, B_WIDTH), F32),
    ]
    return pl.pallas_call(
        functools.partial(_layer_kernel, tiles_per_seq=tiles_per_seq, n_tiles=n_tiles),
        grid=(n_tiles + 1,),
        in_specs=in_specs,
        out_specs=out_specs,
        out_shape=out_shape,
        scratch_shapes=scratch_shapes,
        compiler_params=pltpu.CompilerParams(
            dimension_semantics=("arbitrary",),
            vmem_limit_bytes=VMEM_LIMIT_BYTES),
        name="hybrid_layer",
    )(x, xs, s0, p["pre_mix_w"], p["ln_v_w"], p["ln_v_b"], p["w_s"], p["b_s_t"], p["w_s0"], p["b_s0"],
      p["gn_w"], p["post_mix_w"], p["pre_mlp_w"], p["post_mlp_w"], inv_freq,
      p["w_in"], p["w_out"], p["w_up"], p["w_down"])


def kernel(x_prompt, x_sample, state_ret, pre_mix_w, w_in, ln_v_w, ln_v_b, w_s, b_s, gn_w,
           w_out, post_mix_w, pre_mlp_w, w_up, w_down, post_mlp_w):
    depth = w_in.shape[0]
    seq = x_prompt.shape[1]
    assert x_sample.shape[1] == 1 and seq % ROW_TILE == 0

    half = B_DK // 2
    inv = ROPE_BASE ** (-jnp.arange(half, dtype=F32) / half)
    inv_freq = jnp.concatenate([inv, inv])[None, :]

    yp, ys = x_prompt, x_sample
    sp_list, ss_list, vs_list = [], [], []
    for l in range(depth):
        p = {
            "pre_mix_w": pre_mix_w[l][None, :],
            "w_in": w_in[l],
            "ln_v_w": ln_v_w[l].reshape(1, A_WIDTH),
            "ln_v_b": ln_v_b[l].reshape(1, A_WIDTH),
            "w_s": w_s[l],
            "b_s_t": b_s[l].T,
            "w_s0": jnp.repeat(w_s[l][:, 0, 0], A_DIM)[None, :],
            "b_s0": jnp.repeat(b_s[l][:, 0], A_DIM)[None, :],
            "gn_w": gn_w[l].reshape(1, B_WIDTH),
            "w_out": w_out[l],
            "post_mix_w": post_mix_w[l][None, :],
            "pre_mlp_w": pre_mlp_w[l][None, :],
            "w_up": w_up[l],
            "w_down": w_down[l],
            "post_mlp_w": post_mlp_w[l][None, :],
        }
        yp, sp, ys, ss, vs = _layer(yp, ys, state_ret[l], p, inv_freq)
        sp_list.append(sp)
        ss_list.append(ss)
        vs_list.append(vs)
    return (yp, ys, jnp.stack(sp_list, axis=0), jnp.stack(ss_list, axis=0), jnp.stack(vs_list, axis=0))
```

```python
import functools
import math
from typing import Any, NamedTuple

import jax
import jax.numpy as jnp
from jax import lax
from jax.experimental import pallas as pl
from jax.experimental.pallas import tpu as pltpu

D_MODEL = 1024
A_WIDTH = 512
B_WIDTH = 512
A_GROUPS = 4
A_DIM = 128
CHUNK = 128
B_HEADS = 4
B_DK = 128
B_DV = 128
D_FF = 4096
IN_COLS = 2 * A_WIDTH + 4 * B_WIDTH
ROPE_BASE = 10000.0
EPS = 1e-6
PAST_LEN = 16384
LOG_GAMMA = [math.log(1.0 - 2.0 ** (-5.0 - h)) for h in range(B_HEADS)]

ROW_TILE = 512
FF_TILE = 512
DEC_ROWS = 4
DEC_GROUP = 16
STAGE_BYTES = 1 << 20
VMEM_LIMIT_BYTES = 60 * 1024 * 1024

F32 = jnp.float32
BF16 = jnp.bfloat16


def _rms(x, w):
    return x * lax.rsqrt(jnp.mean(x * x, axis=-1, keepdims=True) + EPS) * w


def _center_norm(x):
    mu = jnp.mean(x, axis=-1, keepdims=True)
    xc = x - mu
    return xc * lax.rsqrt(jnp.mean(xc * xc, axis=-1, keepdims=True) + EPS)


def _gelu(x):
    return jax.nn.gelu(x, approximate=True)


def _silu(x):
    return x * (1.0 / (1.0 + jnp.exp(-x)))


def _rope(x, cosf, sinf):
    return x * cosf + pltpu.roll(x, B_DK // 2, axis=1) * sinf


def _rope_rows(pos, inv):
    ang = pos * inv
    lane = lax.broadcasted_iota(jnp.int32, ang.shape, 1)
    sin = jnp.sin(ang)
    return jnp.cos(ang), jnp.where(lane < B_DK // 2, -sin, sin)


def _dot(a, b):
    return jnp.dot(a, b, preferred_element_type=F32)


def _dot_nt(a, b):
    return lax.dot_general(a, b, (((1,), (1,)), ((), ())), preferred_element_type=F32)


def _dot_tn(a, b):
    return lax.dot_general(a, b, (((0,), (0,)), ((), ())), preferred_element_type=F32)


class _Refs(NamedTuple):
    x: Any
    xs: Any
    s0: Any
    pre_mix: Any
    lnw: Any
    lnb: Any
    ws: Any
    bs: Any
    ws0: Any
    bs0: Any
    gn: Any
    post_mix: Any
    pre_mlp: Any
    post_mlp: Any
    inv: Any
    w_in_hbm: Any
    w_out_hbm: Any
    w_up_hbm: Any
    w_down_hbm: Any
    y: Any
    s: Any
    ys: Any
    s_dec: Any
    va: Any
    w_in: Any
    w_out: Any
    w_up: Any
    w_down: Any
    mix: Any
    x1: Any
    hn: Any
    cos: Any
    sin: Any
    mask: Any
    qdec: Any
    kdec: Any
    dq: Any
    dk: Any
    dv: Any
    dgate: Any
    douta: Any
    do: Any


def _convert_weight(w_hbm, w_bf, col_pieces=1):
    n_rows, n_cols = w_hbm.shape
    piece = n_cols // col_pieces
    rows_per = 1 << int(math.log2(STAGE_BYTES // (4 * n_cols)))
    assert n_rows % rows_per == 0 and rows_per % 16 == 0
    n_chunks = n_rows // rows_per

    def body(stage, sem):
        def copy(c, slot):
            return pltpu.make_async_copy(
                w_hbm.at[pl.ds(c * rows_per, rows_per), :], stage.at[slot], sem.at[slot])

        copy(0, 0).start()

        def step(c, carry):
            slot = lax.rem(c, 2)

            @pl.when(c + 1 < n_chunks)
            def _():
                copy(c + 1, 1 - slot).start()

            copy(c, slot).wait()
            rows = pl.ds(pl.multiple_of(c * rows_per, rows_per), rows_per)
            if col_pieces == 1:
                w_bf[rows, :] = stage[slot].astype(BF16)
            else:
                for p in range(col_pieces):
                    w_bf[p, rows, :] = stage[slot, :, p * piece:(p + 1) * piece].astype(BF16)
            return carry

        lax.fori_loop(0, n_chunks, step, 0)

    pl.run_scoped(body, pltpu.VMEM((2, rows_per, n_cols), F32), pltpu.SemaphoreType.DMA((2,)))


def _fill_tables(r):
    seq = r.cos.shape[0]

    def rope_block(b, carry):
        r0 = pl.multiple_of(b * ROW_TILE, ROW_TILE)
        pos = (lax.broadcasted_iota(jnp.int32, (ROW_TILE, B_DK), 0) + r0).astype(F32)
        cosf, sinf = _rope_rows(pos, r.inv[...])
        r.cos[pl.ds(r0, ROW_TILE), :] = cosf
        r.sin[pl.ds(r0, ROW_TILE), :] = sinf
        return carry

    lax.fori_loop(0, seq // ROW_TILE, rope_block, 0)
    row = lax.broadcasted_iota(jnp.int32, (CHUNK, CHUNK), 0).astype(F32)
    col = lax.broadcasted_iota(jnp.int32, (CHUNK, CHUNK), 1).astype(F32)
    diff = row - col
    for hh in range(B_HEADS):
        lg = LOG_GAMMA[hh]
        r.mask[hh] = jnp.where(diff >= 0, jnp.exp(lg * jnp.maximum(diff, 0.0)), 0.0)
        r.qdec[hh] = jnp.exp(lg * (row + 1.0))
        r.kdec[hh] = jnp.exp(lg * (CHUNK - 1.0 - row))


def _decode_projections(r):
    h = _rms(r.xs[...], r.pre_mix[...]).astype(BF16)
    z = _dot(h, r.w_in[...])
    cosf, sinf = _rope_rows(jnp.full((1, B_DK), PAST_LEN, F32), r.inv[...])
    for g in range(A_GROUPS):
        gc = slice(g * A_DIM, (g + 1) * A_DIM)
        ua = _gelu(z[:, g * A_DIM:(g + 1) * A_DIM])
        va = _gelu(z[:, A_WIDTH + g * A_DIM:A_WIDTH + (g + 1) * A_DIM])
        va = _center_norm(va) * r.lnw[:, gc] + r.lnb[:, gc]
        r.va[:, g, :] = va
        r.douta[:, gc] = ua * (r.ws0[:, gc] * va + r.bs0[:, gc])
    base = 2 * A_WIDTH
    for hh in range(B_HEADS):
        hc = slice(hh * B_DK, (hh + 1) * B_DK)
        r.dq[:, hc] = _rope(z[:, base + hh * B_DK:base + (hh + 1) * B_DK], cosf, sinf)
        r.dk[:, hc] = _rope(z[:, base + B_WIDTH + hh * B_DK:base + B_WIDTH + (hh + 1) * B_DK],
                            cosf, sinf) * (B_DK ** -0.5)
    r.dv[...] = z[:, base + 2 * B_WIDTH:base + 3 * B_WIDTH]
    r.dgate[...] = _silu(z[:, base + 3 * B_WIDTH:])
    r.do[...] = jnp.zeros_like(r.do)


def _decode_state_step(r):
    i = pl.program_id(0)
    per_group = DEC_GROUP // DEC_ROWS
    rows = pl.ds(pl.multiple_of((i // per_group) * DEC_GROUP, DEC_GROUP), DEC_GROUP)
    first = lax.rem(i, per_group) * DEC_ROWS
    rid = lax.broadcasted_iota(jnp.int32, (DEC_GROUP, B_DK), 0)
    live = jnp.logical_and(rid >= first, rid < first + DEC_ROWS)
    for hh in range(B_HEADS):
        hc = slice(hh * B_DK, (hh + 1) * B_DK)
        q = r.dq[rows, hc]
        k = r.dk[rows, hc]
        v = r.dv[rows, hc]
        gam = math.exp(LOG_GAMMA[hh])
        qk = jnp.sum(q * k, axis=-1, keepdims=True)
        vb = v.astype(BF16)
        qg = (q * gam).astype(BF16)
        cross = jnp.zeros((DEC_GROUP, B_DV), F32)
        for j in range(DEC_ROWS):
            sel = rid == first + j
            s = r.s0[j, hh]
            cross = jnp.where(sel, _dot(qg, s.astype(BF16)), cross)
            kj = jnp.where(sel, k, 0.0).astype(BF16)
            r.s_dec[j, hh] = s * gam + _dot_tn(kj, vb)
        r.do[rows, hc] = jnp.where(live, qk * v + cross, r.do[rows, hc])


def _decode_output(r):
    parts = [r.douta[...].astype(BF16)]
    for hh in range(B_HEADS):
        hc = slice(hh * B_DV, (hh + 1) * B_DV)
        o = _center_norm(r.do[:, hc]) * r.gn[:, hc]
        parts.append((o * r.dgate[:, hc]).astype(BF16))
    mix = _dot(jnp.concatenate(parts, axis=-1), r.w_out[...])
    x1 = r.xs[...] + _rms(mix, r.post_mix[...])
    hn = _rms(x1, r.pre_mlp[...]).astype(BF16)

    def mlp_piece(j, f):
        u = jnp.maximum(_dot(hn, r.w_up[j]), 0.0)
        rows = pl.ds(pl.multiple_of(j * FF_TILE, FF_TILE), FF_TILE)
        return f + _dot((u * u).astype(BF16), r.w_down[rows, :])

    f = lax.fori_loop(0, D_FF // FF_TILE, mlp_piece, jnp.zeros_like(x1))
    r.ys[...] = x1 + _rms(f, r.post_mlp[...])


def _layer_kernel(*refs, tiles_per_seq, n_tiles):
    r = _Refs(*refs)
    i = pl.program_id(0)

    @pl.when(i == 0)
    def _():
        _convert_weight(r.w_in_hbm, r.w_in)
        _convert_weight(r.w_out_hbm, r.w_out)
        _convert_weight(r.w_up_hbm, r.w_up, col_pieces=D_FF // FF_TILE)
        _convert_weight(r.w_down_hbm, r.w_down)
        _fill_tables(r)
        _decode_projections(r)
        r.x1[...] = jnp.zeros_like(r.x1)
        r.hn[...] = jnp.zeros_like(r.hn)

    @pl.when(jnp.logical_and(lax.rem(i, tiles_per_seq) == 0, i < n_tiles))
    def _():
        r.s[...] = jnp.zeros_like(r.s)

    @pl.when(i < n_tiles)
    def _():
        _prompt_step(r, tiles_per_seq=tiles_per_seq, with_mixer=True)

    @pl.when(i == n_tiles)
    def _():
        _prompt_step(r, tiles_per_seq=tiles_per_seq, with_mixer=False)
        _decode_output(r)


def _prompt_step(r, *, tiles_per_seq, with_mixer):
    n_chunks = ROW_TILE // CHUNK
    chunk_rows = [slice(c * CHUNK, (c + 1) * CHUNK) for c in range(n_chunks)]
    acc = []

    def mlp_piece(j):
        u = jnp.maximum(_dot(r.hn[...], r.w_up[j]), 0.0)
        part = _dot((u * u).astype(BF16), r.w_down[j * FF_TILE:(j + 1) * FF_TILE, :])
        acc[:] = [part if not acc else acc[0] + part]

    def mlp_finish():
        r.y[...] = r.x1[...] + _rms(acc[0], r.post_mlp[...])

    if not with_mixer:
        for j in range(D_FF // FF_TILE):
            mlp_piece(j)
        mlp_finish()
        return

    x = r.x[...]
    mlp_piece(0)
    h = _rms(x, r.pre_mix[...]).astype(BF16)
    z = _dot(h, r.w_in[...])
    _decode_state_step(r)
    tile_pos = pl.multiple_of(lax.rem(pl.program_id(0), tiles_per_seq) * ROW_TILE, ROW_TILE)
    cosf = r.cos[pl.ds(tile_pos, ROW_TILE), :]
    sinf = r.sin[pl.ds(tile_pos, ROW_TILE), :]

    def head_inputs(hh):
        base = 2 * A_WIDTH + hh * B_DK
        q = _rope(z[:, base:base + B_DK], cosf, sinf)
        k = _rope(z[:, base + B_WIDTH:base + B_WIDTH + B_DK], cosf, sinf) * (B_DK ** -0.5)
        v = z[:, base + 2 * B_WIDTH:base + 2 * B_WIDTH + B_DV].astype(BF16)
        return {
            "q": q.astype(BF16),
            "k": k.astype(BF16),
            "v": v,
            "qd": [(q[c] * r.qdec[hh]).astype(BF16) for c in chunk_rows],
            "kd": [(k[c] * r.kdec[hh]).astype(BF16) for c in chunk_rows],
        }

    def head_dots_a(hd):
        hd["scores"] = [_dot_nt(hd["q"][c], hd["k"][c]) for c in chunk_rows]
        hd["kv"] = [_dot_tn(hd["kd"][ci], hd["v"][c]) for ci, c in enumerate(chunk_rows)]

    def head_chain(hh, hd):
        s = r.s[hh]
        hd["s"] = []
        for ci in range(n_chunks):
            hd["s"].append(s.astype(BF16))
            s = s * math.exp(LOG_GAMMA[hh] * CHUNK) + hd["kv"][ci]
        r.s[hh] = s
        hd["p"] = [(hd["scores"][ci] * r.mask[hh]).astype(BF16) for ci in range(n_chunks)]

    def head_dots_b(hd):
        hd["o"] = [_dot(hd["p"][ci], hd["v"][c]) + _dot(hd["qd"][ci], hd["s"][ci])
                   for ci, c in enumerate(chunk_rows)]

    def head_finish(hh, hd):
        base = 2 * A_WIDTH + 3 * B_WIDTH + hh * B_DV
        gate = _silu(z[:, base:base + B_DV])
        gnw = r.gn[:, hh * B_DV:(hh + 1) * B_DV]
        for ci, c in enumerate(chunk_rows):
            o = _center_norm(hd["o"][ci]) * gnw
            r.mix[c, A_WIDTH + hh * B_DV:A_WIDTH + (hh + 1) * B_DV] = (o * gate[c]).astype(BF16)

    mlp_piece(1)
    ua, va = [], []
    for g in range(A_GROUPS):
        gc = slice(g * A_DIM, (g + 1) * A_DIM)
        ua.append(_gelu(z[:, g * A_DIM:(g + 1) * A_DIM]))
        vg = _gelu(z[:, A_WIDTH + g * A_DIM:A_WIDTH + (g + 1) * A_DIM])
        va.append((_center_norm(vg) * r.lnw[:, gc] + r.lnb[:, gc]).astype(BF16))
        if g == 1:
            mlp_piece(2)
    heads = [head_inputs(0)]

    row = lax.broadcasted_iota(jnp.int32, (CHUNK, CHUNK), 0)
    col = lax.broadcasted_iota(jnp.int32, (CHUNK, CHUNK), 1)
    mixed = []
    for g in range(A_GROUPS):
        w_tril = jnp.where(row >= col, r.ws[g], 0.0).astype(BF16)
        mixed.append([_dot(w_tril, va[g][c]) for c in chunk_rows])
    head_dots_a(heads[0])

    for hh in range(B_HEADS):
        mlp_piece(3 + hh)
        if hh == 0:
            for g in range(A_GROUPS):
                gc = slice(g * A_DIM, (g + 1) * A_DIM)
                bias = jnp.broadcast_to(r.bs[:, g:g + 1], (CHUNK, A_DIM))
                for ci, c in enumerate(chunk_rows):
                    r.mix[c, gc] = (ua[g][c] * (mixed[g][ci] + bias)).astype(BF16)
        else:
            head_finish(hh - 1, heads[hh - 1])
        head_chain(hh, heads[hh])
        if hh + 1 < B_HEADS:
            heads.append(head_inputs(hh + 1))
        head_dots_b(heads[hh])
        if hh + 1 < B_HEADS:
            head_dots_a(heads[hh + 1])

    mlp_piece(7)
    head_finish(B_HEADS - 1, heads[B_HEADS - 1])
    half = ROW_TILE // 2
    mix_a = _dot(r.mix[:half, :], r.w_out[...])
    mlp_finish()
    mix_b = _dot(r.mix[half:, :], r.w_out[...])
    for rows, mix in ((slice(0, half), mix_a), (slice(half, ROW_TILE), mix_b)):
        x1 = x[rows] + _rms(mix, r.post_mix[...])
        r.x1[rows, :] = x1
        r.hn[rows, :] = _rms(x1, r.pre_mlp[...]).astype(BF16)


def _resident(shape):
    zeros = (0,) * len(shape)
    return pl.BlockSpec(shape, lambda i: zeros, pipeline_mode=pl.Buffered(1))


def _layer(x, xs, s0, p, inv_freq):
    batch, seq, _ = x.shape
    n_dec = xs.shape[0]
    tiles_per_seq = seq // ROW_TILE
    n_tiles = batch * tiles_per_seq
    assert n_dec == n_tiles * DEC_ROWS and DEC_GROUP % DEC_ROWS == 0

    def mixer_tile(i):
        return jnp.minimum(i, n_tiles - 1)

    def mlp_tile(i):
        return jnp.maximum(i - 1, 0)

    x_spec = pl.BlockSpec((None, ROW_TILE, D_MODEL),
                          lambda i: (mixer_tile(i) // tiles_per_seq, mixer_tile(i) % tiles_per_seq, 0))
    y_spec = pl.BlockSpec((None, ROW_TILE, D_MODEL),
                          lambda i: (mlp_tile(i) // tiles_per_seq, mlp_tile(i) % tiles_per_seq, 0))
    tok_spec = pl.BlockSpec((n_dec, None, D_MODEL), lambda i: (0, 0, 0), pipeline_mode=pl.Buffered(1))
    ys_spec = pl.BlockSpec((n_dec, None, D_MODEL), lambda i: (0, 0, 0))
    va_spec = pl.BlockSpec((n_dec, None, A_GROUPS, A_DIM), lambda i: (0, 0, 0, 0))
    dec_state_spec = pl.BlockSpec((DEC_ROWS, B_HEADS, B_DK, B_DV), lambda i: (mixer_tile(i), 0, 0, 0))
    hbm = pl.BlockSpec(memory_space=pl.ANY)
    in_specs = [
        x_spec, tok_spec, dec_state_spec,
        _resident((1, D_MODEL)),
        _resident((1, A_WIDTH)), _resident((1, A_WIDTH)),
        _resident((A_GROUPS, CHUNK, CHUNK)), _resident((CHUNK, A_GROUPS)),
        _resident((1, A_WIDTH)), _resident((1, A_WIDTH)),
        _resident((1, B_WIDTH)),
        _resident((1, D_MODEL)), _resident((1, D_MODEL)), _resident((1, D_MODEL)),
        _resident((1, B_DK)),
        hbm, hbm, hbm, hbm,
    ]
    out_specs = [
        y_spec,
        pl.BlockSpec((None, B_HEADS, B_DK, B_DV), lambda i: (mixer_tile(i) // tiles_per_seq, 0, 0, 0)),
        ys_spec, dec_state_spec, va_spec,
    ]
    out_shape = [
        jax.ShapeDtypeStruct(x.shape, F32),
        jax.ShapeDtypeStruct((batch, B_HEADS, B_DK, B_DV), F32),
        jax.ShapeDtypeStruct((n_dec, 1, D_MODEL), F32),
        jax.ShapeDtypeStruct(s0.shape, F32),
        jax.ShapeDtypeStruct((n_dec, 1, A_GROUPS, A_DIM), F32),
    ]
    scratch_shapes = [
        pltpu.VMEM((D_MODEL, IN_COLS), BF16),
        pltpu.VMEM((D_MODEL, D_MODEL), BF16),
        pltpu.VMEM((D_FF // FF_TILE, D_MODEL, FF_TILE), BF16),
        pltpu.VMEM((D_FF, D_MODEL), BF16),
        pltpu.VMEM((ROW_TILE, D_MODEL), BF16),
        pltpu.VMEM((ROW_TILE, D_MODEL), F32),
        pltpu.VMEM((ROW_TILE, D_MODEL), BF16),
        pltpu.VMEM((seq, B_DK), F32),
        pltpu.VMEM((seq, B_DK), F32),
        pltpu.VMEM((B_HEADS, CHUNK, CHUNK), F32),
        pltpu.VMEM((B_HEADS, CHUNK, B_DK), F32),
        pltpu.VMEM((B_HEADS, CHUNK, B_DK), F32),
        pltpu.VMEM((n_dec, B_WIDTH), F32),
        pltpu.VMEM((n_dec, B_WIDTH), F32),
        pltpu.VMEM((n_dec, B_WIDTH), F32),
        pltpu.VMEM((n_dec, B_WIDTH), F32),
        pltpu.VMEM((n_dec, A_WIDTH), F32),
        pltpu.VMEM((n_dec, B_WIDTH), F32),
    ]
    return pl.pallas_call(
        functools.partial(_layer_kernel, tiles_per_seq=tiles_per_seq, n_tiles=n_tiles),
        grid=(n_tiles + 1,),
        in_specs=in_specs,
        out_specs=out_specs,
        out_shape=out_shape,
        scratch_shapes=scratch_shapes,
        compiler_params=pltpu.CompilerParams(
            dimension_semantics=("arbitrary",),
            vmem_limit_bytes=VMEM_LIMIT_BYTES),
        name="hybrid_layer",
    )(x, xs, s0, p["pre_mix_w"], p["ln_v_w"], p["ln_v_b"], p["w_s"], p["b_s_t"], p["w_s0"], p["b_s0"],
      p["gn_w"], p["post_mix_w"], p["pre_mlp_w"], p["post_mlp_w"], inv_freq,
      p["w_in"], p["w_out"], p["w_up"], p["w_down"])


def kernel(x_prompt, x_sample, state_ret, pre_mix_w, w_in, ln_v_w, ln_v_b, w_s, b_s, gn_w,
           w_out, post_mix_w, pre_mlp_w, w_up, w_down, post_mlp_w):
    depth = w_in.shape[0]
    seq = x_prompt.shape[1]
    assert x_sample.shape[1] == 1 and seq % ROW_TILE == 0

    half = B_DK // 2
    inv = ROPE_BASE ** (-jnp.arange(half, dtype=F32) / half)
    inv_freq = jnp.concatenate([inv, inv])[None, :]

    yp, ys = x_prompt, x_sample
    sp_list, ss_list, vs_list = [], [], []
    for l in range(depth):
        p = {
            "pre_mix_w": pre_mix_w[l][None, :],
            "w_in": w_in[l],
            "ln_v_w": ln_v_w[l].reshape(1, A_WIDTH),
            "ln_v_b": ln_v_b[l].reshape(1, A_WIDTH),
            "w_s": w_s[l],
            "b_s_t": b_s[l].T,
            "w_s0": jnp.repeat(w_s[l][:, 0, 0], A_DIM)[None, :],
            "b_s0": jnp.repeat(b_s[l][:, 0], A_DIM)[None, :],
            "gn_w": gn_w[l].reshape(1, B_WIDTH),
            "w_out": w_out[l],
            "post_mix_w": post_mix_w[l][None, :],
            "pre_mlp_w": pre_mlp_w[l][None, :],
            "w_up": w_up[l],
            "w_down": w_down[l],
            "post_mlp_w": post_mlp_w[l][None, :],
        }
        yp, sp, ys, ss, vs = _layer(yp, ys, state_ret[l], p, inv_freq)
        sp_list.append(sp)
        ss_list.append(ss)
        vs_list.append(vs)
    return (yp, ys, jnp.stack(sp_list, axis=0), jnp.stack(ss_list, axis=0), jnp.stack(vs_list, axis=0))
```

```python
import functools
import math
from typing import Any, NamedTuple

import jax
import jax.numpy as jnp
from jax import lax
from jax.experimental import pallas as pl
from jax.experimental.pallas import tpu as pltpu

D_MODEL = 1024
A_WIDTH = 512
B_WIDTH = 512
A_GROUPS = 4
A_DIM = 128
CHUNK = 128
B_HEADS = 4
B_DK = 128
B_DV = 128
D_FF = 4096
IN_COLS = 2 * A_WIDTH + 4 * B_WIDTH
ROPE_BASE = 10000.0
EPS = 1e-6
PAST_LEN = 16384
LOG_GAMMA = [math.log(1.0 - 2.0 ** (-5.0 - h)) for h in range(B_HEADS)]

ROW_TILE = 512
FF_TILE = 512
DEC_ROWS = 4
DEC_GROUP = 16
STAGE_BYTES = 1 << 20
VMEM_LIMIT_BYTES = 60 * 1024 * 1024

F32 = jnp.float32
BF16 = jnp.bfloat16


def _rms(x, w):
    return x * lax.rsqrt(jnp.mean(x * x, axis=-1, keepdims=True) + EPS) * w


def _center_norm(x):
    mu = jnp.mean(x, axis=-1, keepdims=True)
    xc = x - mu
    return xc * lax.rsqrt(jnp.mean(xc * xc, axis=-1, keepdims=True) + EPS)


def _gelu(x):
    return jax.nn.gelu(x, approximate=True)


def _silu(x):
    return x * (1.0 / (1.0 + jnp.exp(-x)))


def _rope(x, cosf, sinf):
    return x * cosf + pltpu.roll(x, B_DK // 2, axis=1) * sinf


def _rope_rows(pos, inv):
    ang = pos * inv
    lane = lax.broadcasted_iota(jnp.int32, ang.shape, 1)
    sin = jnp.sin(ang)
    return jnp.cos(ang), jnp.where(lane < B_DK // 2, -sin, sin)


def _dot(a, b):
    return jnp.dot(a, b, preferred_element_type=F32)


def _dot_nt(a, b):
    return lax.dot_general(a, b, (((1,), (1,)), ((), ())), preferred_element_type=F32)


def _dot_tn(a, b):
    return lax.dot_general(a, b, (((0,), (0,)), ((), ())), preferred_element_type=F32)


class _Refs(NamedTuple):
    x: Any
    xs: Any
    s0: Any
    pre_mix: Any
    lnw: Any
    lnb: Any
    ws: Any
    bs: Any
    ws0: Any
    bs0: Any
    gn: Any
    post_mix: Any
    pre_mlp: Any
    post_mlp: Any
    inv: Any
    w_in_hbm: Any
    w_out_hbm: Any
    w_up_hbm: Any
    w_down_hbm: Any
    y: Any
    s: Any
    ys: Any
    s_dec: Any
    va: Any
    w_in: Any
    w_out: Any
    w_up: Any
    w_down: Any
    mix: Any
    x1: Any
    hn: Any
    cos: Any
    sin: Any
    mask: Any
    qdec: Any
    kdec: Any
    dq: Any
    dk: Any
    dv: Any
    dgate: Any
    douta: Any
    do: Any


def _convert_weight(w_hbm, w_bf, col_pieces=1):
    n_rows, n_cols = w_hbm.shape
    piece = n_cols // col_pieces
    rows_per = 1 << int(math.log2(STAGE_BYTES // (4 * n_cols)))
    assert n_rows % rows_per == 0 and rows_per % 16 == 0
    n_chunks = n_rows // rows_per

    def body(stage, sem):
        def copy(c, slot):
            return pltpu.make_async_copy(
                w_hbm.at[pl.ds(c * rows_per, rows_per), :], stage.at[slot], sem.at[slot])

        copy(0, 0).start()

        def step(c, carry):
            slot = lax.rem(c, 2)

            @pl.when(c + 1 < n_chunks)
            def _():
                copy(c + 1, 1 - slot).start()

            copy(c, slot).wait()
            rows = pl.ds(pl.multiple_of(c * rows_per, rows_per), rows_per)
            if col_pieces == 1:
                w_bf[rows, :] = stage[slot].astype(BF16)
            else:
                for p in range(col_pieces):
                    w_bf[p, rows, :] = stage[slot, :, p * piece:(p + 1) * piece].astype(BF16)
            return carry

        lax.fori_loop(0, n_chunks, step, 0)

    pl.run_scoped(body, pltpu.VMEM((2, rows_per, n_cols), F32), pltpu.SemaphoreType.DMA((2,)))


def _fill_tables(r):
    seq = r.cos.shape[0]

    def rope_block(b, carry):
        r0 = pl.multiple_of(b * ROW_TILE, ROW_TILE)
        pos = (lax.broadcasted_iota(jnp.int32, (ROW_TILE, B_DK), 0) + r0).astype(F32)
        cosf, sinf = _rope_rows(pos, r.inv[...])
        r.cos[pl.ds(r0, ROW_TILE), :] = cosf
        r.sin[pl.ds(r0, ROW_TILE), :] = sinf
        return carry

    lax.fori_loop(0, seq // ROW_TILE, rope_block, 0)
    row = lax.broadcasted_iota(jnp.int32, (CHUNK, CHUNK), 0).astype(F32)
    col = lax.broadcasted_iota(jnp.int32, (CHUNK, CHUNK), 1).astype(F32)
    diff = row - col
    for hh in range(B_HEADS):
        lg = LOG_GAMMA[hh]
        r.mask[hh] = jnp.where(diff >= 0, jnp.exp(lg * jnp.maximum(diff, 0.0)), 0.0)
        r.qdec[hh] = jnp.exp(lg * (row + 1.0))
        r.kdec[hh] = jnp.exp(lg * (CHUNK - 1.0 - row))


def _decode_projections(r):
    h = _rms(r.xs[...], r.pre_mix[...]).astype(BF16)
    z = _dot(h, r.w_in[...])
    cosf, sinf = _rope_rows(jnp.full((1, B_DK), PAST_LEN, F32), r.inv[...])
    for g in range(A_GROUPS):
        gc = slice(g * A_DIM, (g + 1) * A_DIM)
        ua = _gelu(z[:, g * A_DIM:(g + 1) * A_DIM])
        va = _gelu(z[:, A_WIDTH + g * A_DIM:A_WIDTH + (g + 1) * A_DIM])
        va = _center_norm(va) * r.lnw[:, gc] + r.lnb[:, gc]
        r.va[:, g, :] = va
        r.douta[:, gc] = ua * (r.ws0[:, gc] * va + r.bs0[:, gc])
    base = 2 * A_WIDTH
    for hh in range(B_HEADS):
        hc = slice(hh * B_DK, (hh + 1) * B_DK)
        r.dq[:, hc] = _rope(z[:, base + hh * B_DK:base + (hh + 1) * B_DK], cosf, sinf)
        r.dk[:, hc] = _rope(z[:, base + B_WIDTH + hh * B_DK:base + B_WIDTH + (hh + 1) * B_DK],
                            cosf, sinf) * (B_DK ** -0.5)
    r.dv[...] = z[:, base + 2 * B_WIDTH:base + 3 * B_WIDTH]
    r.dgate[...] = _silu(z[:, base + 3 * B_WIDTH:])
    r.do[...] = jnp.zeros_like(r.do)


def _decode_state_step(r):
    i = pl.program_id(0)
    per_group = DEC_GROUP // DEC_ROWS
    rows = pl.ds(pl.multiple_of((i // per_group) * DEC_GROUP, DEC_GROUP), DEC_GROUP)
    first = lax.rem(i, per_group) * DEC_ROWS
    rid = lax.broadcasted_iota(jnp.int32, (DEC_GROUP, B_DK), 0)
    live = jnp.logical_and(rid >= first, rid < first + DEC_ROWS)
    for hh in range(B_HEADS):
        hc = slice(hh * B_DK, (hh + 1) * B_DK)
        q = r.dq[rows, hc]
        k = r.dk[rows, hc]
        v = r.dv[rows, hc]
        gam = math.exp(LOG_GAMMA[hh])
        qk = jnp.sum(q * k, axis=-1, keepdims=True)
        vb = v.astype(BF16)
        qg = (q * gam).astype(BF16)
        cross = jnp.zeros((DEC_GROUP, B_DV), F32)
        for j in range(DEC_ROWS):
            sel = rid == first + j
            s = r.s0[j, hh]
            cross = jnp.where(sel, _dot(qg, s.astype(BF16)), cross)
            kj = jnp.where(sel, k, 0.0).astype(BF16)
            r.s_dec[j, hh] = s * gam + _dot_tn(kj, vb)
        r.do[rows, hc] = jnp.where(live, qk * v + cross, r.do[rows, hc])


def _decode_output(r):
    parts = [r.douta[...].astype(BF16)]
    for hh in range(B_HEADS):
        hc = slice(hh * B_DV, (hh + 1) * B_DV)
        o = _center_norm(r.do[:, hc]) * r.gn[:, hc]
        parts.append((o * r.dgate[:, hc]).astype(BF16))
    mix = _dot(jnp.concatenate(parts, axis=-1), r.w_out[...])
    x1 = r.xs[...] + _rms(mix, r.post_mix[...])
    hn = _rms(x1, r.pre_mlp[...]).astype(BF16)

    def mlp_piece(j, f):
        u = jnp.maximum(_dot(hn, r.w_up[j]), 0.0)
        rows = pl.ds(pl.multiple_of(j * FF_TILE, FF_TILE), FF_TILE)
        return f + _dot((u * u).astype(BF16), r.w_down[rows, :])

    f = lax.fori_loop(0, D_FF // FF_TILE, mlp_piece, jnp.zeros_like(x1))
    r.ys[...] = x1 + _rms(f, r.post_mlp[...])


def _layer_kernel(*refs, tiles_per_seq, n_tiles):
    r = _Refs(*refs)
    i = pl.program_id(0)

    @pl.when(i == 0)
    def _():
        _convert_weight(r.w_in_hbm, r.w_in)
        _convert_weight(r.w_out_hbm, r.w_out)
        _convert_weight(r.w_up_hbm, r.w_up, col_pieces=D_FF // FF_TILE)
        _convert_weight(r.w_down_hbm, r.w_down)
        _fill_tables(r)
        _decode_projections(r)
        r.x1[...] = jnp.zeros_like(r.x1)
        r.hn[...] = jnp.zeros_like(r.hn)

    @pl.when(jnp.logical_and(lax.rem(i, tiles_per_seq) == 0, i < n_tiles))
    def _():
        r.s[...] = jnp.zeros_like(r.s)

    @pl.when(i < n_tiles)
    def _():
        _prompt_step(r, tiles_per_seq=tiles_per_seq, with_mixer=True)

    @pl.when(i == n_tiles)
    def _():
        _prompt_step(r, tiles_per_seq=tiles_per_seq, with_mixer=False)
        _decode_output(r)


def _prompt_step(r, *, tiles_per_seq, with_mixer):
    n_chunks = ROW_TILE // CHUNK
    chunk_rows = [slice(c * CHUNK, (c + 1) * CHUNK) for c in range(n_chunks)]
    acc = []

    def mlp_piece(j):
        u = jnp.maximum(_dot(r.hn[...], r.w_up[j]), 0.0)
        part = _dot((u * u).astype(BF16), r.w_down[j * FF_TILE:(j + 1) * FF_TILE, :])
        acc[:] = [part if not acc else acc[0] + part]

    def mlp_finish():
        r.y[...] = r.x1[...] + _rms(acc[0], r.post_mlp[...])

    if not with_mixer:
        def piece(j, f):
            u = jnp.maximum(_dot(r.hn[...], r.w_up[j]), 0.0)
            rows = pl.ds(pl.multiple_of(j * FF_TILE, FF_TILE), FF_TILE)
            return f + _dot((u * u).astype(BF16), r.w_down[rows, :])

        acc.append(lax.fori_loop(0, D_FF // FF_TILE, piece, jnp.zeros((ROW_TILE, D_MODEL), F32)))
        mlp_finish()
        return

    x = r.x[...]
    mlp_piece(0)
    h = _rms(x, r.pre_mix[...]).astype(BF16)
    z = _dot(h, r.w_in[...])
    _decode_state_step(r)
    tile_pos = pl.multiple_of(lax.rem(pl.program_id(0), tiles_per_seq) * ROW_TILE, ROW_TILE)
    cosf = r.cos[pl.ds(tile_pos, ROW_TILE), :]
    sinf = r.sin[pl.ds(tile_pos, ROW_TILE), :]

    def head_inputs(hh):
        base = 2 * A_WIDTH + hh * B_DK
        q = _rope(z[:, base:base + B_DK], cosf, sinf)
        k = _rope(z[:, base + B_WIDTH:base + B_WIDTH + B_DK], cosf, sinf) * (B_DK ** -0.5)
        v = z[:, base + 2 * B_WIDTH:base + 2 * B_WIDTH + B_DV].astype(BF16)
        return {
            "q": q.astype(BF16),
            "k": k.astype(BF16),
            "v": v,
            "qd": [(q[c] * r.qdec[hh]).astype(BF16) for c in chunk_rows],
            "kd": [(k[c] * r.kdec[hh]).astype(BF16) for c in chunk_rows],
        }

    def head_dots_a(hd):
        hd["scores"] = [_dot_nt(hd["q"][c], hd["k"][c]) for c in chunk_rows]
        hd["kv"] = [_dot_tn(hd["kd"][ci], hd["v"][c]) for ci, c in enumerate(chunk_rows)]

    def head_chain(hh, hd):
        s = r.s[hh]
        hd["s"] = []
        for ci in range(n_chunks):
            hd["s"].append(s.astype(BF16))
            s = s * math.exp(LOG_GAMMA[hh] * CHUNK) + hd["kv"][ci]
        r.s[hh] = s
        hd["p"] = [(hd["scores"][ci] * r.mask[hh]).astype(BF16) for ci in range(n_chunks)]

    def head_dots_b(hd):
        hd["o"] = [_dot(hd["p"][ci], hd["v"][c]) + _dot(hd["qd"][ci], hd["s"][ci])
                   for ci, c in enumerate(chunk_rows)]

    def head_finish(hh, hd):
        base = 2 * A_WIDTH + 3 * B_WIDTH + hh * B_DV
        gate = _silu(z[:, base:base + B_DV])
        gnw = r.gn[:, hh * B_DV:(hh + 1) * B_DV]
        for ci, c in enumerate(chunk_rows):
            o = _center_norm(hd["o"][ci]) * gnw
            r.mix[c, A_WIDTH + hh * B_DV:A_WIDTH + (hh + 1) * B_DV] = (o * gate[c]).astype(BF16)

    mlp_piece(1)
    ua, va = [], []
    for g in range(A_GROUPS):
        gc = slice(g * A_DIM, (g + 1) * A_DIM)
        ua.append(_gelu(z[:, g * A_DIM:(g + 1) * A_DIM]))
        vg = _gelu(z[:, A_WIDTH + g * A_DIM:A_WIDTH + (g + 1) * A_DIM])
        va.append((_center_norm(vg) * r.lnw[:, gc] + r.lnb[:, gc]).astype(BF16))
        if g == 1:
            mlp_piece(2)
    heads = [head_inputs(0)]

    row = lax.broadcasted_iota(jnp.int32, (CHUNK, CHUNK), 0)
    col = lax.broadcasted_iota(jnp.int32, (CHUNK, CHUNK), 1)
    mixed = []
    for g in range(A_GROUPS):
        w_tril = jnp.where(row >= col, r.ws[g], 0.0).astype(BF16)
        mixed.append([_dot(w_tril, va[g][c]) for c in chunk_rows])
    head_dots_a(heads[0])

    for hh in range(B_HEADS):
        mlp_piece(3 + hh)
        if hh == 0:
            for g in range(A_GROUPS):
                gc = slice(g * A_DIM, (g + 1) * A_DIM)
                bias = jnp.broadcast_to(r.bs[:, g:g + 1], (CHUNK, A_DIM))
                for ci, c in enumerate(chunk_rows):
                    r.mix[c, gc] = (ua[g][c] * (mixed[g][ci] + bias)).astype(BF16)
        else:
            head_finish(hh - 1, heads[hh - 1])
        head_chain(hh, heads[hh])
        if hh + 1 < B_HEADS:
            heads.append(head_inputs(hh + 1))
        head_dots_b(heads[hh])
        if hh + 1 < B_HEADS:
            head_dots_a(heads[hh + 1])

    mlp_piece(7)
    head_finish(B_HEADS - 1, heads[B_HEADS - 1])
    half = ROW_TILE // 2
    mix_a = _dot(r.mix[:half, :], r.w_out[...])
    mlp_finish()
    mix_b = _dot(r.mix[half:, :], r.w_out[...])
    for rows, mix in ((slice(0, half), mix_a), (slice(half, ROW_TILE), mix_b)):
        x1 = x[rows] + _rms(mix, r.post_mix[...])
        r.x1[rows, :] = x1
        r.hn[rows, :] = _rms(x1, r.pre_mlp[...]).astype(BF16)


def _resident(shape):
    zeros = (0,) * len(shape)
    return pl.BlockSpec(shape, lambda i: zeros, pipeline_mode=pl.Buffered(1))


def _layer(x, xs, s0, p, inv_freq):
    batch, seq, _ = x.shape
    n_dec = xs.shape[0]
    tiles_per_seq = seq // ROW_TILE
    n_tiles = batch * tiles_per_seq
    assert n_dec == n_tiles * DEC_ROWS and DEC_GROUP % DEC_ROWS == 0

    def mixer_tile(i):
        return jnp.minimum(i, n_tiles - 1)

    def mlp_tile(i):
        return jnp.maximum(i - 1, 0)

    x_spec = pl.BlockSpec((None, ROW_TILE, D_MODEL),
                          lambda i: (mixer_tile(i) // tiles_per_seq, mixer_tile(i) % tiles_per_seq, 0))
    y_spec = pl.BlockSpec((None, ROW_TILE, D_MODEL),
                          lambda i: (mlp_tile(i) // tiles_per_seq, mlp_tile(i) % tiles_per_seq, 0))
    tok_spec = pl.BlockSpec((n_dec, None, D_MODEL), lambda i: (0, 0, 0), pipeline_mode=pl.Buffered(1))
    ys_spec = pl.BlockSpec((n_dec, None, D_MODEL), lambda i: (0, 0, 0))
    va_spec = pl.BlockSpec((n_dec, None, A_GROUPS, A_DIM), lambda i: (0, 0, 0, 0))
    dec_state_spec = pl.BlockSpec((DEC_ROWS, B_HEADS, B_DK, B_DV), lambda i: (mixer_tile(i), 0, 0, 0))
    hbm = pl.BlockSpec(memory_space=pl.ANY)
    in_specs = [
        x_spec, tok_spec, dec_state_spec,
        _resident((1, D_MODEL)),
        _resident((1, A_WIDTH)), _resident((1, A_WIDTH)),
        _resident((A_GROUPS, CHUNK, CHUNK)), _resident((CHUNK, A_GROUPS)),
        _resident((1, A_WIDTH)), _resident((1, A_WIDTH)),
        _resident((1, B_WIDTH)),
        _resident((1, D_MODEL)), _resident((1, D_MODEL)), _resident((1, D_MODEL)),
        _resident((1, B_DK)),
        hbm, hbm, hbm, hbm,
    ]
    out_specs = [
        y_spec,
        pl.BlockSpec((None, B_HEADS, B_DK, B_DV), lambda i: (mixer_tile(i) // tiles_per_seq, 0, 0, 0)),
        ys_spec, dec_state_spec, va_spec,
    ]
    out_shape = [
        jax.ShapeDtypeStruct(x.shape, F32),
        jax.ShapeDtypeStruct((batch, B_HEADS, B_DK, B_DV), F32),
        jax.ShapeDtypeStruct((n_dec, 1, D_MODEL), F32),
        jax.ShapeDtypeStruct(s0.shape, F32),
        jax.ShapeDtypeStruct((n_dec, 1, A_GROUPS, A_DIM), F32),
    ]
    scratch_shapes = [
        pltpu.VMEM((D_MODEL, IN_COLS), BF16),
        pltpu.VMEM((D_MODEL, D_MODEL), BF16),
        pltpu.VMEM((D_FF // FF_TILE, D_MODEL, FF_TILE), BF16),
        pltpu.VMEM((D_FF, D_MODEL), BF16),
        pltpu.VMEM((ROW_TILE, D_MODEL), BF16),
        pltpu.VMEM((ROW_TILE, D_MODEL), F32),
        pltpu.VMEM((ROW_TILE, D_MODEL), BF16),
        pltpu.VMEM((seq, B_DK), F32),
        pltpu.VMEM((seq, B_DK), F32),
        pltpu.VMEM((B_HEADS, CHUNK, CHUNK), F32),
        pltpu.VMEM((B_HEADS, CHUNK, B_DK), F32),
        pltpu.VMEM((B_HEADS, CHUNK, B_DK), F32),
        pltpu.VMEM((n_dec, B_WIDTH), F32),
        pltpu.VMEM((n_dec, B_WIDTH), F32),
        pltpu.VMEM((n_dec, B_WIDTH), F32),
        pltpu.VMEM((n_dec, B_WIDTH), F32),
        pltpu.VMEM((n_dec, A_WIDTH), F32),
        pltpu.VMEM((n_dec, B_WIDTH), F32),
    ]
    return pl.pallas_call(
        functools.partial(_layer_kernel, tiles_per_seq=tiles_per_seq, n_tiles=n_tiles),
        grid=(n_tiles + 1,),
        in_specs=in_specs,
        out_specs=out_specs,
        out_shape=out_shape,
        scratch_shapes=scratch_shapes,
        compiler_params=pltpu.CompilerParams(
            dimension_semantics=("arbitrary",),
            vmem_limit_bytes=VMEM_LIMIT_BYTES),
        name="hybrid_layer",
    )(x, xs, s0, p["pre_mix_w"], p["ln_v_w"], p["ln_v_b"], p["w_s"], p["b_s_t"], p["w_s0"], p["b_s0"],
      p["gn_w"], p["post_mix_w"], p["pre_mlp_w"], p["post_mlp_w"], inv_freq,
      p["w_in"], p["w_out"], p["w_up"], p["w_down"])


def kernel(x_prompt, x_sample, state_ret, pre_mix_w, w_in, ln_v_w, ln_v_b, w_s, b_s, gn_w,
           w_out, post_mix_w, pre_mlp_w, w_up, w_down, post_mlp_w):
    depth = w_in.shape[0]
    seq = x_prompt.shape[1]
    assert x_sample.shape[1] == 1 and seq % ROW_TILE == 0

    half = B_DK // 2
    inv = ROPE_BASE ** (-jnp.arange(half, dtype=F32) / half)
    inv_freq = jnp.concatenate([inv, inv])[None, :]

    yp, ys = x_prompt, x_sample
    sp_list, ss_list, vs_list = [], [], []
    for l in range(depth):
        p = {
            "pre_mix_w": pre_mix_w[l][None, :],
            "w_in": w_in[l],
            "ln_v_w": ln_v_w[l].reshape(1, A_WIDTH),
            "ln_v_b": ln_v_b[l].reshape(1, A_WIDTH),
            "w_s": w_s[l],
            "b_s_t": b_s[l].T,
            "w_s0": jnp.repeat(w_s[l][:, 0, 0], A_DIM)[None, :],
            "b_s0": jnp.repeat(b_s[l][:, 0], A_DIM)[None, :],
            "gn_w": gn_w[l].reshape(1, B_WIDTH),
            "w_out": w_out[l],
            "post_mix_w": post_mix_w[l][None, :],
            "pre_mlp_w": pre_mlp_w[l][None, :],
            "w_up": w_up[l],
            "w_down": w_down[l],
            "post_mlp_w": post_mlp_w[l][None, :],
        }
        yp, sp, ys, ss, vs = _layer(yp, ys, state_ret[l], p, inv_freq)
        sp_list.append(sp)
        ss_list.append(ss)
        vs_list.append(vs)
    return (yp, ys, jnp.stack(sp_list, axis=0), jnp.stack(ss_list, axis=0), jnp.stack(vs_list, axis=0))
```

```python
import functools
import math
from typing import Any, NamedTuple

import jax
import jax.numpy as jnp
from jax import lax
from jax.experimental import pallas as pl
from jax.experimental.pallas import tpu as pltpu

D_MODEL = 1024
A_WIDTH = 512
B_WIDTH = 512
A_GROUPS = 4
A_DIM = 128
CHUNK = 128
B_HEADS = 4
B_DK = 128
B_DV = 128
D_FF = 4096
IN_COLS = 2 * A_WIDTH + 4 * B_WIDTH
ROPE_BASE = 10000.0
EPS = 1e-6
PAST_LEN = 16384
LOG_GAMMA = [math.log(1.0 - 2.0 ** (-5.0 - h)) for h in range(B_HEADS)]

ROW_TILE = 512
FF_TILE = 512
DEC_ROWS = 4
DEC_GROUP = 16
STAGE_BYTES = 1 << 20
STAGE_SLOTS = 4
VMEM_LIMIT_BYTES = 60 * 1024 * 1024

F32 = jnp.float32
BF16 = jnp.bfloat16


def _rms(x, w):
    return x * lax.rsqrt(jnp.mean(x * x, axis=-1, keepdims=True) + EPS) * w


def _center_norm(x):
    mu = jnp.mean(x, axis=-1, keepdims=True)
    xc = x - mu
    return xc * lax.rsqrt(jnp.mean(xc * xc, axis=-1, keepdims=True) + EPS)


def _gelu(x):
    return jax.nn.gelu(x, approximate=True)


def _silu(x):
    return x * (1.0 / (1.0 + jnp.exp(-x)))


def _rope(x, cosf, sinf):
    return x * cosf + pltpu.roll(x, B_DK // 2, axis=1) * sinf


def _rope_rows(pos, inv):
    ang = pos * inv
    lane = lax.broadcasted_iota(jnp.int32, ang.shape, 1)
    sin = jnp.sin(ang)
    return jnp.cos(ang), jnp.where(lane < B_DK // 2, -sin, sin)


def _dot(a, b):
    return jnp.dot(a, b, preferred_element_type=F32)


def _dot_nt(a, b):
    return lax.dot_general(a, b, (((1,), (1,)), ((), ())), preferred_element_type=F32)


def _dot_tn(a, b):
    return lax.dot_general(a, b, (((0,), (0,)), ((), ())), preferred_element_type=F32)


class _Refs(NamedTuple):
    x: Any
    xs: Any
    s0: Any
    pre_mix: Any
    lnw: Any
    lnb: Any
    ws: Any
    bs: Any
    ws0: Any
    bs0: Any
    gn: Any
    post_mix: Any
    pre_mlp: Any
    post_mlp: Any
    inv: Any
    w_in_hbm: Any
    w_out_hbm: Any
    w_up_hbm: Any
    w_down_hbm: Any
    y: Any
    s: Any
    ys: Any
    s_dec: Any
    va: Any
    w_in: Any
    w_out: Any
    w_up: Any
    w_down: Any
    mix: Any
    x1: Any
    hn: Any
    cos: Any
    sin: Any
    mask: Any
    qdec: Any
    kdec: Any
    dq: Any
    dk: Any
    dv: Any
    dgate: Any
    douta: Any
    do: Any


def _convert_weight(w_hbm, w_bf, col_pieces=1):
    n_rows, n_cols = w_hbm.shape
    piece = n_cols // col_pieces
    rows_per = 1 << int(math.log2(STAGE_BYTES // (4 * n_cols)))
    assert n_rows % rows_per == 0 and rows_per % 16 == 0
    n_chunks = n_rows // rows_per

    ahead = STAGE_SLOTS - 1
    assert n_chunks >= ahead

    def body(stage, sem):
        def copy(c):
            slot = lax.rem(c, STAGE_SLOTS)
            return pltpu.make_async_copy(
                w_hbm.at[pl.ds(c * rows_per, rows_per), :], stage.at[slot], sem.at[slot])

        for c in range(ahead):
            copy(c).start()

        def step(c, carry):
            slot = lax.rem(c, STAGE_SLOTS)

            @pl.when(c + ahead < n_chunks)
            def _():
                copy(c + ahead).start()

            copy(c).wait()
            rows = pl.ds(pl.multiple_of(c * rows_per, rows_per), rows_per)
            if col_pieces == 1:
                w_bf[rows, :] = stage[slot].astype(BF16)
            else:
                for p in range(col_pieces):
                    w_bf[p, rows, :] = stage[slot, :, p * piece:(p + 1) * piece].astype(BF16)
            return carry

        lax.fori_loop(0, n_chunks, step, 0)

    pl.run_scoped(body, pltpu.VMEM((STAGE_SLOTS, rows_per, n_cols), F32),
                  pltpu.SemaphoreType.DMA((STAGE_SLOTS,)))


def _fill_tables(r):
    seq = r.cos.shape[0]

    def rope_block(b, carry):
        r0 = pl.multiple_of(b * ROW_TILE, ROW_TILE)
        pos = (lax.broadcasted_iota(jnp.int32, (ROW_TILE, B_DK), 0) + r0).astype(F32)
        cosf, sinf = _rope_rows(pos, r.inv[...])
        r.cos[pl.ds(r0, ROW_TILE), :] = cosf
        r.sin[pl.ds(r0, ROW_TILE), :] = sinf
        return carry

    lax.fori_loop(0, seq // ROW_TILE, rope_block, 0)
    row = lax.broadcasted_iota(jnp.int32, (CHUNK, CHUNK), 0).astype(F32)
    col = lax.broadcasted_iota(jnp.int32, (CHUNK, CHUNK), 1).astype(F32)
    diff = row - col
    for hh in range(B_HEADS):
        lg = LOG_GAMMA[hh]
        r.mask[hh] = jnp.where(diff >= 0, jnp.exp(lg * jnp.maximum(diff, 0.0)), 0.0)
        r.qdec[hh] = jnp.exp(lg * (row + 1.0))
        r.kdec[hh] = jnp.exp(lg * (CHUNK - 1.0 - row))


def _decode_projections(r):
    h = _rms(r.xs[...], r.pre_mix[...]).astype(BF16)
    z = _dot(h, r.w_in[...])
    cosf, sinf = _rope_rows(jnp.full((1, B_DK), PAST_LEN, F32), r.inv[...])
    for g in range(A_GROUPS):
        gc = slice(g * A_DIM, (g + 1) * A_DIM)
        ua = _gelu(z[:, g * A_DIM:(g + 1) * A_DIM])
        va = _gelu(z[:, A_WIDTH + g * A_DIM:A_WIDTH + (g + 1) * A_DIM])
        va = _center_norm(va) * r.lnw[:, gc] + r.lnb[:, gc]
        r.va[:, g, :] = va
        r.douta[:, gc] = ua * (r.ws0[:, gc] * va + r.bs0[:, gc])
    base = 2 * A_WIDTH
    for hh in range(B_HEADS):
        hc = slice(hh * B_DK, (hh + 1) * B_DK)
        r.dq[:, hc] = _rope(z[:, base + hh * B_DK:base + (hh + 1) * B_DK], cosf, sinf)
        r.dk[:, hc] = _rope(z[:, base + B_WIDTH + hh * B_DK:base + B_WIDTH + (hh + 1) * B_DK],
                            cosf, sinf) * (B_DK ** -0.5)
    r.dv[...] = z[:, base + 2 * B_WIDTH:base + 3 * B_WIDTH]
    r.dgate[...] = _silu(z[:, base + 3 * B_WIDTH:])
    r.do[...] = jnp.zeros_like(r.do)


def _decode_state_step(r):
    i = pl.program_id(0)
    per_group = DEC_GROUP // DEC_ROWS
    rows = pl.ds(pl.multiple_of((i // per_group) * DEC_GROUP, DEC_GROUP), DEC_GROUP)
    first = lax.rem(i, per_group) * DEC_ROWS
    rid = lax.broadcasted_iota(jnp.int32, (DEC_GROUP, B_DK), 0)
    live = jnp.logical_and(rid >= first, rid < first + DEC_ROWS)
    for hh in range(B_HEADS):
        hc = slice(hh * B_DK, (hh + 1) * B_DK)
        q = r.dq[rows, hc]
        k = r.dk[rows, hc]
        v = r.dv[rows, hc]
        gam = math.exp(LOG_GAMMA[hh])
        qk = jnp.sum(q * k, axis=-1, keepdims=True)
        vb = v.astype(BF16)
        qg = (q * gam).astype(BF16)
        cross = jnp.zeros((DEC_GROUP, B_DV), F32)
        for j in range(DEC_ROWS):
            sel = rid == first + j
            s = r.s0[j, hh]
            cross = jnp.where(sel, _dot(qg, s.astype(BF16)), cross)
            kj = jnp.where(sel, k, 0.0).astype(BF16)
            r.s_dec[j, hh] = s * gam + _dot_tn(kj, vb)
        r.do[rows, hc] = jnp.where(live, qk * v + cross, r.do[rows, hc])


def _decode_output(r):
    parts = [r.douta[...].astype(BF16)]
    for hh in range(B_HEADS):
        hc = slice(hh * B_DV, (hh + 1) * B_DV)
        o = _center_norm(r.do[:, hc]) * r.gn[:, hc]
        parts.append((o * r.dgate[:, hc]).astype(BF16))
    mix = _dot(jnp.concatenate(parts, axis=-1), r.w_out[...])
    x1 = r.xs[...] + _rms(mix, r.post_mix[...])
    hn = _rms(x1, r.pre_mlp[...]).astype(BF16)

    def mlp_piece(j, f):
        u = jnp.maximum(_dot(hn, r.w_up[j]), 0.0)
        rows = pl.ds(pl.multiple_of(j * FF_TILE, FF_TILE), FF_TILE)
        return f + _dot((u * u).astype(BF16), r.w_down[rows, :])

    f = lax.fori_loop(0, D_FF // FF_TILE, mlp_piece, jnp.zeros_like(x1))
    r.ys[...] = x1 + _rms(f, r.post_mlp[...])


def _layer_kernel(*refs, tiles_per_seq, n_tiles):
    r = _Refs(*refs)
    i = pl.program_id(0)

    @pl.when(i == 0)
    def _():
        _convert_weight(r.w_in_hbm, r.w_in)
        _convert_weight(r.w_out_hbm, r.w_out)
        _convert_weight(r.w_up_hbm, r.w_up, col_pieces=D_FF // FF_TILE)
        _convert_weight(r.w_down_hbm, r.w_down)
        _fill_tables(r)
        _decode_projections(r)
        r.x1[...] = jnp.zeros_like(r.x1)
        r.hn[...] = jnp.zeros_like(r.hn)

    @pl.when(jnp.logical_and(lax.rem(i, tiles_per_seq) == 0, i < n_tiles))
    def _():
        r.s[...] = jnp.zeros_like(r.s)

    @pl.when(i < n_tiles)
    def _():
        _prompt_step(r, tiles_per_seq=tiles_per_seq, with_mixer=True)

    @pl.when(i == n_tiles)
    def _():
        _prompt_step(r, tiles_per_seq=tiles_per_seq, with_mixer=False)
        _decode_output(r)


def _prompt_step(r, *, tiles_per_seq, with_mixer):
    n_chunks = ROW_TILE // CHUNK
    chunk_rows = [slice(c * CHUNK, (c + 1) * CHUNK) for c in range(n_chunks)]
    acc = []

    def mlp_piece(j):
        u = jnp.maximum(_dot(r.hn[...], r.w_up[j]), 0.0)
        part = _dot((u * u).astype(BF16), r.w_down[j * FF_TILE:(j + 1) * FF_TILE, :])
        acc[:] = [part if not acc else acc[0] + part]

    def mlp_finish():
        r.y[...] = r.x1[...] + _rms(acc[0], r.post_mlp[...])

    if not with_mixer:
        for j in range(D_FF // FF_TILE):
            mlp_piece(j)
        mlp_finish()
        return

    x = r.x[...]
    mlp_piece(0)
    h = _rms(x, r.pre_mix[...]).astype(BF16)
    z = _dot(h, r.w_in[...])
    _decode_state_step(r)
    tile_pos = pl.multiple_of(lax.rem(pl.program_id(0), tiles_per_seq) * ROW_TILE, ROW_TILE)
    cosf = r.cos[pl.ds(tile_pos, ROW_TILE), :]
    sinf = r.sin[pl.ds(tile_pos, ROW_TILE), :]

    def head_inputs(hh):
        base = 2 * A_WIDTH + hh * B_DK
        q = _rope(z[:, base:base + B_DK], cosf, sinf)
        k = _rope(z[:, base + B_WIDTH:base + B_WIDTH + B_DK], cosf, sinf) * (B_DK ** -0.5)
        v = z[:, base + 2 * B_WIDTH:base + 2 * B_WIDTH + B_DV].astype(BF16)
        return {
            "q": q.astype(BF16),
            "k": k.astype(BF16),
            "v": v,
            "qd": [(q[c] * r.qdec[hh]).astype(BF16) for c in chunk_rows],
            "kd": [(k[c] * r.kdec[hh]).astype(BF16) for c in chunk_rows],
        }

    def head_dots_a(hd):
        hd["scores"] = [_dot_nt(hd["q"][c], hd["k"][c]) for c in chunk_rows]
        hd["kv"] = [_dot_tn(hd["kd"][ci], hd["v"][c]) for ci, c in enumerate(chunk_rows)]

    def head_chain(hh, hd):
        s = r.s[hh]
        hd["s"] = []
        for ci in range(n_chunks):
            hd["s"].append(s.astype(BF16))
            s = s * math.exp(LOG_GAMMA[hh] * CHUNK) + hd["kv"][ci]
        r.s[hh] = s
        hd["p"] = [(hd["scores"][ci] * r.mask[hh]).astype(BF16) for ci in range(n_chunks)]

    def head_dots_b(hd):
        hd["o"] = [_dot(hd["p"][ci], hd["v"][c]) + _dot(hd["qd"][ci], hd["s"][ci])
                   for ci, c in enumerate(chunk_rows)]

    def head_finish(hh, hd):
        base = 2 * A_WIDTH + 3 * B_WIDTH + hh * B_DV
        gate = _silu(z[:, base:base + B_DV])
        gnw = r.gn[:, hh * B_DV:(hh + 1) * B_DV]
        for ci, c in enumerate(chunk_rows):
            o = _center_norm(hd["o"][ci]) * gnw
            r.mix[c, A_WIDTH + hh * B_DV:A_WIDTH + (hh + 1) * B_DV] = (o * gate[c]).astype(BF16)

    mlp_piece(1)
    ua, va = [], []
    for g in range(A_GROUPS):
        gc = slice(g * A_DIM, (g + 1) * A_DIM)
        ua.append(_gelu(z[:, g * A_DIM:(g + 1) * A_DIM]))
        vg = _gelu(z[:, A_WIDTH + g * A_DIM:A_WIDTH + (g + 1) * A_DIM])
        va.append((_center_norm(vg) * r.lnw[:, gc] + r.lnb[:, gc]).astype(BF16))
        if g == 1:
            mlp_piece(2)
    heads = [head_inputs(0)]

    row = lax.broadcasted_iota(jnp.int32, (CHUNK, CHUNK), 0)
    col = lax.broadcasted_iota(jnp.int32, (CHUNK, CHUNK), 1)
    mixed = []
    for g in range(A_GROUPS):
        w_tril = jnp.where(row >= col, r.ws[g], 0.0).astype(BF16)
        mixed.append([_dot(w_tril, va[g][c]) for c in chunk_rows])
    head_dots_a(heads[0])

    for hh in range(B_HEADS):
        mlp_piece(3 + hh)
        if hh == 0:
            for g in range(A_GROUPS):
                gc = slice(g * A_DIM, (g + 1) * A_DIM)
                bias = jnp.broadcast_to(r.bs[:, g:g + 1], (CHUNK, A_DIM))
                for ci, c in enumerate(chunk_rows):
                    r.mix[c, gc] = (ua[g][c] * (mixed[g][ci] + bias)).astype(BF16)
        else:
            head_finish(hh - 1, heads[hh - 1])
        head_chain(hh, heads[hh])
        if hh + 1 < B_HEADS:
            heads.append(head_inputs(hh + 1))
        head_dots_b(heads[hh])
        if hh + 1 < B_HEADS:
            head_dots_a(heads[hh + 1])

    mlp_piece(7)
    head_finish(B_HEADS - 1, heads[B_HEADS - 1])
    half = ROW_TILE // 2
    mix_a = _dot(r.mix[:half, :], r.w_out[...])
    mlp_finish()
    mix_b = _dot(r.mix[half:, :], r.w_out[...])
    for rows, mix in ((slice(0, half), mix_a), (slice(half, ROW_TILE), mix_b)):
        x1 = x[rows] + _rms(mix, r.post_mix[...])
        r.x1[rows, :] = x1
        r.hn[rows, :] = _rms(x1, r.pre_mlp[...]).astype(BF16)


def _resident(shape):
    zeros = (0,) * len(shape)
    return pl.BlockSpec(shape, lambda i: zeros, pipeline_mode=pl.Buffered(1))


def _layer(x, xs, s0, p, inv_freq):
    batch, seq, _ = x.shape
    n_dec = xs.shape[0]
    tiles_per_seq = seq // ROW_TILE
    n_tiles = batch * tiles_per_seq
    assert n_dec == n_tiles * DEC_ROWS and DEC_GROUP % DEC_ROWS == 0

    def mixer_tile(i):
        return jnp.minimum(i, n_tiles - 1)

    def mlp_tile(i):
        return jnp.maximum(i - 1, 0)

    x_spec = pl.BlockSpec((None, ROW_TILE, D_MODEL),
                          lambda i: (mixer_tile(i) // tiles_per_seq, mixer_tile(i) % tiles_per_seq, 0))
    y_spec = pl.BlockSpec((None, ROW_TILE, D_MODEL),
                          lambda i: (mlp_tile(i) // tiles_per_seq, mlp_tile(i) % tiles_per_seq, 0))
    tok_spec = pl.BlockSpec((n_dec, None, D_MODEL), lambda i: (0, 0, 0), pipeline_mode=pl.Buffered(1))
    ys_spec = pl.BlockSpec((n_dec, None, D_MODEL), lambda i: (0, 0, 0))
    va_spec = pl.BlockSpec((n_dec, None, A_GROUPS, A_DIM), lambda i: (0, 0, 0, 0))
    dec_state_spec = pl.BlockSpec((DEC_ROWS, B_HEADS, B_DK, B_DV), lambda i: (mixer_tile(i), 0, 0, 0))
    hbm = pl.BlockSpec(memory_space=pl.ANY)
    in_specs = [
        x_spec, tok_spec, dec_state_spec,
        _resident((1, D_MODEL)),
        _resident((1, A_WIDTH)), _resident((1, A_WIDTH)),
        _resident((A_GROUPS, CHUNK, CHUNK)), _resident((CHUNK, A_GROUPS)),
        _resident((1, A_WIDTH)), _resident((1, A_WIDTH)),
        _resident((1, B_WIDTH)),
        _resident((1, D_MODEL)), _resident((1, D_MODEL)), _resident((1, D_MODEL)),
        _resident((1, B_DK)),
        hbm, hbm, hbm, hbm,
    ]
    out_specs = [
        y_spec,
        pl.BlockSpec((None, B_HEADS, B_DK, B_DV), lambda i: (mixer_tile(i) // tiles_per_seq, 0, 0, 0)),
        ys_spec, dec_state_spec, va_spec,
    ]
    out_shape = [
        jax.ShapeDtypeStruct(x.shape, F32),
        jax.ShapeDtypeStruct((batch, B_HEADS, B_DK, B_DV), F32),
        jax.ShapeDtypeStruct((n_dec, 1, D_MODEL), F32),
        jax.ShapeDtypeStruct(s0.shape, F32),
        jax.ShapeDtypeStruct((n_dec, 1, A_GROUPS, A_DIM), F32),
    ]
    scratch_shapes = [
        pltpu.VMEM((D_MODEL, IN_COLS), BF16),
        pltpu.VMEM((D_MODEL, D_MODEL), BF16),
        pltpu.VMEM((D_FF // FF_TILE, D_MODEL, FF_TILE), BF16),
        pltpu.VMEM((D_FF, D_MODEL), BF16),
        pltpu.VMEM((ROW_TILE, D_MODEL), BF16),
        pltpu.VMEM((ROW_TILE, D_MODEL), F32),
        pltpu.VMEM((ROW_TILE, D_MODEL), BF16),
        pltpu.VMEM((seq, B_DK), F32),
        pltpu.VMEM((seq, B_DK), F32),
        pltpu.VMEM((B_HEADS, CHUNK, CHUNK), F32),
        pltpu.VMEM((B_HEADS, CHUNK, B_DK), F32),
        pltpu.VMEM((B_HEADS, CHUNK, B_DK), F32),
        pltpu.VMEM((n_dec, B_WIDTH), F32),
        pltpu.VMEM((n_dec, B_WIDTH), F32),
        pltpu.VMEM((n_dec, B_WIDTH), F32),
        pltpu.VMEM((n_dec, B_WIDTH), F32),
        pltpu.VMEM((n_dec, A_WIDTH), F32),
        pltpu.VMEM((n_dec, B_WIDTH), F32),
    ]
    return pl.pallas_call(
        functools.partial(_layer_kernel, tiles_per_seq=tiles_per_seq, n_tiles=n_tiles),
        grid=(n_tiles + 1,),
        in_specs=in_specs,
        out_specs=out_specs,
        out_shape=out_shape,
        scratch_shapes=scratch_shapes,
        compiler_params=pltpu.CompilerParams(
            dimension_semantics=("arbitrary",),
            vmem_limit_bytes=VMEM_LIMIT_BYTES),
        name="hybrid_layer",
    )(x, xs, s0, p["pre_mix_w"], p["ln_v_w"], p["ln_v_b"], p["w_s"], p["b_s_t"], p["w_s0"], p["b_s0"],
      p["gn_w"], p["post_mix_w"], p["pre_mlp_w"], p["post_mlp_w"], inv_freq,
      p["w_in"], p["w_out"], p["w_up"], p["w_down"])


def kernel(x_prompt, x_sample, state_ret, pre_mix_w, w_in, ln_v_w, ln_v_b, w_s, b_s, gn_w,
           w_out, post_mix_w, pre_mlp_w, w_up, w_down, post_mlp_w):
    depth = w_in.shape[0]
    seq = x_prompt.shape[1]
    assert x_sample.shape[1] == 1 and seq % ROW_TILE == 0

    half = B_DK // 2
    inv = ROPE_BASE ** (-jnp.arange(half, dtype=F32) / half)
    inv_freq = jnp.concatenate([inv, inv])[None, :]

    yp, ys = x_prompt, x_sample
    sp_list, ss_list, vs_list = [], [], []
    for l in range(depth):
        p = {
            "pre_mix_w": pre_mix_w[l][None, :],
            "w_in": w_in[l],
            "ln_v_w": ln_v_w[l].reshape(1, A_WIDTH),
            "ln_v_b": ln_v_b[l].reshape(1, A_WIDTH),
            "w_s": w_s[l],
            "b_s_t": b_s[l].T,
            "w_s0": jnp.repeat(w_s[l][:, 0, 0], A_DIM)[None, :],
            "b_s0": jnp.repeat(b_s[l][:, 0], A_DIM)[None, :],
            "gn_w": gn_w[l].reshape(1, B_WIDTH),
            "w_out": w_out[l],
            "post_mix_w": post_mix_w[l][None, :],
            "pre_mlp_w": pre_mlp_w[l][None, :],
            "w_up": w_up[l],
            "w_down": w_down[l],
            "post_mlp_w": post_mlp_w[l][None, :],
        }
        yp, sp, ys, ss, vs = _layer(yp, ys, state_ret[l], p, inv_freq)
        sp_list.append(sp)
        ss_list.append(ss)
        vs_list.append(vs)
    return (yp, ys, jnp.stack(sp_list, axis=0), jnp.stack(ss_list, axis=0), jnp.stack(vs_list, axis=0))
```

```python
import functools
import math
from typing import Any, NamedTuple

import jax
import jax.numpy as jnp
from jax import lax
from jax.experimental import pallas as pl
from jax.experimental.pallas import tpu as pltpu

D_MODEL = 1024
A_WIDTH = 512
B_WIDTH = 512
A_GROUPS = 4
A_DIM = 128
CHUNK = 128
B_HEADS = 4
B_DK = 128
B_DV = 128
D_FF = 4096
IN_COLS = 2 * A_WIDTH + 4 * B_WIDTH
ROPE_BASE = 10000.0
EPS = 1e-6
PAST_LEN = 16384
LOG_GAMMA = [math.log(1.0 - 2.0 ** (-5.0 - h)) for h in range(B_HEADS)]

ROW_TILE = 512
FF_TILE = 512
DEC_ROWS = 4
DEC_GROUP = 16
STAGE_BYTES = 1 << 20
STAGE_SLOTS = 6
VMEM_LIMIT_BYTES = 60 * 1024 * 1024

F32 = jnp.float32
BF16 = jnp.bfloat16


def _rms(x, w):
    return x * lax.rsqrt(jnp.mean(x * x, axis=-1, keepdims=True) + EPS) * w


def _center_norm(x):
    mu = jnp.mean(x, axis=-1, keepdims=True)
    xc = x - mu
    return xc * lax.rsqrt(jnp.mean(xc * xc, axis=-1, keepdims=True) + EPS)


def _gelu(x):
    return jax.nn.gelu(x, approximate=True)


def _silu(x):
    return x * (1.0 / (1.0 + jnp.exp(-x)))


def _rope(x, cosf, sinf):
    return x * cosf + pltpu.roll(x, B_DK // 2, axis=1) * sinf


def _rope_rows(pos, inv):
    ang = pos * inv
    lane = lax.broadcasted_iota(jnp.int32, ang.shape, 1)
    sin = jnp.sin(ang)
    return jnp.cos(ang), jnp.where(lane < B_DK // 2, -sin, sin)


def _dot(a, b):
    return jnp.dot(a, b, preferred_element_type=F32)


def _dot_nt(a, b):
    return lax.dot_general(a, b, (((1,), (1,)), ((), ())), preferred_element_type=F32)


def _dot_tn(a, b):
    return lax.dot_general(a, b, (((0,), (0,)), ((), ())), preferred_element_type=F32)


class _Refs(NamedTuple):
    x: Any
    xs: Any
    s0: Any
    pre_mix: Any
    lnw: Any
    lnb: Any
    ws: Any
    bs: Any
    ws0: Any
    bs0: Any
    gn: Any
    post_mix: Any
    pre_mlp: Any
    post_mlp: Any
    inv: Any
    w_in_hbm: Any
    w_out_hbm: Any
    w_up_hbm: Any
    w_down_hbm: Any
    y: Any
    s: Any
    ys: Any
    s_dec: Any
    va: Any
    w_in: Any
    w_out: Any
    w_up: Any
    w_down: Any
    mix: Any
    x1: Any
    hn: Any
    cos: Any
    sin: Any
    mask: Any
    qdec: Any
    kdec: Any
    dq: Any
    dk: Any
    dv: Any
    dgate: Any
    douta: Any
    do: Any


def _convert_weight(w_hbm, w_bf, col_pieces=1):
    n_rows, n_cols = w_hbm.shape
    piece = n_cols // col_pieces
    rows_per = 1 << int(math.log2(STAGE_BYTES // (4 * n_cols)))
    assert n_rows % rows_per == 0 and rows_per % 16 == 0
    n_chunks = n_rows // rows_per

    ahead = STAGE_SLOTS - 1

    def body(stage, sem):
        def copy(c):
            slot = lax.rem(c, STAGE_SLOTS)
            return pltpu.make_async_copy(
                w_hbm.at[pl.ds(c * rows_per, rows_per), :], stage.at[slot], sem.at[slot])

        for c in range(min(ahead, n_chunks)):
            copy(c).start()

        def step(c, carry):
            slot = lax.rem(c, STAGE_SLOTS)

            @pl.when(c + ahead < n_chunks)
            def _():
                copy(c + ahead).start()

            copy(c).wait()
            rows = pl.ds(pl.multiple_of(c * rows_per, rows_per), rows_per)
            if col_pieces == 1:
                w_bf[rows, :] = stage[slot].astype(BF16)
            else:
                for p in range(col_pieces):
                    w_bf[p, rows, :] = stage[slot, :, p * piece:(p + 1) * piece].astype(BF16)
            return carry

        lax.fori_loop(0, n_chunks, step, 0)

    pl.run_scoped(body, pltpu.VMEM((STAGE_SLOTS, rows_per, n_cols), F32),
                  pltpu.SemaphoreType.DMA((STAGE_SLOTS,)))


def _fill_tables(r):
    seq = r.cos.shape[0]

    def rope_block(b, carry):
        r0 = pl.multiple_of(b * ROW_TILE, ROW_TILE)
        pos = (lax.broadcasted_iota(jnp.int32, (ROW_TILE, B_DK), 0) + r0).astype(F32)
        cosf, sinf = _rope_rows(pos, r.inv[...])
        r.cos[pl.ds(r0, ROW_TILE), :] = cosf
        r.sin[pl.ds(r0, ROW_TILE), :] = sinf
        return carry

    lax.fori_loop(0, seq // ROW_TILE, rope_block, 0)
    row = lax.broadcasted_iota(jnp.int32, (CHUNK, CHUNK), 0).astype(F32)
    col = lax.broadcasted_iota(jnp.int32, (CHUNK, CHUNK), 1).astype(F32)
    diff = row - col
    for hh in range(B_HEADS):
        lg = LOG_GAMMA[hh]
        r.mask[hh] = jnp.where(diff >= 0, jnp.exp(lg * jnp.maximum(diff, 0.0)), 0.0)
        r.qdec[hh] = jnp.exp(lg * (row + 1.0))
        r.kdec[hh] = jnp.exp(lg * (CHUNK - 1.0 - row))


def _decode_projections(r):
    h = _rms(r.xs[...], r.pre_mix[...]).astype(BF16)
    z = _dot(h, r.w_in[...])
    cosf, sinf = _rope_rows(jnp.full((1, B_DK), PAST_LEN, F32), r.inv[...])
    for g in range(A_GROUPS):
        gc = slice(g * A_DIM, (g + 1) * A_DIM)
        ua = _gelu(z[:, g * A_DIM:(g + 1) * A_DIM])
        va = _gelu(z[:, A_WIDTH + g * A_DIM:A_WIDTH + (g + 1) * A_DIM])
        va = _center_norm(va) * r.lnw[:, gc] + r.lnb[:, gc]
        r.va[:, g, :] = va
        r.douta[:, gc] = ua * (r.ws0[:, gc] * va + r.bs0[:, gc])
    base = 2 * A_WIDTH
    for hh in range(B_HEADS):
        hc = slice(hh * B_DK, (hh + 1) * B_DK)
        r.dq[:, hc] = _rope(z[:, base + hh * B_DK:base + (hh + 1) * B_DK], cosf, sinf)
        r.dk[:, hc] = _rope(z[:, base + B_WIDTH + hh * B_DK:base + B_WIDTH + (hh + 1) * B_DK],
                            cosf, sinf) * (B_DK ** -0.5)
    r.dv[...] = z[:, base + 2 * B_WIDTH:base + 3 * B_WIDTH]
    r.dgate[...] = _silu(z[:, base + 3 * B_WIDTH:])
    r.do[...] = jnp.zeros_like(r.do)


def _decode_state_step(r):
    i = pl.program_id(0)
    per_group = DEC_GROUP // DEC_ROWS
    rows = pl.ds(pl.multiple_of((i // per_group) * DEC_GROUP, DEC_GROUP), DEC_GROUP)
    first = lax.rem(i, per_group) * DEC_ROWS
    rid = lax.broadcasted_iota(jnp.int32, (DEC_GROUP, B_DK), 0)
    live = jnp.logical_and(rid >= first, rid < first + DEC_ROWS)
    for hh in range(B_HEADS):
        hc = slice(hh * B_DK, (hh + 1) * B_DK)
        q = r.dq[rows, hc]
        k = r.dk[rows, hc]
        v = r.dv[rows, hc]
        gam = math.exp(LOG_GAMMA[hh])
        qk = jnp.sum(q * k, axis=-1, keepdims=True)
        qg = (q * gam).astype(BF16)
        v_sel = jnp.concatenate([jnp.where(rid == first + j, v, 0.0) for j in range(DEC_ROWS)],
                                axis=-1).astype(BF16)
        kv = _dot_tn(k.astype(BF16), v_sel)
        one_sel = jnp.concatenate([jnp.where(rid == first + j, 1.0, 0.0) for j in range(DEC_ROWS)],
                                  axis=-1).astype(BF16)
        q_cols = _dot_tn(qg, one_sel)
        cross = jnp.zeros((DEC_GROUP, B_DV), F32)
        for j in range(DEC_ROWS):
            s = r.s0[j, hh]
            cross_j = jnp.sum(q_cols[:, j * B_DV:(j + 1) * B_DV] * s, axis=0, keepdims=True)
            cross = jnp.where(rid == first + j, cross_j, cross)
            r.s_dec[j, hh] = s * gam + kv[:, j * B_DV:(j + 1) * B_DV]
        r.do[rows, hc] = jnp.where(live, qk * v + cross, r.do[rows, hc])


def _decode_output(r):
    parts = [r.douta[...].astype(BF16)]
    for hh in range(B_HEADS):
        hc = slice(hh * B_DV, (hh + 1) * B_DV)
        o = _center_norm(r.do[:, hc]) * r.gn[:, hc]
        parts.append((o * r.dgate[:, hc]).astype(BF16))
    mix = _dot(jnp.concatenate(parts, axis=-1), r.w_out[...])
    x1 = r.xs[...] + _rms(mix, r.post_mix[...])
    hn = _rms(x1, r.pre_mlp[...]).astype(BF16)

    def mlp_piece(j, f):
        u = jnp.maximum(_dot(hn, r.w_up[j]), 0.0)
        rows = pl.ds(pl.multiple_of(j * FF_TILE, FF_TILE), FF_TILE)
        return f + _dot((u * u).astype(BF16), r.w_down[rows, :])

    f = lax.fori_loop(0, D_FF // FF_TILE, mlp_piece, jnp.zeros_like(x1))
    r.ys[...] = x1 + _rms(f, r.post_mlp[...])


def _layer_kernel(*refs, tiles_per_seq, n_tiles):
    r = _Refs(*refs)
    i = pl.program_id(0)

    @pl.when(i == 0)
    def _():
        _convert_weight(r.w_in_hbm, r.w_in)
        _convert_weight(r.w_out_hbm, r.w_out)
        _convert_weight(r.w_up_hbm, r.w_up, col_pieces=D_FF // FF_TILE)
        _convert_weight(r.w_down_hbm, r.w_down)
        _fill_tables(r)
        _decode_projections(r)
        r.x1[...] = jnp.zeros_like(r.x1)
        r.hn[...] = jnp.zeros_like(r.hn)

    @pl.when(jnp.logical_and(lax.rem(i, tiles_per_seq) == 0, i < n_tiles))
    def _():
        r.s[...] = jnp.zeros_like(r.s)

    @pl.when(i < n_tiles)
    def _():
        _prompt_step(r, tiles_per_seq=tiles_per_seq, with_mixer=True)

    @pl.when(i == n_tiles)
    def _():
        _prompt_step(r, tiles_per_seq=tiles_per_seq, with_mixer=False)
        _decode_output(r)


def _prompt_step(r, *, tiles_per_seq, with_mixer):
    n_chunks = ROW_TILE // CHUNK
    chunk_rows = [slice(c * CHUNK, (c + 1) * CHUNK) for c in range(n_chunks)]
    acc = []

    def mlp_piece(j):
        u = jnp.maximum(_dot(r.hn[...], r.w_up[j]), 0.0)
        part = _dot((u * u).astype(BF16), r.w_down[j * FF_TILE:(j + 1) * FF_TILE, :])
        acc[:] = [part if not acc else acc[0] + part]

    def mlp_finish():
        r.y[...] = r.x1[...] + _rms(acc[0], r.post_mlp[...])

    if not with_mixer:
        for j in range(D_FF // FF_TILE):
            mlp_piece(j)
        mlp_finish()
        return

    x = r.x[...]
    mlp_piece(0)
    h = _rms(x, r.pre_mix[...]).astype(BF16)
    z = _dot(h, r.w_in[...])
    _decode_state_step(r)
    tile_pos = pl.multiple_of(lax.rem(pl.program_id(0), tiles_per_seq) * ROW_TILE, ROW_TILE)
    cosf = r.cos[pl.ds(tile_pos, ROW_TILE), :]
    sinf = r.sin[pl.ds(tile_pos, ROW_TILE), :]

    def head_inputs(hh):
        base = 2 * A_WIDTH + hh * B_DK
        q = _rope(z[:, base:base + B_DK], cosf, sinf)
        k = _rope(z[:, base + B_WIDTH:base + B_WIDTH + B_DK], cosf, sinf) * (B_DK ** -0.5)
        v = z[:, base + 2 * B_WIDTH:base + 2 * B_WIDTH + B_DV].astype(BF16)
        return {
            "q": q.astype(BF16),
            "k": k.astype(BF16),
            "v": v,
            "qd": [(q[c] * r.qdec[hh]).astype(BF16) for c in chunk_rows],
            "kd": [(k[c] * r.kdec[hh]).astype(BF16) for c in chunk_rows],
        }

    def head_dots_a(hd):
        hd["scores"] = [_dot_nt(hd["q"][c], hd["k"][c]) for c in chunk_rows]
        hd["kv"] = [_dot_tn(hd["kd"][ci], hd["v"][c]) for ci, c in enumerate(chunk_rows)]

    def head_chain(hh, hd):
        s = r.s[hh]
        hd["s"] = []
        for ci in range(n_chunks):
            hd["s"].append(s.astype(BF16))
            s = s * math.exp(LOG_GAMMA[hh] * CHUNK) + hd["kv"][ci]
        r.s[hh] = s
        hd["p"] = [(hd["scores"][ci] * r.mask[hh]).astype(BF16) for ci in range(n_chunks)]

    def head_dots_b(hd):
        hd["o"] = [_dot(hd["p"][ci], hd["v"][c]) + _dot(hd["qd"][ci], hd["s"][ci])
                   for ci, c in enumerate(chunk_rows)]

    def head_finish(hh, hd):
        base = 2 * A_WIDTH + 3 * B_WIDTH + hh * B_DV
        gate = _silu(z[:, base:base + B_DV])
        gnw = r.gn[:, hh * B_DV:(hh + 1) * B_DV]
        for ci, c in enumerate(chunk_rows):
            o = _center_norm(hd["o"][ci]) * gnw
            r.mix[c, A_WIDTH + hh * B_DV:A_WIDTH + (hh + 1) * B_DV] = (o * gate[c]).astype(BF16)

    mlp_piece(1)
    ua, va = [], []
    for g in range(A_GROUPS):
        gc = slice(g * A_DIM, (g + 1) * A_DIM)
        ua.append(_gelu(z[:, g * A_DIM:(g + 1) * A_DIM]))
        vg = _gelu(z[:, A_WIDTH + g * A_DIM:A_WIDTH + (g + 1) * A_DIM])
        va.append((_center_norm(vg) * r.lnw[:, gc] + r.lnb[:, gc]).astype(BF16))
        if g == 1:
            mlp_piece(2)
    heads = [head_inputs(0)]

    row = lax.broadcasted_iota(jnp.int32, (CHUNK, CHUNK), 0)
    col = lax.broadcasted_iota(jnp.int32, (CHUNK, CHUNK), 1)
    mixed = []
    for g in range(A_GROUPS):
        w_tril = jnp.where(row >= col, r.ws[g], 0.0).astype(BF16)
        mixed.append([_dot(w_tril, va[g][c]) for c in chunk_rows])
    head_dots_a(heads[0])

    for hh in range(B_HEADS):
        mlp_piece(3 + hh)
        if hh == 0:
            for g in range(A_GROUPS):
                gc = slice(g * A_DIM, (g + 1) * A_DIM)
                bias = jnp.broadcast_to(r.bs[:, g:g + 1], (CHUNK, A_DIM))
                for ci, c in enumerate(chunk_rows):
                    r.mix[c, gc] = (ua[g][c] * (mixed[g][ci] + bias)).astype(BF16)
        else:
            head_finish(hh - 1, heads[hh - 1])
        head_chain(hh, heads[hh])
        if hh + 1 < B_HEADS:
            heads.append(head_inputs(hh + 1))
        head_dots_b(heads[hh])
        if hh + 1 < B_HEADS:
            head_dots_a(heads[hh + 1])

    mlp_piece(7)
    head_finish(B_HEADS - 1, heads[B_HEADS - 1])
    half = ROW_TILE // 2
    mix_a = _dot(r.mix[:half, :], r.w_out[...])
    mlp_finish()
    mix_b = _dot(r.mix[half:, :], r.w_out[...])
    for rows, mix in ((slice(0, half), mix_a), (slice(half, ROW_TILE), mix_b)):
        x1 = x[rows] + _rms(mix, r.post_mix[...])
        r.x1[rows, :] = x1
        r.hn[rows, :] = _rms(x1, r.pre_mlp[...]).astype(BF16)


def _resident(shape):
    zeros = (0,) * len(shape)
    return pl.BlockSpec(shape, lambda i: zeros, pipeline_mode=pl.Buffered(1))


def _layer(x, xs, s0, p, inv_freq):
    batch, seq, _ = x.shape
    n_dec = xs.shape[0]
    tiles_per_seq = seq // ROW_TILE
    n_tiles = batch * tiles_per_seq
    assert n_dec == n_tiles * DEC_ROWS and DEC_GROUP % DEC_ROWS == 0

    def mixer_tile(i):
        return jnp.minimum(i, n_tiles - 1)

    def mlp_tile(i):
        return jnp.maximum(i - 1, 0)

    x_spec = pl.BlockSpec((None, ROW_TILE, D_MODEL),
                          lambda i: (mixer_tile(i) // tiles_per_seq, mixer_tile(i) % tiles_per_seq, 0))
    y_spec = pl.BlockSpec((None, ROW_TILE, D_MODEL),
                          lambda i: (mlp_tile(i) // tiles_per_seq, mlp_tile(i) % tiles_per_seq, 0))
    tok_spec = pl.BlockSpec((n_dec, None, D_MODEL), lambda i: (0, 0, 0), pipeline_mode=pl.Buffered(1))
    ys_spec = pl.BlockSpec((n_dec, None, D_MODEL), lambda i: (0, 0, 0))
    va_spec = pl.BlockSpec((n_dec, None, A_GROUPS, A_DIM), lambda i: (0, 0, 0, 0))
    dec_state_spec = pl.BlockSpec((DEC_ROWS, B_HEADS, B_DK, B_DV), lambda i: (mixer_tile(i), 0, 0, 0))
    hbm = pl.BlockSpec(memory_space=pl.ANY)
    in_specs = [
        x_spec, tok_spec, dec_state_spec,
        _resident((1, D_MODEL)),
        _resident((1, A_WIDTH)), _resident((1, A_WIDTH)),
        _resident((A_GROUPS, CHUNK, CHUNK)), _resident((CHUNK, A_GROUPS)),
        _resident((1, A_WIDTH)), _resident((1, A_WIDTH)),
        _resident((1, B_WIDTH)),
        _resident((1, D_MODEL)), _resident((1, D_MODEL)), _resident((1, D_MODEL)),
        _resident((1, B_DK)),
        hbm, hbm, hbm, hbm,
    ]
    out_specs = [
        y_spec,
        pl.BlockSpec((None, B_HEADS, B_DK, B_DV), lambda i: (mixer_tile(i) // tiles_per_seq, 0, 0, 0)),
        ys_spec, dec_state_spec, va_spec,
    ]
    out_shape = [
        jax.ShapeDtypeStruct(x.shape, F32),
        jax.ShapeDtypeStruct((batch, B_HEADS, B_DK, B_DV), F32),
        jax.ShapeDtypeStruct((n_dec, 1, D_MODEL), F32),
        jax.ShapeDtypeStruct(s0.shape, F32),
        jax.ShapeDtypeStruct((n_dec, 1, A_GROUPS, A_DIM), F32),
    ]
    scratch_shapes = [
        pltpu.VMEM((D_MODEL, IN_COLS), BF16),
        pltpu.VMEM((D_MODEL, D_MODEL), BF16),
        pltpu.VMEM((D_FF // FF_TILE, D_MODEL, FF_TILE), BF16),
        pltpu.VMEM((D_FF, D_MODEL), BF16),
        pltpu.VMEM((ROW_TILE, D_MODEL), BF16),
        pltpu.VMEM((ROW_TILE, D_MODEL), F32),
        pltpu.VMEM((ROW_TILE, D_MODEL), BF16),
        pltpu.VMEM((seq, B_DK), F32),
        pltpu.VMEM((seq, B_DK), F32),
        pltpu.VMEM((B_HEADS, CHUNK, CHUNK), F32),
        pltpu.VMEM((B_HEADS, CHUNK, B_DK), F32),
        pltpu.VMEM((B_HEADS, CHUNK, B_DK), F32),
        pltpu.VMEM((n_dec, B_WIDTH), F32),
        pltpu.VMEM((n_dec, B_WIDTH), F32),
        pltpu.VMEM((n_dec, B_WIDTH), F32),
        pltpu.VMEM((n_dec, B_WIDTH), F32),
        pltpu.VMEM((n_dec, A_WIDTH), F32),
        pltpu.VMEM((n_dec, B_WIDTH), F32),
    ]
    return pl.pallas_call(
        functools.partial(_layer_kernel, tiles_per_seq=tiles_per_seq, n_tiles=n_tiles),
        grid=(n_tiles + 1,),
        in_specs=in_specs,
        out_specs=out_specs,
        out_shape=out_shape,
        scratch_shapes=scratch_shapes,
        compiler_params=pltpu.CompilerParams(
            dimension_semantics=("arbitrary",),
            vmem_limit_bytes=VMEM_LIMIT_BYTES),
        name="hybrid_layer",
    )(x, xs, s0, p["pre_mix_w"], p["ln_v_w"], p["ln_v_b"], p["w_s"], p["b_s_t"], p["w_s0"], p["b_s0"],
      p["gn_w"], p["post_mix_w"], p["pre_mlp_w"], p["post_mlp_w"], inv_freq,
      p["w_in"], p["w_out"], p["w_up"], p["w_down"])


def kernel(x_prompt, x_sample, state_ret, pre_mix_w, w_in, ln_v_w, ln_v_b, w_s, b_s, gn_w,
           w_out, post_mix_w, pre_mlp_w, w_up, w_down, post_mlp_w):
    depth = w_in.shape[0]
    seq = x_prompt.shape[1]
    assert x_sample.shape[1] == 1 and seq % ROW_TILE == 0

    half = B_DK // 2
    inv = ROPE_BASE ** (-jnp.arange(half, dtype=F32) / half)
    inv_freq = jnp.concatenate([inv, inv])[None, :]

    yp, ys = x_prompt, x_sample
    sp_list, ss_list, vs_list = [], [], []
    for l in range(depth):
        p = {
            "pre_mix_w": pre_mix_w[l][None, :],
            "w_in": w_in[l],
            "ln_v_w": ln_v_w[l].reshape(1, A_WIDTH),
            "ln_v_b": ln_v_b[l].reshape(1, A_WIDTH),
            "w_s": w_s[l],
            "b_s_t": b_s[l].T,
            "w_s0": jnp.repeat(w_s[l][:, 0, 0], A_DIM)[None, :],
            "b_s0": jnp.repeat(b_s[l][:, 0], A_DIM)[None, :],
            "gn_w": gn_w[l].reshape(1, B_WIDTH),
            "w_out": w_out[l],
            "post_mix_w": post_mix_w[l][None, :],
            "pre_mlp_w": pre_mlp_w[l][None, :],
            "w_up": w_up[l],
            "w_down": w_down[l],
            "post_mlp_w": post_mlp_w[l][None, :],
        }
        yp, sp, ys, ss, vs = _layer(yp, ys, state_ret[l], p, inv_freq)
        sp_list.append(sp)
        ss_list.append(ss)
        vs_list.append(vs)
    return (yp, ys, jnp.stack(sp_list, axis=0), jnp.stack(ss_list, axis=0), jnp.stack(vs_list, axis=0))
```

```python
import functools
import math
from typing import Any, NamedTuple

import jax
import jax.numpy as jnp
from jax import lax
from jax.experimental import pallas as pl
from jax.experimental.pallas import tpu as pltpu

D_MODEL = 1024
A_WIDTH = 512
B_WIDTH = 512
A_GROUPS = 4
A_DIM = 128
CHUNK = 128
B_HEADS = 4
B_DK = 128
B_DV = 128
D_FF = 4096
IN_COLS = 2 * A_WIDTH + 4 * B_WIDTH
ROPE_BASE = 10000.0
EPS = 1e-6
PAST_LEN = 16384
LOG_GAMMA = [math.log(1.0 - 2.0 ** (-5.0 - h)) for h in range(B_HEADS)]

ROW_TILE = 512
FF_TILE = 512
DEC_ROWS = 4
DEC_GROUP = 16
STAGE_BYTES = 1 << 20
STAGE_SLOTS = 6
VMEM_LIMIT_BYTES = 60 * 1024 * 1024

F32 = jnp.float32
BF16 = jnp.bfloat16


def _rms(x, w):
    return x * lax.rsqrt(jnp.mean(x * x, axis=-1, keepdims=True) + EPS) * w


def _center_norm(x):
    mu = jnp.mean(x, axis=-1, keepdims=True)
    xc = x - mu
    return xc * lax.rsqrt(jnp.mean(xc * xc, axis=-1, keepdims=True) + EPS)


def _gelu(x):
    return jax.nn.gelu(x, approximate=True)


def _silu(x):
    return x * (1.0 / (1.0 + jnp.exp(-x)))


def _rope(x, cosf, sinf):
    return x * cosf + pltpu.roll(x, B_DK // 2, axis=1) * sinf


def _rope_rows(pos, inv):
    ang = pos * inv
    lane = lax.broadcasted_iota(jnp.int32, ang.shape, 1)
    sin = jnp.sin(ang)
    return jnp.cos(ang), jnp.where(lane < B_DK // 2, -sin, sin)


def _dot(a, b):
    return jnp.dot(a, b, preferred_element_type=F32)


def _dot_nt(a, b):
    return lax.dot_general(a, b, (((1,), (1,)), ((), ())), preferred_element_type=F32)


def _dot_tn(a, b):
    return lax.dot_general(a, b, (((0,), (0,)), ((), ())), preferred_element_type=F32)


class _Refs(NamedTuple):
    x: Any
    xs: Any
    s0: Any
    pre_mix: Any
    lnw: Any
    lnb: Any
    ws: Any
    bs: Any
    ws0: Any
    bs0: Any
    gn: Any
    post_mix: Any
    pre_mlp: Any
    post_mlp: Any
    inv: Any
    w_in_hbm: Any
    w_out_hbm: Any
    w_up_hbm: Any
    w_down_hbm: Any
    y: Any
    s: Any
    ys: Any
    s_dec: Any
    va: Any
    w_in: Any
    w_out: Any
    w_up: Any
    w_down: Any
    mix: Any
    x1: Any
    hn: Any
    cos: Any
    sin: Any
    mask: Any
    qdec: Any
    kdec: Any
    dq: Any
    dk: Any
    dv: Any
    dgate: Any
    douta: Any
    do: Any


def _convert_weight(w_hbm, w_bf, col_pieces=1):
    n_rows, n_cols = w_hbm.shape
    piece = n_cols // col_pieces
    rows_per = 1 << int(math.log2(STAGE_BYTES // (4 * n_cols)))
    assert n_rows % rows_per == 0 and rows_per % 16 == 0
    n_chunks = n_rows // rows_per

    ahead = STAGE_SLOTS - 1

    def body(stage, sem):
        def copy(c):
            slot = lax.rem(c, STAGE_SLOTS)
            return pltpu.make_async_copy(
                w_hbm.at[pl.ds(c * rows_per, rows_per), :], stage.at[slot], sem.at[slot])

        for c in range(min(ahead, n_chunks)):
            copy(c).start()

        def step(c, carry):
            slot = lax.rem(c, STAGE_SLOTS)

            @pl.when(c + ahead < n_chunks)
            def _():
                copy(c + ahead).start()

            copy(c).wait()
            rows = pl.ds(pl.multiple_of(c * rows_per, rows_per), rows_per)
            if col_pieces == 1:
                w_bf[rows, :] = stage[slot].astype(BF16)
            else:
                for p in range(col_pieces):
                    w_bf[p, rows, :] = stage[slot, :, p * piece:(p + 1) * piece].astype(BF16)
            return carry

        lax.fori_loop(0, n_chunks, step, 0)

    pl.run_scoped(body, pltpu.VMEM((STAGE_SLOTS, rows_per, n_cols), F32),
                  pltpu.SemaphoreType.DMA((STAGE_SLOTS,)))


def _fill_tables(r):
    seq = r.cos.shape[0]

    def rope_block(b, carry):
        r0 = pl.multiple_of(b * ROW_TILE, ROW_TILE)
        pos = (lax.broadcasted_iota(jnp.int32, (ROW_TILE, B_DK), 0) + r0).astype(F32)
        cosf, sinf = _rope_rows(pos, r.inv[...])
        r.cos[pl.ds(r0, ROW_TILE), :] = cosf
        r.sin[pl.ds(r0, ROW_TILE), :] = sinf
        return carry

    lax.fori_loop(0, seq // ROW_TILE, rope_block, 0)
    row = lax.broadcasted_iota(jnp.int32, (CHUNK, CHUNK), 0).astype(F32)
    col = lax.broadcasted_iota(jnp.int32, (CHUNK, CHUNK), 1).astype(F32)
    diff = row - col
    for hh in range(B_HEADS):
        lg = LOG_GAMMA[hh]
        r.mask[hh] = jnp.where(diff >= 0, jnp.exp(lg * jnp.maximum(diff, 0.0)), 0.0)
        r.qdec[hh] = jnp.exp(lg * (row + 1.0))
        r.kdec[hh] = jnp.exp(lg * (CHUNK - 1.0 - row))


def _decode_projections(r):
    h = _rms(r.xs[...], r.pre_mix[...]).astype(BF16)
    z = _dot(h, r.w_in[...])
    cosf, sinf = _rope_rows(jnp.full((1, B_DK), PAST_LEN, F32), r.inv[...])
    for g in range(A_GROUPS):
        gc = slice(g * A_DIM, (g + 1) * A_DIM)
        ua = _gelu(z[:, g * A_DIM:(g + 1) * A_DIM])
        va = _gelu(z[:, A_WIDTH + g * A_DIM:A_WIDTH + (g + 1) * A_DIM])
        va = _center_norm(va) * r.lnw[:, gc] + r.lnb[:, gc]
        r.va[:, g, :] = va
        r.douta[:, gc] = ua * (r.ws0[:, gc] * va + r.bs0[:, gc])
    base = 2 * A_WIDTH
    for hh in range(B_HEADS):
        hc = slice(hh * B_DK, (hh + 1) * B_DK)
        r.dq[:, hc] = _rope(z[:, base + hh * B_DK:base + (hh + 1) * B_DK], cosf, sinf)
        r.dk[:, hc] = _rope(z[:, base + B_WIDTH + hh * B_DK:base + B_WIDTH + (hh + 1) * B_DK],
                            cosf, sinf) * (B_DK ** -0.5)
    r.dv[...] = z[:, base + 2 * B_WIDTH:base + 3 * B_WIDTH]
    r.dgate[...] = _silu(z[:, base + 3 * B_WIDTH:])
    r.do[...] = jnp.zeros_like(r.do)


def _decode_state_step(r):
    i = pl.program_id(0)
    per_group = DEC_GROUP // DEC_ROWS
    rows = pl.ds(pl.multiple_of((i // per_group) * DEC_GROUP, DEC_GROUP), DEC_GROUP)
    first = lax.rem(i, per_group) * DEC_ROWS
    rid = lax.broadcasted_iota(jnp.int32, (DEC_GROUP, B_DK), 0)
    live = jnp.logical_and(rid >= first, rid < first + DEC_ROWS)
    for hh in range(B_HEADS):
        hc = slice(hh * B_DK, (hh + 1) * B_DK)
        q = r.dq[rows, hc]
        k = r.dk[rows, hc]
        v = r.dv[rows, hc]
        gam = math.exp(LOG_GAMMA[hh])
        qk = jnp.sum(q * k, axis=-1, keepdims=True)
        qg = (q * gam).astype(BF16)
        v_sel = jnp.concatenate([jnp.where(rid == first + j, v, 0.0) for j in range(DEC_ROWS)],
                                axis=-1).astype(BF16)
        kv = _dot_tn(k.astype(BF16), v_sel)
        one_sel = jnp.concatenate([jnp.where(rid == first + j, 1.0, 0.0) for j in range(DEC_ROWS)],
                                  axis=-1).astype(BF16)
        q_cols = _dot_tn(qg, one_sel)
        cross = jnp.zeros((DEC_GROUP, B_DV), F32)
        for j in range(DEC_ROWS):
            s = r.s0[j, hh]
            cross_j = jnp.sum(q_cols[:, j * B_DV:(j + 1) * B_DV] * s, axis=0, keepdims=True)
            cross = jnp.where(rid == first + j, cross_j, cross)
            r.s_dec[j, hh] = s * gam + kv[:, j * B_DV:(j + 1) * B_DV]
        r.do[rows, hc] = jnp.where(live, qk * v + cross, r.do[rows, hc])


def _decode_output(r):
    parts = [r.douta[...].astype(BF16)]
    for hh in range(B_HEADS):
        hc = slice(hh * B_DV, (hh + 1) * B_DV)
        o = _center_norm(r.do[:, hc]) * r.gn[:, hc]
        parts.append((o * r.dgate[:, hc]).astype(BF16))
    mix = _dot(jnp.concatenate(parts, axis=-1), r.w_out[...])
    x1 = r.xs[...] + _rms(mix, r.post_mix[...])
    hn = _rms(x1, r.pre_mlp[...]).astype(BF16)

    def mlp_piece(j, f):
        u = jnp.maximum(_dot(hn, r.w_up[j]), 0.0)
        rows = pl.ds(pl.multiple_of(j * FF_TILE, FF_TILE), FF_TILE)
        return f + _dot((u * u).astype(BF16), r.w_down[rows, :])

    f = lax.fori_loop(0, D_FF // FF_TILE, mlp_piece, jnp.zeros_like(x1))
    r.ys[...] = x1 + _rms(f, r.post_mlp[...])


def _layer_kernel(*refs, tiles_per_seq, n_tiles):
    r = _Refs(*refs)
    i = pl.program_id(0)

    @pl.when(i == 0)
    def _():
        _convert_weight(r.w_in_hbm, r.w_in)
        _convert_weight(r.w_out_hbm, r.w_out)
        _convert_weight(r.w_up_hbm, r.w_up, col_pieces=D_FF // FF_TILE)
        _convert_weight(r.w_down_hbm, r.w_down)
        _fill_tables(r)
        _decode_projections(r)
        r.x1[...] = jnp.zeros_like(r.x1)
        r.hn[...] = jnp.zeros_like(r.hn)

    @pl.when(jnp.logical_and(lax.rem(i, tiles_per_seq) == 0, i < n_tiles))
    def _():
        r.s[...] = jnp.zeros_like(r.s)

    @pl.when(i < n_tiles)
    def _():
        _prompt_step(r, tiles_per_seq=tiles_per_seq, with_mixer=True)

    @pl.when(i == n_tiles)
    def _():
        _prompt_step(r, tiles_per_seq=tiles_per_seq, with_mixer=False)
        _decode_output(r)


def _prompt_step(r, *, tiles_per_seq, with_mixer):
    n_chunks = ROW_TILE // CHUNK
    chunk_rows = [slice(c * CHUNK, (c + 1) * CHUNK) for c in range(n_chunks)]
    acc = []

    def mlp_piece(j):
        u = jnp.maximum(_dot(r.hn[...], r.w_up[j]), 0.0)
        part = _dot((u * u).astype(BF16), r.w_down[j * FF_TILE:(j + 1) * FF_TILE, :])
        acc[:] = [part if not acc else acc[0] + part]

    def mlp_finish():
        r.y[...] = r.x1[...] + _rms(acc[0], r.post_mlp[...])

    if not with_mixer:
        for j in range(D_FF // FF_TILE):
            mlp_piece(j)
        mlp_finish()
        return

    x = r.x[...]
    mlp_piece(0)
    h = _rms(x, r.pre_mix[...]).astype(BF16)
    split = 2 * A_WIDTH + 2 * B_WIDTH
    z1 = _dot(h, r.w_in[:, :split])
    _decode_state_step(r)
    z2 = _dot(h, r.w_in[:, split:])
    tile_pos = pl.multiple_of(lax.rem(pl.program_id(0), tiles_per_seq) * ROW_TILE, ROW_TILE)
    cosf = r.cos[pl.ds(tile_pos, ROW_TILE), :]
    sinf = r.sin[pl.ds(tile_pos, ROW_TILE), :]

    def head_inputs(hh):
        base = 2 * A_WIDTH + hh * B_DK
        q = _rope(z1[:, base:base + B_DK], cosf, sinf)
        k = _rope(z1[:, base + B_WIDTH:base + B_WIDTH + B_DK], cosf, sinf) * (B_DK ** -0.5)
        v = z2[:, hh * B_DV:(hh + 1) * B_DV].astype(BF16)
        return {
            "q": q.astype(BF16),
            "k": k.astype(BF16),
            "v": v,
            "qd": [(q[c] * r.qdec[hh]).astype(BF16) for c in chunk_rows],
            "kd": [(k[c] * r.kdec[hh]).astype(BF16) for c in chunk_rows],
        }

    def head_dots_a(hd):
        hd["scores"] = [_dot_nt(hd["q"][c], hd["k"][c]) for c in chunk_rows]
        hd["kv"] = [_dot_tn(hd["kd"][ci], hd["v"][c]) for ci, c in enumerate(chunk_rows)]

    def head_chain(hh, hd):
        s = r.s[hh]
        hd["s"] = []
        for ci in range(n_chunks):
            hd["s"].append(s.astype(BF16))
            s = s * math.exp(LOG_GAMMA[hh] * CHUNK) + hd["kv"][ci]
        r.s[hh] = s
        hd["p"] = [(hd["scores"][ci] * r.mask[hh]).astype(BF16) for ci in range(n_chunks)]

    def head_dots_b(hd):
        hd["o"] = [_dot(hd["p"][ci], hd["v"][c]) + _dot(hd["qd"][ci], hd["s"][ci])
                   for ci, c in enumerate(chunk_rows)]

    def head_finish(hh, hd):
        base = B_WIDTH + hh * B_DV
        gate = _silu(z2[:, base:base + B_DV])
        gnw = r.gn[:, hh * B_DV:(hh + 1) * B_DV]
        for ci, c in enumerate(chunk_rows):
            o = _center_norm(hd["o"][ci]) * gnw
            r.mix[c, A_WIDTH + hh * B_DV:A_WIDTH + (hh + 1) * B_DV] = (o * gate[c]).astype(BF16)

    mlp_piece(1)
    ua, va = [], []
    for g in range(A_GROUPS):
        gc = slice(g * A_DIM, (g + 1) * A_DIM)
        ua.append(_gelu(z1[:, g * A_DIM:(g + 1) * A_DIM]))
        vg = _gelu(z1[:, A_WIDTH + g * A_DIM:A_WIDTH + (g + 1) * A_DIM])
        va.append((_center_norm(vg) * r.lnw[:, gc] + r.lnb[:, gc]).astype(BF16))
        if g == 1:
            mlp_piece(2)
    heads = [head_inputs(0)]

    row = lax.broadcasted_iota(jnp.int32, (CHUNK, CHUNK), 0)
    col = lax.broadcasted_iota(jnp.int32, (CHUNK, CHUNK), 1)
    mixed = []
    for g in range(A_GROUPS):
        w_tril = jnp.where(row >= col, r.ws[g], 0.0).astype(BF16)
        mixed.append([_dot(w_tril, va[g][c]) for c in chunk_rows])
    head_dots_a(heads[0])

    for hh in range(B_HEADS):
        mlp_piece(3 + hh)
        if hh == 0:
            for g in range(A_GROUPS):
                gc = slice(g * A_DIM, (g + 1) * A_DIM)
                bias = jnp.broadcast_to(r.bs[:, g:g + 1], (CHUNK, A_DIM))
                for ci, c in enumerate(chunk_rows):
                    r.mix[c, gc] = (ua[g][c] * (mixed[g][ci] + bias)).astype(BF16)
        else:
            head_finish(hh - 1, heads[hh - 1])
        head_chain(hh, heads[hh])
        if hh + 1 < B_HEADS:
            heads.append(head_inputs(hh + 1))
        head_dots_b(heads[hh])
        if hh + 1 < B_HEADS:
            head_dots_a(heads[hh + 1])

    mlp_piece(7)
    head_finish(B_HEADS - 1, heads[B_HEADS - 1])
    half = ROW_TILE // 2
    mix_a = _dot(r.mix[:half, :], r.w_out[...])
    mlp_finish()
    mix_b = _dot(r.mix[half:, :], r.w_out[...])
    for rows, mix in ((slice(0, half), mix_a), (slice(half, ROW_TILE), mix_b)):
        x1 = x[rows] + _rms(mix, r.post_mix[...])
        r.x1[rows, :] = x1
        r.hn[rows, :] = _rms(x1, r.pre_mlp[...]).astype(BF16)


def _resident(shape):
    zeros = (0,) * len(shape)
    return pl.BlockSpec(shape, lambda i: zeros, pipeline_mode=pl.Buffered(1))


def _layer(x, xs, s0, p, inv_freq):
    batch, seq, _ = x.shape
    n_dec = xs.shape[0]
    tiles_per_seq = seq // ROW_TILE
    n_tiles = batch * tiles_per_seq
    assert n_dec == n_tiles * DEC_ROWS and DEC_GROUP % DEC_ROWS == 0

    def mixer_tile(i):
        return jnp.minimum(i, n_tiles - 1)

    def mlp_tile(i):
        return jnp.maximum(i - 1, 0)

    x_spec = pl.BlockSpec((None, ROW_TILE, D_MODEL),
                          lambda i: (mixer_tile(i) // tiles_per_seq, mixer_tile(i) % tiles_per_seq, 0))
    y_spec = pl.BlockSpec((None, ROW_TILE, D_MODEL),
                          lambda i: (mlp_tile(i) // tiles_per_seq, mlp_tile(i) % tiles_per_seq, 0))
    tok_spec = pl.BlockSpec((n_dec, None, D_MODEL), lambda i: (0, 0, 0), pipeline_mode=pl.Buffered(1))
    ys_spec = pl.BlockSpec((n_dec, None, D_MODEL), lambda i: (0, 0, 0))
    va_spec = pl.BlockSpec((n_dec, None, A_GROUPS, A_DIM), lambda i: (0, 0, 0, 0))
    dec_state_spec = pl.BlockSpec((DEC_ROWS, B_HEADS, B_DK, B_DV), lambda i: (mixer_tile(i), 0, 0, 0))
    hbm = pl.BlockSpec(memory_space=pl.ANY)
    in_specs = [
        x_spec, tok_spec, dec_state_spec,
        _resident((1, D_MODEL)),
        _resident((1, A_WIDTH)), _resident((1, A_WIDTH)),
        _resident((A_GROUPS, CHUNK, CHUNK)), _resident((CHUNK, A_GROUPS)),
        _resident((1, A_WIDTH)), _resident((1, A_WIDTH)),
        _resident((1, B_WIDTH)),
        _resident((1, D_MODEL)), _resident((1, D_MODEL)), _resident((1, D_MODEL)),
        _resident((1, B_DK)),
        hbm, hbm, hbm, hbm,
    ]
    out_specs = [
        y_spec,
        pl.BlockSpec((None, B_HEADS, B_DK, B_DV), lambda i: (mixer_tile(i) // tiles_per_seq, 0, 0, 0)),
        ys_spec, dec_state_spec, va_spec,
    ]
    out_shape = [
        jax.ShapeDtypeStruct(x.shape, F32),
        jax.ShapeDtypeStruct((batch, B_HEADS, B_DK, B_DV), F32),
        jax.ShapeDtypeStruct((n_dec, 1, D_MODEL), F32),
        jax.ShapeDtypeStruct(s0.shape, F32),
        jax.ShapeDtypeStruct((n_dec, 1, A_GROUPS, A_DIM), F32),
    ]
    scratch_shapes = [
        pltpu.VMEM((D_MODEL, IN_COLS), BF16),
        pltpu.VMEM((D_MODEL, D_MODEL), BF16),
        pltpu.VMEM((D_FF // FF_TILE, D_MODEL, FF_TILE), BF16),
        pltpu.VMEM((D_FF, D_MODEL), BF16),
        pltpu.VMEM((ROW_TILE, D_MODEL), BF16),
        pltpu.VMEM((ROW_TILE, D_MODEL), F32),
        pltpu.VMEM((ROW_TILE, D_MODEL), BF16),
        pltpu.VMEM((seq, B_DK), F32),
        pltpu.VMEM((seq, B_DK), F32),
        pltpu.VMEM((B_HEADS, CHUNK, CHUNK), F32),
        pltpu.VMEM((B_HEADS, CHUNK, B_DK), F32),
        pltpu.VMEM((B_HEADS, CHUNK, B_DK), F32),
        pltpu.VMEM((n_dec, B_WIDTH), F32),
        pltpu.VMEM((n_dec, B_WIDTH), F32),
        pltpu.VMEM((n_dec, B_WIDTH), F32),
        pltpu.VMEM((n_dec, B_WIDTH), F32),
        pltpu.VMEM((n_dec, A_WIDTH), F32),
        pltpu.VMEM((n_dec, B_WIDTH), F32),
    ]
    return pl.pallas_call(
        functools.partial(_layer_kernel, tiles_per_seq=tiles_per_seq, n_tiles=n_tiles),
        grid=(n_tiles + 1,),
        in_specs=in_specs,
        out_specs=out_specs,
        out_shape=out_shape,
        scratch_shapes=scratch_shapes,
        compiler_params=pltpu.CompilerParams(
            dimension_semantics=("arbitrary",),
            vmem_limit_bytes=VMEM_LIMIT_BYTES),
        name="hybrid_layer",
    )(x, xs, s0, p["pre_mix_w"], p["ln_v_w"], p["ln_v_b"], p["w_s"], p["b_s_t"], p["w_s0"], p["b_s0"],
      p["gn_w"], p["post_mix_w"], p["pre_mlp_w"], p["post_mlp_w"], inv_freq,
      p["w_in"], p["w_out"], p["w_up"], p["w_down"])


def kernel(x_prompt, x_sample, state_ret, pre_mix_w, w_in, ln_v_w, ln_v_b, w_s, b_s, gn_w,
           w_out, post_mix_w, pre_mlp_w, w_up, w_down, post_mlp_w):
    depth = w_in.shape[0]
    seq = x_prompt.shape[1]
    assert x_sample.shape[1] == 1 and seq % ROW_TILE == 0

    half = B_DK // 2
    inv = ROPE_BASE ** (-jnp.arange(half, dtype=F32) / half)
    inv_freq = jnp.concatenate([inv, inv])[None, :]

    yp, ys = x_prompt, x_sample
    sp_list, ss_list, vs_list = [], [], []
    for l in range(depth):
        p = {
            "pre_mix_w": pre_mix_w[l][None, :],
            "w_in": w_in[l],
            "ln_v_w": ln_v_w[l].reshape(1, A_WIDTH),
            "ln_v_b": ln_v_b[l].reshape(1, A_WIDTH),
            "w_s": w_s[l],
            "b_s_t": b_s[l].T,
            "w_s0": jnp.repeat(w_s[l][:, 0, 0], A_DIM)[None, :],
            "b_s0": jnp.repeat(b_s[l][:, 0], A_DIM)[None, :],
            "gn_w": gn_w[l].reshape(1, B_WIDTH),
            "w_out": w_out[l],
            "post_mix_w": post_mix_w[l][None, :],
            "pre_mlp_w": pre_mlp_w[l][None, :],
            "w_up": w_up[l],
            "w_down": w_down[l],
            "post_mlp_w": post_mlp_w[l][None, :],
        }
        yp, sp, ys, ss, vs = _layer(yp, ys, state_ret[l], p, inv_freq)
        sp_list.append(sp)
        ss_list.append(ss)
        vs_list.append(vs)
    return (yp, ys, jnp.stack(sp_list, axis=0), jnp.stack(ss_list, axis=0), jnp.stack(vs_list, axis=0))
```

```python
import functools
import math
from typing import Any, NamedTuple

import jax
import jax.numpy as jnp
from jax import lax
from jax.experimental import pallas as pl
from jax.experimental.pallas import tpu as pltpu

D_MODEL = 1024
A_WIDTH = 512
B_WIDTH = 512
A_GROUPS = 4
A_DIM = 128
CHUNK = 128
B_HEADS = 4
B_DK = 128
B_DV = 128
D_FF = 4096
IN_COLS = 2 * A_WIDTH + 4 * B_WIDTH
ROPE_BASE = 10000.0
EPS = 1e-6
PAST_LEN = 16384
LOG_GAMMA = [math.log(1.0 - 2.0 ** (-5.0 - h)) for h in range(B_HEADS)]

ROW_TILE = 512
FF_TILE = 1024
MLP_PIECES = 2 * (D_FF // FF_TILE)
DEC_ROWS = 4
DEC_GROUP = 16
STAGE_BYTES = 1 << 20
STAGE_SLOTS = 6
VMEM_LIMIT_BYTES = 60 * 1024 * 1024

F32 = jnp.float32
BF16 = jnp.bfloat16


def _rms(x, w):
    return x * lax.rsqrt(jnp.mean(x * x, axis=-1, keepdims=True) + EPS) * w


def _center_norm(x):
    mu = jnp.mean(x, axis=-1, keepdims=True)
    xc = x - mu
    return xc * lax.rsqrt(jnp.mean(xc * xc, axis=-1, keepdims=True) + EPS)


def _gelu(x):
    return jax.nn.gelu(x, approximate=True)


def _silu(x):
    return x * (1.0 / (1.0 + jnp.exp(-x)))


def _rope(x, cosf, sinf):
    return x * cosf + pltpu.roll(x, B_DK // 2, axis=1) * sinf


def _rope_rows(pos, inv):
    ang = pos * inv
    lane = lax.broadcasted_iota(jnp.int32, ang.shape, 1)
    sin = jnp.sin(ang)
    return jnp.cos(ang), jnp.where(lane < B_DK // 2, -sin, sin)


def _dot(a, b):
    return jnp.dot(a, b, preferred_element_type=F32)


def _dot_nt(a, b):
    return lax.dot_general(a, b, (((1,), (1,)), ((), ())), preferred_element_type=F32)


def _dot_tn(a, b):
    return lax.dot_general(a, b, (((0,), (0,)), ((), ())), preferred_element_type=F32)


class _Refs(NamedTuple):
    x: Any
    xs: Any
    s0: Any
    pre_mix: Any
    lnw: Any
    lnb: Any
    ws: Any
    bs: Any
    ws0: Any
    bs0: Any
    gn: Any
    post_mix: Any
    pre_mlp: Any
    post_mlp: Any
    inv: Any
    w_in_hbm: Any
    w_out_hbm: Any
    w_up_hbm: Any
    w_down_hbm: Any
    y: Any
    s: Any
    ys: Any
    s_dec: Any
    va: Any
    w_in: Any
    w_out: Any
    w_up: Any
    w_down: Any
    mix: Any
    x1: Any
    hn: Any
    cos: Any
    sin: Any
    mask: Any
    qdec: Any
    kdec: Any
    dq: Any
    dk: Any
    dv: Any
    dgate: Any
    douta: Any
    do: Any


def _convert_weight(w_hbm, w_bf, col_pieces=1, meanwhile=None):
    n_rows, n_cols = w_hbm.shape
    piece = n_cols // col_pieces
    rows_per = 1 << int(math.log2(STAGE_BYTES // (4 * n_cols)))
    assert n_rows % rows_per == 0 and rows_per % 16 == 0
    n_chunks = n_rows // rows_per

    ahead = STAGE_SLOTS - 1

    def body(stage, sem):
        def copy(c):
            slot = lax.rem(c, STAGE_SLOTS)
            return pltpu.make_async_copy(
                w_hbm.at[pl.ds(c * rows_per, rows_per), :], stage.at[slot], sem.at[slot])

        for c in range(min(ahead, n_chunks)):
            copy(c).start()
        if meanwhile is not None:
            meanwhile()

        def step(c, carry):
            slot = lax.rem(c, STAGE_SLOTS)

            @pl.when(c + ahead < n_chunks)
            def _():
                copy(c + ahead).start()

            copy(c).wait()
            rows = pl.ds(pl.multiple_of(c * rows_per, rows_per), rows_per)
            if col_pieces == 1:
                w_bf[rows, :] = stage[slot].astype(BF16)
            else:
                for p in range(col_pieces):
                    w_bf[p, rows, :] = stage[slot, :, p * piece:(p + 1) * piece].astype(BF16)
            return carry

        lax.fori_loop(0, n_chunks, step, 0)

    pl.run_scoped(body, pltpu.VMEM((STAGE_SLOTS, rows_per, n_cols), F32),
                  pltpu.SemaphoreType.DMA((STAGE_SLOTS,)))


def _fill_rope_tables(r, part, n_parts):
    blocks = r.cos.shape[0] // ROW_TILE // n_parts

    def rope_block(b, carry):
        r0 = pl.multiple_of(b * ROW_TILE, ROW_TILE)
        pos = (lax.broadcasted_iota(jnp.int32, (ROW_TILE, B_DK), 0) + r0).astype(F32)
        cosf, sinf = _rope_rows(pos, r.inv[...])
        r.cos[pl.ds(r0, ROW_TILE), :] = cosf
        r.sin[pl.ds(r0, ROW_TILE), :] = sinf
        return carry

    lax.fori_loop(part * blocks, (part + 1) * blocks, rope_block, 0)


def _fill_decay_tables(r):
    row = lax.broadcasted_iota(jnp.int32, (CHUNK, CHUNK), 0).astype(F32)
    col = lax.broadcasted_iota(jnp.int32, (CHUNK, CHUNK), 1).astype(F32)
    diff = row - col
    for hh in range(B_HEADS):
        lg = LOG_GAMMA[hh]
        r.mask[hh] = jnp.where(diff >= 0, jnp.exp(lg * jnp.maximum(diff, 0.0)), 0.0)
        r.qdec[hh] = jnp.exp(lg * (row + 1.0))
        r.kdec[hh] = jnp.exp(lg * (CHUNK - 1.0 - row))


def _decode_projections(r):
    h = _rms(r.xs[...], r.pre_mix[...]).astype(BF16)
    z = _dot(h, r.w_in[...])
    cosf, sinf = _rope_rows(jnp.full((1, B_DK), PAST_LEN, F32), r.inv[...])
    for g in range(A_GROUPS):
        gc = slice(g * A_DIM, (g + 1) * A_DIM)
        ua = _gelu(z[:, g * A_DIM:(g + 1) * A_DIM])
        va = _gelu(z[:, A_WIDTH + g * A_DIM:A_WIDTH + (g + 1) * A_DIM])
        va = _center_norm(va) * r.lnw[:, gc] + r.lnb[:, gc]
        r.va[:, g, :] = va
        r.douta[:, gc] = ua * (r.ws0[:, gc] * va + r.bs0[:, gc])
    base = 2 * A_WIDTH
    for hh in range(B_HEADS):
        hc = slice(hh * B_DK, (hh + 1) * B_DK)
        r.dq[:, hc] = _rope(z[:, base + hh * B_DK:base + (hh + 1) * B_DK], cosf, sinf)
        r.dk[:, hc] = _rope(z[:, base + B_WIDTH + hh * B_DK:base + B_WIDTH + (hh + 1) * B_DK],
                            cosf, sinf) * (B_DK ** -0.5)
    r.dv[...] = z[:, base + 2 * B_WIDTH:base + 3 * B_WIDTH]
    r.dgate[...] = _silu(z[:, base + 3 * B_WIDTH:])
    r.do[...] = jnp.zeros_like(r.do)


def _decode_state_step(r):
    i = pl.program_id(0)
    per_group = DEC_GROUP // DEC_ROWS
    rows = pl.ds(pl.multiple_of((i // per_group) * DEC_GROUP, DEC_GROUP), DEC_GROUP)
    first = lax.rem(i, per_group) * DEC_ROWS
    rid = lax.broadcasted_iota(jnp.int32, (DEC_GROUP, B_DK), 0)
    live = jnp.logical_and(rid >= first, rid < first + DEC_ROWS)
    for hh in range(B_HEADS):
        hc = slice(hh * B_DK, (hh + 1) * B_DK)
        q = r.dq[rows, hc]
        k = r.dk[rows, hc]
        v = r.dv[rows, hc]
        gam = math.exp(LOG_GAMMA[hh])
        qk = jnp.sum(q * k, axis=-1, keepdims=True)
        qg = (q * gam).astype(BF16)
        v_sel = jnp.concatenate([jnp.where(rid == first + j, v, 0.0) for j in range(DEC_ROWS)],
                                axis=-1).astype(BF16)
        kv = _dot_tn(k.astype(BF16), v_sel)
        one_sel = jnp.concatenate([jnp.where(rid == first + j, 1.0, 0.0) for j in range(DEC_ROWS)],
                                  axis=-1).astype(BF16)
        q_cols = _dot_tn(qg, one_sel)
        cross = jnp.zeros((DEC_GROUP, B_DV), F32)
        for j in range(DEC_ROWS):
            s = r.s0[j, hh]
            cross_j = jnp.sum(q_cols[:, j * B_DV:(j + 1) * B_DV] * s, axis=0, keepdims=True)
            cross = jnp.where(rid == first + j, cross_j, cross)
            r.s_dec[j, hh] = s * gam + kv[:, j * B_DV:(j + 1) * B_DV]
        r.do[rows, hc] = jnp.where(live, qk * v + cross, r.do[rows, hc])


def _decode_output(r):
    parts = [r.douta[...].astype(BF16)]
    for hh in range(B_HEADS):
        hc = slice(hh * B_DV, (hh + 1) * B_DV)
        o = _center_norm(r.do[:, hc]) * r.gn[:, hc]
        parts.append((o * r.dgate[:, hc]).astype(BF16))
    mix = _dot(jnp.concatenate(parts, axis=-1), r.w_out[...])
    x1 = r.xs[...] + _rms(mix, r.post_mix[...])
    hn = _rms(x1, r.pre_mlp[...]).astype(BF16)

    def mlp_piece(j, f):
        u = jnp.maximum(_dot(hn, r.w_up[j]), 0.0)
        rows = pl.ds(pl.multiple_of(j * FF_TILE, FF_TILE), FF_TILE)
        return f + _dot((u * u).astype(BF16), r.w_down[rows, :])

    f = lax.fori_loop(0, D_FF // FF_TILE, mlp_piece, jnp.zeros_like(x1))
    r.ys[...] = x1 + _rms(f, r.post_mlp[...])


def _layer_kernel(*refs, tiles_per_seq, n_tiles):
    r = _Refs(*refs)
    i = pl.program_id(0)

    @pl.when(i == 0)
    def _():
        def zero_handoff():
            r.x1[...] = jnp.zeros_like(r.x1)
            r.hn[...] = jnp.zeros_like(r.hn)

        def decay_tables_and_handoff():
            _fill_decay_tables(r)
            zero_handoff()

        _convert_weight(r.w_in_hbm, r.w_in, meanwhile=lambda: _fill_rope_tables(r, 0, 2))
        _convert_weight(r.w_out_hbm, r.w_out, meanwhile=lambda: _fill_rope_tables(r, 1, 2))
        _convert_weight(r.w_up_hbm, r.w_up, col_pieces=D_FF // FF_TILE,
                        meanwhile=lambda: _decode_projections(r))
        _convert_weight(r.w_down_hbm, r.w_down, meanwhile=decay_tables_and_handoff)

    @pl.when(jnp.logical_and(lax.rem(i, tiles_per_seq) == 0, i < n_tiles))
    def _():
        r.s[...] = jnp.zeros_like(r.s)

    @pl.when(i < n_tiles)
    def _():
        _prompt_step(r, tiles_per_seq=tiles_per_seq, with_mixer=True)

    @pl.when(i == n_tiles)
    def _():
        _prompt_step(r, tiles_per_seq=tiles_per_seq, with_mixer=False)
        _decode_output(r)


def _prompt_step(r, *, tiles_per_seq, with_mixer):
    n_chunks = ROW_TILE // CHUNK
    chunk_rows = [slice(c * CHUNK, (c + 1) * CHUNK) for c in range(n_chunks)]
    acc = []
    hidden = []

    def mlp_piece(k):
        j = k // 2
        if k % 2 == 0:
            u = jnp.maximum(_dot(r.hn[...], r.w_up[j]), 0.0)
            hidden.append((u * u).astype(BF16))
        else:
            part = _dot(hidden.pop(), r.w_down[j * FF_TILE:(j + 1) * FF_TILE, :])
            acc[:] = [part if not acc else acc[0] + part]

    def mlp_finish():
        r.y[...] = r.x1[...] + _rms(acc[0], r.post_mlp[...])

    if not with_mixer:
        for k in range(MLP_PIECES):
            mlp_piece(k)
        mlp_finish()
        return

    x = r.x[...]
    mlp_piece(0)
    h = _rms(x, r.pre_mix[...]).astype(BF16)
    split = 2 * A_WIDTH + 2 * B_WIDTH
    z1 = _dot(h, r.w_in[:, :split])
    _decode_state_step(r)
    z2 = _dot(h, r.w_in[:, split:])
    tile_pos = pl.multiple_of(lax.rem(pl.program_id(0), tiles_per_seq) * ROW_TILE, ROW_TILE)
    cosf = r.cos[pl.ds(tile_pos, ROW_TILE), :]
    sinf = r.sin[pl.ds(tile_pos, ROW_TILE), :]

    def head_inputs(hh):
        base = 2 * A_WIDTH + hh * B_DK
        q = _rope(z1[:, base:base + B_DK], cosf, sinf)
        k = _rope(z1[:, base + B_WIDTH:base + B_WIDTH + B_DK], cosf, sinf) * (B_DK ** -0.5)
        v = z2[:, hh * B_DV:(hh + 1) * B_DV].astype(BF16)
        return {
            "q": q.astype(BF16),
            "k": k.astype(BF16),
            "v": v,
            "qd": [(q[c] * r.qdec[hh]).astype(BF16) for c in chunk_rows],
            "kd": [(k[c] * r.kdec[hh]).astype(BF16) for c in chunk_rows],
        }

    def head_dots_a(hd):
        hd["scores"] = [_dot_nt(hd["q"][c], hd["k"][c]) for c in chunk_rows]
        hd["kv"] = [_dot_tn(hd["kd"][ci], hd["v"][c]) for ci, c in enumerate(chunk_rows)]

    def head_chain(hh, hd):
        s = r.s[hh]
        hd["s"] = []
        for ci in range(n_chunks):
            hd["s"].append(s.astype(BF16))
            s = s * math.exp(LOG_GAMMA[hh] * CHUNK) + hd["kv"][ci]
        r.s[hh] = s
        hd["p"] = [(hd["scores"][ci] * r.mask[hh]).astype(BF16) for ci in range(n_chunks)]

    def head_dots_b(hd):
        hd["o"] = [_dot(hd["p"][ci], hd["v"][c]) + _dot(hd["qd"][ci], hd["s"][ci])
                   for ci, c in enumerate(chunk_rows)]

    def head_finish(hh, hd):
        base = B_WIDTH + hh * B_DV
        gate = _silu(z2[:, base:base + B_DV])
        gnw = r.gn[:, hh * B_DV:(hh + 1) * B_DV]
        for ci, c in enumerate(chunk_rows):
            o = _center_norm(hd["o"][ci]) * gnw
            r.mix[c, A_WIDTH + hh * B_DV:A_WIDTH + (hh + 1) * B_DV] = (o * gate[c]).astype(BF16)

    mlp_piece(1)
    ua, va = [], []
    for g in range(A_GROUPS):
        gc = slice(g * A_DIM, (g + 1) * A_DIM)
        ua.append(_gelu(z1[:, g * A_DIM:(g + 1) * A_DIM]))
        vg = _gelu(z1[:, A_WIDTH + g * A_DIM:A_WIDTH + (g + 1) * A_DIM])
        va.append((_center_norm(vg) * r.lnw[:, gc] + r.lnb[:, gc]).astype(BF16))
        if g == 1:
            mlp_piece(2)
    heads = [head_inputs(0)]

    row = lax.broadcasted_iota(jnp.int32, (CHUNK, CHUNK), 0)
    col = lax.broadcasted_iota(jnp.int32, (CHUNK, CHUNK), 1)
    mixed = []
    for g in range(A_GROUPS):
        w_tril = jnp.where(row >= col, r.ws[g], 0.0).astype(BF16)
        mixed.append([_dot(w_tril, va[g][c]) for c in chunk_rows])
    head_dots_a(heads[0])

    for hh in range(B_HEADS):
        mlp_piece(3 + hh)
        if hh == 0:
            for g in range(A_GROUPS):
                gc = slice(g * A_DIM, (g + 1) * A_DIM)
                bias = jnp.broadcast_to(r.bs[:, g:g + 1], (CHUNK, A_DIM))
                for ci, c in enumerate(chunk_rows):
                    r.mix[c, gc] = (ua[g][c] * (mixed[g][ci] + bias)).astype(BF16)
        else:
            head_finish(hh - 1, heads[hh - 1])
        head_chain(hh, heads[hh])
        if hh + 1 < B_HEADS:
            heads.append(head_inputs(hh + 1))
        head_dots_b(heads[hh])
        if hh + 1 < B_HEADS:
            head_dots_a(heads[hh + 1])

    mlp_piece(7)
    head_finish(B_HEADS - 1, heads[B_HEADS - 1])
    half = ROW_TILE // 2
    mix_a = _dot(r.mix[:half, :], r.w_out[...])
    mlp_finish()
    mix_b = _dot(r.mix[half:, :], r.w_out[...])
    for rows, mix in ((slice(0, half), mix_a), (slice(half, ROW_TILE), mix_b)):
        x1 = x[rows] + _rms(mix, r.post_mix[...])
        r.x1[rows, :] = x1
        r.hn[rows, :] = _rms(x1, r.pre_mlp[...]).astype(BF16)


def _resident(shape):
    zeros = (0,) * len(shape)
    return pl.BlockSpec(shape, lambda i: zeros, pipeline_mode=pl.Buffered(1))


def _layer(x, xs, s0, p, inv_freq):
    batch, seq, _ = x.shape
    n_dec = xs.shape[0]
    tiles_per_seq = seq // ROW_TILE
    n_tiles = batch * tiles_per_seq
    assert n_dec == n_tiles * DEC_ROWS and DEC_GROUP % DEC_ROWS == 0

    def mixer_tile(i):
        return jnp.minimum(i, n_tiles - 1)

    def mlp_tile(i):
        return jnp.maximum(i - 1, 0)

    x_spec = pl.BlockSpec((None, ROW_TILE, D_MODEL),
                          lambda i: (mixer_tile(i) // tiles_per_seq, mixer_tile(i) % tiles_per_seq, 0))
    y_spec = pl.BlockSpec((None, ROW_TILE, D_MODEL),
                          lambda i: (mlp_tile(i) // tiles_per_seq, mlp_tile(i) % tiles_per_seq, 0))
    tok_spec = pl.BlockSpec((n_dec, None, D_MODEL), lambda i: (0, 0, 0), pipeline_mode=pl.Buffered(1))
    ys_spec = pl.BlockSpec((n_dec, None, D_MODEL), lambda i: (0, 0, 0))
    va_spec = pl.BlockSpec((n_dec, None, A_GROUPS, A_DIM), lambda i: (0, 0, 0, 0))
    dec_state_spec = pl.BlockSpec((DEC_ROWS, B_HEADS, B_DK, B_DV), lambda i: (mixer_tile(i), 0, 0, 0))
    hbm = pl.BlockSpec(memory_space=pl.ANY)
    in_specs = [
        x_spec, tok_spec, dec_state_spec,
        _resident((1, D_MODEL)),
        _resident((1, A_WIDTH)), _resident((1, A_WIDTH)),
        _resident((A_GROUPS, CHUNK, CHUNK)), _resident((CHUNK, A_GROUPS)),
        _resident((1, A_WIDTH)), _resident((1, A_WIDTH)),
        _resident((1, B_WIDTH)),
        _resident((1, D_MODEL)), _resident((1, D_MODEL)), _resident((1, D_MODEL)),
        _resident((1, B_DK)),
        hbm, hbm, hbm, hbm,
    ]
    out_specs = [
        y_spec,
        pl.BlockSpec((None, B_HEADS, B_DK, B_DV), lambda i: (mixer_tile(i) // tiles_per_seq, 0, 0, 0)),
        ys_spec, dec_state_spec, va_spec,
    ]
    out_shape = [
        jax.ShapeDtypeStruct(x.shape, F32),
        jax.ShapeDtypeStruct((batch, B_HEADS, B_DK, B_DV), F32),
        jax.ShapeDtypeStruct((n_dec, 1, D_MODEL), F32),
        jax.ShapeDtypeStruct(s0.shape, F32),
        jax.ShapeDtypeStruct((n_dec, 1, A_GROUPS, A_DIM), F32),
    ]
    scratch_shapes = [
        pltpu.VMEM((D_MODEL, IN_COLS), BF16),
        pltpu.VMEM((D_MODEL, D_MODEL), BF16),
        pltpu.VMEM((D_FF // FF_TILE, D_MODEL, FF_TILE), BF16),
        pltpu.VMEM((D_FF, D_MODEL), BF16),
        pltpu.VMEM((ROW_TILE, D_MODEL), BF16),
        pltpu.VMEM((ROW_TILE, D_MODEL), F32),
        pltpu.VMEM((ROW_TILE, D_MODEL), BF16),
        pltpu.VMEM((seq, B_DK), F32),
        pltpu.VMEM((seq, B_DK), F32),
        pltpu.VMEM((B_HEADS, CHUNK, CHUNK), F32),
        pltpu.VMEM((B_HEADS, CHUNK, B_DK), F32),
        pltpu.VMEM((B_HEADS, CHUNK, B_DK), F32),
        pltpu.VMEM((n_dec, B_WIDTH), F32),
        pltpu.VMEM((n_dec, B_WIDTH), F32),
        pltpu.VMEM((n_dec, B_WIDTH), F32),
        pltpu.VMEM((n_dec, B_WIDTH), F32),
        pltpu.VMEM((n_dec, A_WIDTH), F32),
        pltpu.VMEM((n_dec, B_WIDTH), F32),
    ]
    return pl.pallas_call(
        functools.partial(_layer_kernel, tiles_per_seq=tiles_per_seq, n_tiles=n_tiles),
        grid=(n_tiles + 1,),
        in_specs=in_specs,
        out_specs=out_specs,
        out_shape=out_shape,
        scratch_shapes=scratch_shapes,
        compiler_params=pltpu.CompilerParams(
            dimension_semantics=("arbitrary",),
            vmem_limit_bytes=VMEM_LIMIT_BYTES),
        name="hybrid_layer",
    )(x, xs, s0, p["pre_mix_w"], p["ln_v_w"], p["ln_v_b"], p["w_s"], p["b_s_t"], p["w_s0"], p["b_s0"],
      p["gn_w"], p["post_mix_w"], p["pre_mlp_w"], p["post_mlp_w"], inv_freq,
      p["w_in"], p["w_out"], p["w_up"], p["w_down"])


def kernel(x_prompt, x_sample, state_ret, pre_mix_w, w_in, ln_v_w, ln_v_b, w_s, b_s, gn_w,
           w_out, post_mix_w, pre_mlp_w, w_up, w_down, post_mlp_w):
    depth = w_in.shape[0]
    seq = x_prompt.shape[1]
    assert x_sample.shape[1] == 1 and seq % ROW_TILE == 0

    half = B_DK // 2
    inv = ROPE_BASE ** (-jnp.arange(half, dtype=F32) / half)
    inv_freq = jnp.concatenate([inv, inv])[None, :]

    yp, ys = x_prompt, x_sample
    sp_list, ss_list, vs_list = [], [], []
    for l in range(depth):
        p = {
            "pre_mix_w": pre_mix_w[l][None, :],
            "w_in": w_in[l],
            "ln_v_w": ln_v_w[l].reshape(1, A_WIDTH),
            "ln_v_b": ln_v_b[l].reshape(1, A_WIDTH),
            "w_s": w_s[l],
            "b_s_t": b_s[l].T,
            "w_s0": jnp.repeat(w_s[l][:, 0, 0], A_DIM)[None, :],
            "b_s0": jnp.repeat(b_s[l][:, 0], A_DIM)[None, :],
            "gn_w": gn_w[l].reshape(1, B_WIDTH),
            "w_out": w_out[l],
            "post_mix_w": post_mix_w[l][None, :],
            "pre_mlp_w": pre_mlp_w[l][None, :],
            "w_up": w_up[l],
            "w_down": w_down[l],
            "post_mlp_w": post_mlp_w[l][None, :],
        }
        yp, sp, ys, ss, vs = _layer(yp, ys, state_ret[l], p, inv_freq)
        sp_list.append(sp)
        ss_list.append(ss)
        vs_list.append(vs)
    return (yp, ys, jnp.stack(sp_list, axis=0), jnp.stack(ss_list, axis=0), jnp.stack(vs_list, axis=0))
```

```python
import functools
import math
from typing import Any, NamedTuple

import jax
import jax.numpy as jnp
from jax import lax
from jax.experimental import pallas as pl
from jax.experimental.pallas import tpu as pltpu

D_MODEL = 1024
A_WIDTH = 512
B_WIDTH = 512
A_GROUPS = 4
A_DIM = 128
CHUNK = 128
B_HEADS = 4
B_DK = 128
B_DV = 128
D_FF = 4096
IN_COLS = 2 * A_WIDTH + 4 * B_WIDTH
ROPE_BASE = 10000.0
EPS = 1e-6
PAST_LEN = 16384
LOG_GAMMA = [math.log(1.0 - 2.0 ** (-5.0 - h)) for h in range(B_HEADS)]

ROW_TILE = 512
FF_TILE = 1024
MLP_PIECES = 2 * (D_FF // FF_TILE)
DEC_ROWS = 4
DEC_GROUP = 16
STAGE_BYTES = 1 << 20
STAGE_SLOTS = 6
VMEM_LIMIT_BYTES = 60 * 1024 * 1024

F32 = jnp.float32
BF16 = jnp.bfloat16


def _rms(x, w):
    return x * lax.rsqrt(jnp.mean(x * x, axis=-1, keepdims=True) + EPS) * w


def _center_norm(x):
    mu = jnp.mean(x, axis=-1, keepdims=True)
    xc = x - mu
    return xc * lax.rsqrt(jnp.mean(xc * xc, axis=-1, keepdims=True) + EPS)


def _gelu(x):
    return jax.nn.gelu(x, approximate=True)


def _silu(x):
    return x * (1.0 / (1.0 + jnp.exp(-x)))


def _rope(x, cosf, sinf):
    return x * cosf + pltpu.roll(x, B_DK // 2, axis=1) * sinf


def _rope_rows(pos, inv):
    ang = pos * inv
    lane = lax.broadcasted_iota(jnp.int32, ang.shape, 1)
    sin = jnp.sin(ang)
    return jnp.cos(ang), jnp.where(lane < B_DK // 2, -sin, sin)


def _dot(a, b):
    return jnp.dot(a, b, preferred_element_type=F32)


def _dot_nt(a, b):
    return lax.dot_general(a, b, (((1,), (1,)), ((), ())), preferred_element_type=F32)


def _dot_tn(a, b):
    return lax.dot_general(a, b, (((0,), (0,)), ((), ())), preferred_element_type=F32)


class _Refs(NamedTuple):
    x: Any
    xs: Any
    s0: Any
    pre_mix: Any
    lnw: Any
    lnb: Any
    ws: Any
    bs: Any
    ws0: Any
    bs0: Any
    gn: Any
    post_mix: Any
    pre_mlp: Any
    post_mlp: Any
    inv: Any
    w_in_hbm: Any
    w_out_hbm: Any
    w_up_hbm: Any
    w_down_hbm: Any
    y: Any
    s: Any
    ys: Any
    s_dec: Any
    va: Any
    w_in: Any
    w_out: Any
    w_up: Any
    w_down: Any
    mix: Any
    x1: Any
    hn: Any
    cos: Any
    sin: Any
    mask: Any
    qdec: Any
    kdec: Any
    dq: Any
    dk: Any
    dv: Any
    dgate: Any
    douta: Any
    do: Any


def _convert_weight(w_hbm, w_bf, col_pieces=1, meanwhile=None):
    n_rows, n_cols = w_hbm.shape
    piece = n_cols // col_pieces
    rows_per = 1 << int(math.log2(STAGE_BYTES // (4 * n_cols)))
    assert n_rows % rows_per == 0 and rows_per % 16 == 0
    n_chunks = n_rows // rows_per

    ahead = STAGE_SLOTS - 1

    def body(stage, sem):
        def copy(c):
            slot = lax.rem(c, STAGE_SLOTS)
            return pltpu.make_async_copy(
                w_hbm.at[pl.ds(c * rows_per, rows_per), :], stage.at[slot], sem.at[slot])

        for c in range(min(ahead, n_chunks)):
            copy(c).start()
        if meanwhile is not None:
            meanwhile()

        def step(c, carry):
            slot = lax.rem(c, STAGE_SLOTS)

            @pl.when(c + ahead < n_chunks)
            def _():
                copy(c + ahead).start()

            copy(c).wait()
            rows = pl.ds(pl.multiple_of(c * rows_per, rows_per), rows_per)
            if col_pieces == 1:
                w_bf[rows, :] = stage[slot].astype(BF16)
            else:
                for p in range(col_pieces):
                    w_bf[p, rows, :] = stage[slot, :, p * piece:(p + 1) * piece].astype(BF16)
            return carry

        lax.fori_loop(0, n_chunks, step, 0)

    pl.run_scoped(body, pltpu.VMEM((STAGE_SLOTS, rows_per, n_cols), F32),
                  pltpu.SemaphoreType.DMA((STAGE_SLOTS,)))


def _fill_rope_tables(r, part, n_parts):
    blocks = r.cos.shape[0] // ROW_TILE // n_parts

    def rope_block(b, carry):
        r0 = pl.multiple_of(b * ROW_TILE, ROW_TILE)
        pos = (lax.broadcasted_iota(jnp.int32, (ROW_TILE, B_DK), 0) + r0).astype(F32)
        cosf, sinf = _rope_rows(pos, r.inv[...])
        r.cos[pl.ds(r0, ROW_TILE), :] = cosf
        r.sin[pl.ds(r0, ROW_TILE), :] = sinf
        return carry

    lax.fori_loop(part * blocks, (part + 1) * blocks, rope_block, 0)


def _fill_decay_tables(r):
    row = lax.broadcasted_iota(jnp.int32, (CHUNK, CHUNK), 0).astype(F32)
    col = lax.broadcasted_iota(jnp.int32, (CHUNK, CHUNK), 1).astype(F32)
    diff = row - col
    for hh in range(B_HEADS):
        lg = LOG_GAMMA[hh]
        r.mask[hh] = jnp.where(diff >= 0, jnp.exp(lg * jnp.maximum(diff, 0.0)), 0.0)
        r.qdec[hh] = jnp.exp(lg * (row + 1.0))
        r.kdec[hh] = jnp.exp(lg * (CHUNK - 1.0 - row))


def _decode_projections(r):
    h = _rms(r.xs[...], r.pre_mix[...]).astype(BF16)
    z = _dot(h, r.w_in[...])
    cosf, sinf = _rope_rows(jnp.full((1, B_DK), PAST_LEN, F32), r.inv[...])
    for g in range(A_GROUPS):
        gc = slice(g * A_DIM, (g + 1) * A_DIM)
        ua = _gelu(z[:, g * A_DIM:(g + 1) * A_DIM])
        va = _gelu(z[:, A_WIDTH + g * A_DIM:A_WIDTH + (g + 1) * A_DIM])
        va = _center_norm(va) * r.lnw[:, gc] + r.lnb[:, gc]
        r.va[:, g, :] = va
        r.douta[:, gc] = ua * (r.ws0[:, gc] * va + r.bs0[:, gc])
    base = 2 * A_WIDTH
    for hh in range(B_HEADS):
        hc = slice(hh * B_DK, (hh + 1) * B_DK)
        r.dq[:, hc] = _rope(z[:, base + hh * B_DK:base + (hh + 1) * B_DK], cosf, sinf)
        r.dk[:, hc] = _rope(z[:, base + B_WIDTH + hh * B_DK:base + B_WIDTH + (hh + 1) * B_DK],
                            cosf, sinf) * (B_DK ** -0.5)
    r.dv[...] = z[:, base + 2 * B_WIDTH:base + 3 * B_WIDTH]
    r.dgate[...] = _silu(z[:, base + 3 * B_WIDTH:])
    r.do[...] = jnp.zeros_like(r.do)


def _decode_state_step(r):
    i = pl.program_id(0)
    per_group = DEC_GROUP // DEC_ROWS
    rows = pl.ds(pl.multiple_of((i // per_group) * DEC_GROUP, DEC_GROUP), DEC_GROUP)
    first = lax.rem(i, per_group) * DEC_ROWS
    rid = lax.broadcasted_iota(jnp.int32, (DEC_GROUP, B_DK), 0)
    live = jnp.logical_and(rid >= first, rid < first + DEC_ROWS)
    for hh in range(B_HEADS):
        hc = slice(hh * B_DK, (hh + 1) * B_DK)
        q = r.dq[rows, hc]
        k = r.dk[rows, hc]
        v = r.dv[rows, hc]
        gam = math.exp(LOG_GAMMA[hh])
        qk = jnp.sum(q * k, axis=-1, keepdims=True)
        qg = (q * gam).astype(BF16)
        v_sel = jnp.concatenate([jnp.where(rid == first + j, v, 0.0) for j in range(DEC_ROWS)],
                                axis=-1).astype(BF16)
        kv = _dot_tn(k.astype(BF16), v_sel)
        one_sel = jnp.concatenate([jnp.where(rid == first + j, 1.0, 0.0) for j in range(DEC_ROWS)],
                                  axis=-1).astype(BF16)
        q_cols = _dot_tn(qg, one_sel)
        cross = jnp.zeros((DEC_GROUP, B_DV), F32)
        for j in range(DEC_ROWS):
            s = r.s0[j, hh]
            cross_j = jnp.sum(q_cols[:, j * B_DV:(j + 1) * B_DV] * s, axis=0, keepdims=True)
            cross = jnp.where(rid == first + j, cross_j, cross)
            r.s_dec[j, hh] = s * gam + kv[:, j * B_DV:(j + 1) * B_DV]
        r.do[rows, hc] = jnp.where(live, qk * v + cross, r.do[rows, hc])


def _decode_output(r):
    parts = [r.douta[...].astype(BF16)]
    for hh in range(B_HEADS):
        hc = slice(hh * B_DV, (hh + 1) * B_DV)
        o = _center_norm(r.do[:, hc]) * r.gn[:, hc]
        parts.append((o * r.dgate[:, hc]).astype(BF16))
    mix = _dot(jnp.concatenate(parts, axis=-1), r.w_out[...])
    x1 = r.xs[...] + _rms(mix, r.post_mix[...])
    hn = _rms(x1, r.pre_mlp[...]).astype(BF16)

    def mlp_piece(j, f):
        u = jnp.maximum(_dot(hn, r.w_up[j]), 0.0)
        rows = pl.ds(pl.multiple_of(j * FF_TILE, FF_TILE), FF_TILE)
        return f + _dot((u * u).astype(BF16), r.w_down[rows, :])

    f = lax.fori_loop(0, D_FF // FF_TILE, mlp_piece, jnp.zeros_like(x1))
    r.ys[...] = x1 + _rms(f, r.post_mlp[...])


def _layer_kernel(*refs, tiles_per_seq, n_tiles):
    r = _Refs(*refs)
    i = pl.program_id(0)

    @pl.when(i == 0)
    def _():
        _convert_weight(r.w_in_hbm, r.w_in, meanwhile=lambda: _fill_rope_tables(r, 0, 2))
        _convert_weight(r.w_out_hbm, r.w_out, meanwhile=lambda: _fill_rope_tables(r, 1, 2))
        _convert_weight(r.w_up_hbm, r.w_up, col_pieces=D_FF // FF_TILE,
                        meanwhile=lambda: _decode_projections(r))
        _convert_weight(r.w_down_hbm, r.w_down, meanwhile=lambda: _fill_decay_tables(r))

    @pl.when(jnp.logical_and(lax.rem(i, tiles_per_seq) == 0, i < n_tiles))
    def _():
        r.s[...] = jnp.zeros_like(r.s)

    @pl.when(i == 0)
    def _():
        _prompt_step(r, tiles_per_seq=tiles_per_seq, with_mixer=True, with_mlp=False)

    @pl.when(jnp.logical_and(i > 0, i < n_tiles))
    def _():
        _prompt_step(r, tiles_per_seq=tiles_per_seq, with_mixer=True, with_mlp=True)

    @pl.when(i == n_tiles)
    def _():
        _prompt_step(r, tiles_per_seq=tiles_per_seq, with_mixer=False, with_mlp=True)
        _decode_output(r)


def _prompt_step(r, *, tiles_per_seq, with_mixer, with_mlp):
    n_chunks = ROW_TILE // CHUNK
    chunk_rows = [slice(c * CHUNK, (c + 1) * CHUNK) for c in range(n_chunks)]
    acc = []
    hidden = []

    def mlp_piece(k):
        if not with_mlp:
            return
        j = k // 2
        if k % 2 == 0:
            u = jnp.maximum(_dot(r.hn[...], r.w_up[j]), 0.0)
            hidden.append((u * u).astype(BF16))
        else:
            part = _dot(hidden.pop(), r.w_down[j * FF_TILE:(j + 1) * FF_TILE, :])
            acc[:] = [part if not acc else acc[0] + part]

    def mlp_finish():
        if with_mlp:
            r.y[...] = r.x1[...] + _rms(acc[0], r.post_mlp[...])

    if not with_mixer:
        def block(j, f):
            u = jnp.maximum(_dot(r.hn[...], r.w_up[j]), 0.0)
            rows = pl.ds(pl.multiple_of(j * FF_TILE, FF_TILE), FF_TILE)
            return f + _dot((u * u).astype(BF16), r.w_down[rows, :])

        acc.append(lax.fori_loop(0, D_FF // FF_TILE, block, jnp.zeros((ROW_TILE, D_MODEL), F32)))
        mlp_finish()
        return

    x = r.x[...]
    mlp_piece(0)
    h = _rms(x, r.pre_mix[...]).astype(BF16)
    split = 2 * A_WIDTH + 2 * B_WIDTH
    z1 = _dot(h, r.w_in[:, :split])
    _decode_state_step(r)
    z2 = _dot(h, r.w_in[:, split:])
    tile_pos = pl.multiple_of(lax.rem(pl.program_id(0), tiles_per_seq) * ROW_TILE, ROW_TILE)
    cosf = r.cos[pl.ds(tile_pos, ROW_TILE), :]
    sinf = r.sin[pl.ds(tile_pos, ROW_TILE), :]

    def head_inputs(hh):
        base = 2 * A_WIDTH + hh * B_DK
        q = _rope(z1[:, base:base + B_DK], cosf, sinf)
        k = _rope(z1[:, base + B_WIDTH:base + B_WIDTH + B_DK], cosf, sinf) * (B_DK ** -0.5)
        v = z2[:, hh * B_DV:(hh + 1) * B_DV].astype(BF16)
        return {
            "q": q.astype(BF16),
            "k": k.astype(BF16),
            "v": v,
            "qd": [(q[c] * r.qdec[hh]).astype(BF16) for c in chunk_rows],
            "kd": [(k[c] * r.kdec[hh]).astype(BF16) for c in chunk_rows],
        }

    def head_dots_a(hd):
        hd["scores"] = [_dot_nt(hd["q"][c], hd["k"][c]) for c in chunk_rows]
        hd["kv"] = [_dot_tn(hd["kd"][ci], hd["v"][c]) for ci, c in enumerate(chunk_rows)]

    def head_chain(hh, hd):
        s = r.s[hh]
        hd["s"] = []
        for ci in range(n_chunks):
            hd["s"].append(s.astype(BF16))
            s = s * math.exp(LOG_GAMMA[hh] * CHUNK) + hd["kv"][ci]
        r.s[hh] = s
        hd["p"] = [(hd["scores"][ci] * r.mask[hh]).astype(BF16) for ci in range(n_chunks)]

    def head_dots_b(hd):
        hd["o"] = [_dot(hd["p"][ci], hd["v"][c]) + _dot(hd["qd"][ci], hd["s"][ci])
                   for ci, c in enumerate(chunk_rows)]

    def head_finish(hh, hd):
        base = B_WIDTH + hh * B_DV
        gate = _silu(z2[:, base:base + B_DV])
        gnw = r.gn[:, hh * B_DV:(hh + 1) * B_DV]
        for ci, c in enumerate(chunk_rows):
            o = _center_norm(hd["o"][ci]) * gnw
            r.mix[c, A_WIDTH + hh * B_DV:A_WIDTH + (hh + 1) * B_DV] = (o * gate[c]).astype(BF16)

    mlp_piece(1)
    ua, va = [], []
    for g in range(A_GROUPS):
        gc = slice(g * A_DIM, (g + 1) * A_DIM)
        ua.append(_gelu(z1[:, g * A_DIM:(g + 1) * A_DIM]))
        vg = _gelu(z1[:, A_WIDTH + g * A_DIM:A_WIDTH + (g + 1) * A_DIM])
        va.append((_center_norm(vg) * r.lnw[:, gc] + r.lnb[:, gc]).astype(BF16))
        if g == 1:
            mlp_piece(2)
    heads = [head_inputs(0)]

    row = lax.broadcasted_iota(jnp.int32, (CHUNK, CHUNK), 0)
    col = lax.broadcasted_iota(jnp.int32, (CHUNK, CHUNK), 1)
    mixed = []
    for g in range(A_GROUPS):
        w_tril = jnp.where(row >= col, r.ws[g], 0.0).astype(BF16)
        mixed.append([_dot(w_tril, va[g][c]) for c in chunk_rows])
    head_dots_a(heads[0])

    for hh in range(B_HEADS):
        mlp_piece(3 + hh)
        if hh == 0:
            for g in range(A_GROUPS):
                gc = slice(g * A_DIM, (g + 1) * A_DIM)
                bias = jnp.broadcast_to(r.bs[:, g:g + 1], (CHUNK, A_DIM))
                for ci, c in enumerate(chunk_rows):
                    r.mix[c, gc] = (ua[g][c] * (mixed[g][ci] + bias)).astype(BF16)
        else:
            head_finish(hh - 1, heads[hh - 1])
        head_chain(hh, heads[hh])
        if hh + 1 < B_HEADS:
            heads.append(head_inputs(hh + 1))
        head_dots_b(heads[hh])
        if hh + 1 < B_HEADS:
            head_dots_a(heads[hh + 1])

    mlp_piece(7)
    head_finish(B_HEADS - 1, heads[B_HEADS - 1])
    half = ROW_TILE // 2
    mix_a = _dot(r.mix[:half, :], r.w_out[...])
    mlp_finish()
    mix_b = _dot(r.mix[half:, :], r.w_out[...])
    for rows, mix in ((slice(0, half), mix_a), (slice(half, ROW_TILE), mix_b)):
        x1 = x[rows] + _rms(mix, r.post_mix[...])
        r.x1[rows, :] = x1
        r.hn[rows, :] = _rms(x1, r.pre_mlp[...]).astype(BF16)


def _resident(shape):
    zeros = (0,) * len(shape)
    return pl.BlockSpec(shape, lambda i: zeros, pipeline_mode=pl.Buffered(1))


def _layer(x, xs, s0, p, inv_freq):
    batch, seq, _ = x.shape
    n_dec = xs.shape[0]
    tiles_per_seq = seq // ROW_TILE
    n_tiles = batch * tiles_per_seq
    assert n_dec == n_tiles * DEC_ROWS and DEC_GROUP % DEC_ROWS == 0

    def mixer_tile(i):
        return jnp.minimum(i, n_tiles - 1)

    def mlp_tile(i):
        return jnp.maximum(i - 1, 0)

    x_spec = pl.BlockSpec((None, ROW_TILE, D_MODEL),
                          lambda i: (mixer_tile(i) // tiles_per_seq, mixer_tile(i) % tiles_per_seq, 0))
    y_spec = pl.BlockSpec((None, ROW_TILE, D_MODEL),
                          lambda i: (mlp_tile(i) // tiles_per_seq, mlp_tile(i) % tiles_per_seq, 0))
    tok_spec = pl.BlockSpec((n_dec, None, D_MODEL), lambda i: (0, 0, 0), pipeline_mode=pl.Buffered(1))
    ys_spec = pl.BlockSpec((n_dec, None, D_MODEL), lambda i: (0, 0, 0))
    va_spec = pl.BlockSpec((n_dec, None, A_GROUPS, A_DIM), lambda i: (0, 0, 0, 0))
    dec_state_spec = pl.BlockSpec((DEC_ROWS, B_HEADS, B_DK, B_DV), lambda i: (mixer_tile(i), 0, 0, 0))
    hbm = pl.BlockSpec(memory_space=pl.ANY)
    in_specs = [
        x_spec, tok_spec, dec_state_spec,
        _resident((1, D_MODEL)),
        _resident((1, A_WIDTH)), _resident((1, A_WIDTH)),
        _resident((A_GROUPS, CHUNK, CHUNK)), _resident((CHUNK, A_GROUPS)),
        _resident((1, A_WIDTH)), _resident((1, A_WIDTH)),
        _resident((1, B_WIDTH)),
        _resident((1, D_MODEL)), _resident((1, D_MODEL)), _resident((1, D_MODEL)),
        _resident((1, B_DK)),
        hbm, hbm, hbm, hbm,
    ]
    out_specs = [
        y_spec,
        pl.BlockSpec((None, B_HEADS, B_DK, B_DV), lambda i: (mixer_tile(i) // tiles_per_seq, 0, 0, 0)),
        ys_spec, dec_state_spec, va_spec,
    ]
    out_shape = [
        jax.ShapeDtypeStruct(x.shape, F32),
        jax.ShapeDtypeStruct((batch, B_HEADS, B_DK, B_DV), F32),
        jax.ShapeDtypeStruct((n_dec, 1, D_MODEL), F32),
        jax.ShapeDtypeStruct(s0.shape, F32),
        jax.ShapeDtypeStruct((n_dec, 1, A_GROUPS, A_DIM), F32),
    ]
    scratch_shapes = [
        pltpu.VMEM((D_MODEL, IN_COLS), BF16),
        pltpu.VMEM((D_MODEL, D_MODEL), BF16),
        pltpu.VMEM((D_FF // FF_TILE, D_MODEL, FF_TILE), BF16),
        pltpu.VMEM((D_FF, D_MODEL), BF16),
        pltpu.VMEM((ROW_TILE, D_MODEL), BF16),
        pltpu.VMEM((ROW_TILE, D_MODEL), F32),
        pltpu.VMEM((ROW_TILE, D_MODEL), BF16),
        pltpu.VMEM((seq, B_DK), F32),
        pltpu.VMEM((seq, B_DK), F32),
        pltpu.VMEM((B_HEADS, CHUNK, CHUNK), F32),
        pltpu.VMEM((B_HEADS, CHUNK, B_DK), F32),
        pltpu.VMEM((B_HEADS, CHUNK, B_DK), F32),
        pltpu.VMEM((n_dec, B_WIDTH), F32),
        pltpu.VMEM((n_dec, B_WIDTH), F32),
        pltpu.VMEM((n_dec, B_WIDTH), F32),
        pltpu.VMEM((n_dec, B_WIDTH), F32),
        pltpu.VMEM((n_dec, A_WIDTH), F32),
        pltpu.VMEM((n_dec, B_WIDTH), F32),
    ]
    return pl.pallas_call(
        functools.partial(_layer_kernel, tiles_per_seq=tiles_per_seq, n_tiles=n_tiles),
        grid=(n_tiles + 1,),
        in_specs=in_specs,
        out_specs=out_specs,
        out_shape=out_shape,
        scratch_shapes=scratch_shapes,
        compiler_params=pltpu.CompilerParams(
            dimension_semantics=("arbitrary",),
            vmem_limit_bytes=VMEM_LIMIT_BYTES),
        name="hybrid_layer",
    )(x, xs, s0, p["pre_mix_w"], p["ln_v_w"], p["ln_v_b"], p["w_s"], p["b_s_t"], p["w_s0"], p["b_s0"],
      p["gn_w"], p["post_mix_w"], p["pre_mlp_w"], p["post_mlp_w"], inv_freq,
      p["w_in"], p["w_out"], p["w_up"], p["w_down"])


def kernel(x_prompt, x_sample, state_ret, pre_mix_w, w_in, ln_v_w, ln_v_b, w_s, b_s, gn_w,
           w_out, post_mix_w, pre_mlp_w, w_up, w_down, post_mlp_w):
    depth = w_in.shape[0]
    seq = x_prompt.shape[1]
    assert x_sample.shape[1] == 1 and seq % ROW_TILE == 0

    half = B_DK // 2
    inv = ROPE_BASE ** (-jnp.arange(half, dtype=F32) / half)
    inv_freq = jnp.concatenate([inv, inv])[None, :]

    yp, ys = x_prompt, x_sample
    sp_list, ss_list, vs_list = [], [], []
    for l in range(depth):
        p = {
            "pre_mix_w": pre_mix_w[l][None, :],
            "w_in": w_in[l],
            "ln_v_w": ln_v_w[l].reshape(1, A_WIDTH),
            "ln_v_b": ln_v_b[l].reshape(1, A_WIDTH),
            "w_s": w_s[l],
            "b_s_t": b_s[l].T,
            "w_s0": jnp.repeat(w_s[l][:, 0, 0], A_DIM)[None, :],
            "b_s0": jnp.repeat(b_s[l][:, 0], A_DIM)[None, :],
            "gn_w": gn_w[l].reshape(1, B_WIDTH),
            "w_out": w_out[l],
            "post_mix_w": post_mix_w[l][None, :],
            "pre_mlp_w": pre_mlp_w[l][None, :],
            "w_up": w_up[l],
            "w_down": w_down[l],
            "post_mlp_w": post_mlp_w[l][None, :],
        }
        yp, sp, ys, ss, vs = _layer(yp, ys, state_ret[l], p, inv_freq)
        sp_list.append(sp)
        ss_list.append(ss)
        vs_list.append(vs)
    return (yp, ys, jnp.stack(sp_list, axis=0), jnp.stack(ss_list, axis=0), jnp.stack(vs_list, axis=0))
```

```python
import functools
import math
from typing import Any, NamedTuple

import jax
import jax.numpy as jnp
from jax import lax
from jax.experimental import pallas as pl
from jax.experimental.pallas import tpu as pltpu

D_MODEL = 1024
A_WIDTH = 512
B_WIDTH = 512
A_GROUPS = 4
A_DIM = 128
CHUNK = 128
B_HEADS = 4
B_DK = 128
B_DV = 128
D_FF = 4096
IN_COLS = 2 * A_WIDTH + 4 * B_WIDTH
ROPE_BASE = 10000.0
EPS = 1e-6
PAST_LEN = 16384
LOG_GAMMA = [math.log(1.0 - 2.0 ** (-5.0 - h)) for h in range(B_HEADS)]

ROW_TILE = 512
FF_TILE = 1024
MLP_PIECES = 2 * (D_FF // FF_TILE)
DEC_ROWS = 4
DEC_GROUP = 16
STAGE_BYTES = 1 << 20
STAGE_SLOTS = 6
VMEM_LIMIT_BYTES = 60 * 1024 * 1024

F32 = jnp.float32
BF16 = jnp.bfloat16


def _rms(x, w):
    return x * lax.rsqrt(jnp.mean(x * x, axis=-1, keepdims=True) + EPS) * w


def _center_norm(x):
    mu = jnp.mean(x, axis=-1, keepdims=True)
    xc = x - mu
    return xc * lax.rsqrt(jnp.mean(xc * xc, axis=-1, keepdims=True) + EPS)


def _gelu(x):
    return jax.nn.gelu(x, approximate=True)


def _silu(x):
    return x * (1.0 / (1.0 + jnp.exp(-x)))


def _rope(x, cosf, sinf):
    return x * cosf + pltpu.roll(x, B_DK // 2, axis=1) * sinf


def _rope_rows(pos):
    lane = lax.broadcasted_iota(jnp.int32, pos.shape, 1)
    j = lax.rem(lane, B_DK // 2).astype(F32)
    ang = pos * jnp.exp(j * (-math.log(ROPE_BASE) / (B_DK // 2)))
    sin = jnp.sin(ang)
    return jnp.cos(ang), jnp.where(lane < B_DK // 2, -sin, sin)


def _dot(a, b):
    return jnp.dot(a, b, preferred_element_type=F32)


def _dot_nt(a, b):
    return lax.dot_general(a, b, (((1,), (1,)), ((), ())), preferred_element_type=F32)


def _dot_tn(a, b):
    return lax.dot_general(a, b, (((0,), (0,)), ((), ())), preferred_element_type=F32)


class _Refs(NamedTuple):
    x: Any
    xs: Any
    s0: Any
    pre_mix: Any
    lnw: Any
    lnb: Any
    ws: Any
    bs: Any
    bsr: Any
    gn: Any
    post_mix: Any
    pre_mlp: Any
    post_mlp: Any
    w_in_hbm: Any
    w_out_hbm: Any
    w_up_hbm: Any
    w_down_hbm: Any
    y: Any
    s: Any
    ys: Any
    s_dec: Any
    va: Any
    w_in: Any
    w_out: Any
    w_up: Any
    w_down: Any
    mix: Any
    x1: Any
    hn: Any
    cos: Any
    sin: Any
    mask: Any
    qdec: Any
    kdec: Any
    dq: Any
    dk: Any
    dv: Any
    dgate: Any
    douta: Any
    do: Any


def _convert_weight(w_hbm, w_bf, col_pieces=1, meanwhile=None):
    n_rows, n_cols = w_hbm.shape
    piece = n_cols // col_pieces
    rows_per = 1 << int(math.log2(STAGE_BYTES // (4 * n_cols)))
    assert n_rows % rows_per == 0 and rows_per % 16 == 0
    n_chunks = n_rows // rows_per

    ahead = STAGE_SLOTS - 1

    def body(stage, sem):
        def copy(c):
            slot = lax.rem(c, STAGE_SLOTS)
            return pltpu.make_async_copy(
                w_hbm.at[pl.ds(c * rows_per, rows_per), :], stage.at[slot], sem.at[slot])

        for c in range(min(ahead, n_chunks)):
            copy(c).start()
        if meanwhile is not None:
            meanwhile()

        def step(c, carry):
            slot = lax.rem(c, STAGE_SLOTS)

            @pl.when(c + ahead < n_chunks)
            def _():
                copy(c + ahead).start()

            copy(c).wait()
            rows = pl.ds(pl.multiple_of(c * rows_per, rows_per), rows_per)
            if col_pieces == 1:
                w_bf[rows, :] = stage[slot].astype(BF16)
            else:
                for p in range(col_pieces):
                    w_bf[p, rows, :] = stage[slot, :, p * piece:(p + 1) * piece].astype(BF16)
            return carry

        lax.fori_loop(0, n_chunks, step, 0)

    pl.run_scoped(body, pltpu.VMEM((STAGE_SLOTS, rows_per, n_cols), F32),
                  pltpu.SemaphoreType.DMA((STAGE_SLOTS,)))


def _fill_rope_tables(r, part, n_parts):
    blocks = r.cos.shape[0] // ROW_TILE // n_parts

    def rope_block(b, carry):
        r0 = pl.multiple_of(b * ROW_TILE, ROW_TILE)
        pos = (lax.broadcasted_iota(jnp.int32, (ROW_TILE, B_DK), 0) + r0).astype(F32)
        cosf, sinf = _rope_rows(pos)
        r.cos[pl.ds(r0, ROW_TILE), :] = cosf
        r.sin[pl.ds(r0, ROW_TILE), :] = sinf
        return carry

    lax.fori_loop(part * blocks, (part + 1) * blocks, rope_block, 0)


def _fill_decay_tables(r):
    row = lax.broadcasted_iota(jnp.int32, (CHUNK, CHUNK), 0).astype(F32)
    col = lax.broadcasted_iota(jnp.int32, (CHUNK, CHUNK), 1).astype(F32)
    diff = row - col
    for hh in range(B_HEADS):
        lg = LOG_GAMMA[hh]
        r.mask[hh] = jnp.where(diff >= 0, jnp.exp(lg * jnp.maximum(diff, 0.0)), 0.0)
        r.qdec[hh] = jnp.exp(lg * (row + 1.0))
        r.kdec[hh] = jnp.exp(lg * (CHUNK - 1.0 - row))


def _decode_projections(r):
    h = _rms(r.xs[...], r.pre_mix[...]).astype(BF16)
    z = _dot(h, r.w_in[...])
    cosf, sinf = _rope_rows(jnp.full((1, B_DK), PAST_LEN, F32))
    for g in range(A_GROUPS):
        gc = slice(g * A_DIM, (g + 1) * A_DIM)
        ua = _gelu(z[:, g * A_DIM:(g + 1) * A_DIM])
        va = _gelu(z[:, A_WIDTH + g * A_DIM:A_WIDTH + (g + 1) * A_DIM])
        va = _center_norm(va) * r.lnw[:, gc] + r.lnb[:, gc]
        r.va[:, g, :] = va
        r.douta[:, gc] = ua * (r.ws[g][0:1, 0:1] * va + r.bsr[g:g + 1, 0:1])
    base = 2 * A_WIDTH
    for hh in range(B_HEADS):
        hc = slice(hh * B_DK, (hh + 1) * B_DK)
        r.dq[:, hc] = _rope(z[:, base + hh * B_DK:base + (hh + 1) * B_DK], cosf, sinf)
        r.dk[:, hc] = _rope(z[:, base + B_WIDTH + hh * B_DK:base + B_WIDTH + (hh + 1) * B_DK],
                            cosf, sinf) * (B_DK ** -0.5)
    r.dv[...] = z[:, base + 2 * B_WIDTH:base + 3 * B_WIDTH]
    r.dgate[...] = _silu(z[:, base + 3 * B_WIDTH:])
    r.do[...] = jnp.zeros_like(r.do)


def _decode_state_step(r):
    i = pl.program_id(0)
    per_group = DEC_GROUP // DEC_ROWS
    rows = pl.ds(pl.multiple_of((i // per_group) * DEC_GROUP, DEC_GROUP), DEC_GROUP)
    first = lax.rem(i, per_group) * DEC_ROWS
    rid = lax.broadcasted_iota(jnp.int32, (DEC_GROUP, B_DK), 0)
    live = jnp.logical_and(rid >= first, rid < first + DEC_ROWS)
    for hh in range(B_HEADS):
        hc = slice(hh * B_DK, (hh + 1) * B_DK)
        q = r.dq[rows, hc]
        k = r.dk[rows, hc]
        v = r.dv[rows, hc]
        gam = math.exp(LOG_GAMMA[hh])
        qk = jnp.sum(q * k, axis=-1, keepdims=True)
        qg = (q * gam).astype(BF16)
        v_sel = jnp.concatenate([jnp.where(rid == first + j, v, 0.0) for j in range(DEC_ROWS)],
                                axis=-1).astype(BF16)
        kv = _dot_tn(k.astype(BF16), v_sel)
        one_sel = jnp.concatenate([jnp.where(rid == first + j, 1.0, 0.0) for j in range(DEC_ROWS)],
                                  axis=-1).astype(BF16)
        q_cols = _dot_tn(qg, one_sel)
        cross = jnp.zeros((DEC_GROUP, B_DV), F32)
        for j in range(DEC_ROWS):
            s = r.s0[j, hh]
            cross_j = jnp.sum(q_cols[:, j * B_DV:(j + 1) * B_DV] * s, axis=0, keepdims=True)
            cross = jnp.where(rid == first + j, cross_j, cross)
            r.s_dec[j, hh] = s * gam + kv[:, j * B_DV:(j + 1) * B_DV]
        r.do[rows, hc] = jnp.where(live, qk * v + cross, r.do[rows, hc])


def _decode_output(r):
    parts = [r.douta[...].astype(BF16)]
    for hh in range(B_HEADS):
        hc = slice(hh * B_DV, (hh + 1) * B_DV)
        o = _center_norm(r.do[:, hc]) * r.gn[:, hc]
        parts.append((o * r.dgate[:, hc]).astype(BF16))
    mix = _dot(jnp.concatenate(parts, axis=-1), r.w_out[...])
    x1 = r.xs[...] + _rms(mix, r.post_mix[...])
    hn = _rms(x1, r.pre_mlp[...]).astype(BF16)

    def mlp_piece(j, f):
        u = jnp.maximum(_dot(hn, r.w_up[j]), 0.0)
        rows = pl.ds(pl.multiple_of(j * FF_TILE, FF_TILE), FF_TILE)
        return f + _dot((u * u).astype(BF16), r.w_down[rows, :])

    f = lax.fori_loop(0, D_FF // FF_TILE, mlp_piece, jnp.zeros_like(x1))
    r.ys[...] = x1 + _rms(f, r.post_mlp[...])


def _layer_kernel(*refs, tiles_per_seq, n_tiles):
    r = _Refs(*refs)
    i = pl.program_id(0)

    @pl.when(i == 0)
    def _():
        _convert_weight(r.w_in_hbm, r.w_in, meanwhile=lambda: _fill_rope_tables(r, 0, 2))
        _convert_weight(r.w_out_hbm, r.w_out, meanwhile=lambda: _fill_rope_tables(r, 1, 2))
        _convert_weight(r.w_up_hbm, r.w_up, col_pieces=D_FF // FF_TILE,
                        meanwhile=lambda: _decode_projections(r))
        _convert_weight(r.w_down_hbm, r.w_down, meanwhile=lambda: _fill_decay_tables(r))

    @pl.when(jnp.logical_and(lax.rem(i, tiles_per_seq) == 0, i < n_tiles))
    def _():
        r.s[...] = jnp.zeros_like(r.s)

    @pl.when(i == 0)
    def _():
        _prompt_step(r, tiles_per_seq=tiles_per_seq, with_mixer=True, with_mlp=False)

    @pl.when(jnp.logical_and(i > 0, i < n_tiles))
    def _():
        _prompt_step(r, tiles_per_seq=tiles_per_seq, with_mixer=True, with_mlp=True)

    @pl.when(i == n_tiles)
    def _():
        _prompt_step(r, tiles_per_seq=tiles_per_seq, with_mixer=False, with_mlp=True)
        _decode_output(r)


def _prompt_step(r, *, tiles_per_seq, with_mixer, with_mlp):
    n_chunks = ROW_TILE // CHUNK
    chunk_rows = [slice(c * CHUNK, (c + 1) * CHUNK) for c in range(n_chunks)]
    acc = []
    hidden = []

    def mlp_piece(k):
        if not with_mlp:
            return
        j = k // 2
        if k % 2 == 0:
            u = jnp.maximum(_dot(r.hn[...], r.w_up[j]), 0.0)
            hidden.append((u * u).astype(BF16))
        else:
            part = _dot(hidden.pop(), r.w_down[j * FF_TILE:(j + 1) * FF_TILE, :])
            acc[:] = [part if not acc else acc[0] + part]

    def mlp_finish():
        if with_mlp:
            r.y[...] = r.x1[...] + _rms(acc[0], r.post_mlp[...])

    if not with_mixer:
        def block(j, f):
            u = jnp.maximum(_dot(r.hn[...], r.w_up[j]), 0.0)
            rows = pl.ds(pl.multiple_of(j * FF_TILE, FF_TILE), FF_TILE)
            return f + _dot((u * u).astype(BF16), r.w_down[rows, :])

        acc.append(lax.fori_loop(0, D_FF // FF_TILE, block, jnp.zeros((ROW_TILE, D_MODEL), F32)))
        mlp_finish()
        return

    x = r.x[...]
    mlp_piece(0)
    h = _rms(x, r.pre_mix[...]).astype(BF16)
    split = 2 * A_WIDTH + 2 * B_WIDTH
    z1 = _dot(h, r.w_in[:, :split])
    _decode_state_step(r)
    z2 = _dot(h, r.w_in[:, split:])
    tile_pos = pl.multiple_of(lax.rem(pl.program_id(0), tiles_per_seq) * ROW_TILE, ROW_TILE)
    cosf = r.cos[pl.ds(tile_pos, ROW_TILE), :]
    sinf = r.sin[pl.ds(tile_pos, ROW_TILE), :]

    def head_inputs(hh):
        base = 2 * A_WIDTH + hh * B_DK
        q = _rope(z1[:, base:base + B_DK], cosf, sinf)
        k = _rope(z1[:, base + B_WIDTH:base + B_WIDTH + B_DK], cosf, sinf) * (B_DK ** -0.5)
        v = z2[:, hh * B_DV:(hh + 1) * B_DV].astype(BF16)
        return {
            "q": q.astype(BF16),
            "k": k.astype(BF16),
            "v": v,
            "qd": [(q[c] * r.qdec[hh]).astype(BF16) for c in chunk_rows],
            "kd": [(k[c] * r.kdec[hh]).astype(BF16) for c in chunk_rows],
        }

    def head_dots_a(hd):
        hd["scores"] = [_dot_nt(hd["q"][c], hd["k"][c]) for c in chunk_rows]
        hd["kv"] = [_dot_tn(hd["kd"][ci], hd["v"][c]) for ci, c in enumerate(chunk_rows)]

    def head_chain(hh, hd):
        s = r.s[hh]
        hd["s"] = []
        for ci in range(n_chunks):
            hd["s"].append(s.astype(BF16))
            s = s * math.exp(LOG_GAMMA[hh] * CHUNK) + hd["kv"][ci]
        r.s[hh] = s
        hd["p"] = [(hd["scores"][ci] * r.mask[hh]).astype(BF16) for ci in range(n_chunks)]

    def head_dots_b(hd):
        hd["o"] = [_dot(hd["p"][ci], hd["v"][c]) + _dot(hd["qd"][ci], hd["s"][ci])
                   for ci, c in enumerate(chunk_rows)]

    def head_finish(hh, hd):
        base = B_WIDTH + hh * B_DV
        gate = _silu(z2[:, base:base + B_DV])
        gnw = r.gn[:, hh * B_DV:(hh + 1) * B_DV]
        for ci, c in enumerate(chunk_rows):
            o = _center_norm(hd["o"][ci]) * gnw
            r.mix[c, A_WIDTH + hh * B_DV:A_WIDTH + (hh + 1) * B_DV] = (o * gate[c]).astype(BF16)

    mlp_piece(1)
    ua, va = [], []
    for g in range(A_GROUPS):
        gc = slice(g * A_DIM, (g + 1) * A_DIM)
        ua.append(_gelu(z1[:, g * A_DIM:(g + 1) * A_DIM]))
        vg = _gelu(z1[:, A_WIDTH + g * A_DIM:A_WIDTH + (g + 1) * A_DIM])
        va.append((_center_norm(vg) * r.lnw[:, gc] + r.lnb[:, gc]).astype(BF16))
        if g == 1:
            mlp_piece(2)
    heads = [head_inputs(0)]

    row = lax.broadcasted_iota(jnp.int32, (CHUNK, CHUNK), 0)
    col = lax.broadcasted_iota(jnp.int32, (CHUNK, CHUNK), 1)
    mixed = []
    for g in range(A_GROUPS):
        w_tril = jnp.where(row >= col, r.ws[g], 0.0).astype(BF16)
        mixed.append([_dot(w_tril, va[g][c]) for c in chunk_rows])
    head_dots_a(heads[0])

    for hh in range(B_HEADS):
        mlp_piece(3 + hh)
        if hh == 0:
            for g in range(A_GROUPS):
                gc = slice(g * A_DIM, (g + 1) * A_DIM)
                bias = jnp.broadcast_to(r.bs[:, g:g + 1], (CHUNK, A_DIM))
                for ci, c in enumerate(chunk_rows):
                    r.mix[c, gc] = (ua[g][c] * (mixed[g][ci] + bias)).astype(BF16)
        else:
            head_finish(hh - 1, heads[hh - 1])
        head_chain(hh, heads[hh])
        if hh + 1 < B_HEADS:
            heads.append(head_inputs(hh + 1))
        head_dots_b(heads[hh])
        if hh + 1 < B_HEADS:
            head_dots_a(heads[hh + 1])

    mlp_piece(7)
    head_finish(B_HEADS - 1, heads[B_HEADS - 1])
    half = ROW_TILE // 2
    mix_a = _dot(r.mix[:half, :], r.w_out[...])
    mlp_finish()
    mix_b = _dot(r.mix[half:, :], r.w_out[...])
    for rows, mix in ((slice(0, half), mix_a), (slice(half, ROW_TILE), mix_b)):
        x1 = x[rows] + _rms(mix, r.post_mix[...])
        r.x1[rows, :] = x1
        r.hn[rows, :] = _rms(x1, r.pre_mlp[...]).astype(BF16)


def _resident(shape):
    zeros = (0,) * len(shape)
    return pl.BlockSpec(shape, lambda i: zeros, pipeline_mode=pl.Buffered(1))


def _layer(x, xs, s0, p):
    batch, seq, _ = x.shape
    n_dec = xs.shape[0]
    tiles_per_seq = seq // ROW_TILE
    n_tiles = batch * tiles_per_seq
    assert n_dec == n_tiles * DEC_ROWS and DEC_GROUP % DEC_ROWS == 0

    def mixer_tile(i):
        return jnp.minimum(i, n_tiles - 1)

    def mlp_tile(i):
        return jnp.maximum(i - 1, 0)

    x_spec = pl.BlockSpec((None, ROW_TILE, D_MODEL),
                          lambda i: (mixer_tile(i) // tiles_per_seq, mixer_tile(i) % tiles_per_seq, 0))
    y_spec = pl.BlockSpec((None, ROW_TILE, D_MODEL),
                          lambda i: (mlp_tile(i) // tiles_per_seq, mlp_tile(i) % tiles_per_seq, 0))
    tok_spec = pl.BlockSpec((n_dec, None, D_MODEL), lambda i: (0, 0, 0), pipeline_mode=pl.Buffered(1))
    ys_spec = pl.BlockSpec((n_dec, None, D_MODEL), lambda i: (0, 0, 0))
    va_spec = pl.BlockSpec((n_dec, None, A_GROUPS, A_DIM), lambda i: (0, 0, 0, 0))
    dec_state_spec = pl.BlockSpec((DEC_ROWS, B_HEADS, B_DK, B_DV), lambda i: (mixer_tile(i), 0, 0, 0))
    hbm = pl.BlockSpec(memory_space=pl.ANY)
    in_specs = [
        x_spec, tok_spec, dec_state_spec,
        _resident((1, D_MODEL)),
        _resident((1, A_WIDTH)), _resident((1, A_WIDTH)),
        _resident((A_GROUPS, CHUNK, CHUNK)), _resident((CHUNK, A_GROUPS)),
        _resident((A_GROUPS, CHUNK)),
        _resident((1, B_WIDTH)),
        _resident((1, D_MODEL)), _resident((1, D_MODEL)), _resident((1, D_MODEL)),
        hbm, hbm, hbm, hbm,
    ]
    out_specs = [
        y_spec,
        pl.BlockSpec((None, B_HEADS, B_DK, B_DV), lambda i: (mixer_tile(i) // tiles_per_seq, 0, 0, 0)),
        ys_spec, dec_state_spec, va_spec,
    ]
    out_shape = [
        jax.ShapeDtypeStruct(x.shape, F32),
        jax.ShapeDtypeStruct((batch, B_HEADS, B_DK, B_DV), F32),
        jax.ShapeDtypeStruct((n_dec, 1, D_MODEL), F32),
        jax.ShapeDtypeStruct(s0.shape, F32),
        jax.ShapeDtypeStruct((n_dec, 1, A_GROUPS, A_DIM), F32),
    ]
    scratch_shapes = [
        pltpu.VMEM((D_MODEL, IN_COLS), BF16),
        pltpu.VMEM((D_MODEL, D_MODEL), BF16),
        pltpu.VMEM((D_FF // FF_TILE, D_MODEL, FF_TILE), BF16),
        pltpu.VMEM((D_FF, D_MODEL), BF16),
        pltpu.VMEM((ROW_TILE, D_MODEL), BF16),
        pltpu.VMEM((ROW_TILE, D_MODEL), F32),
        pltpu.VMEM((ROW_TILE, D_MODEL), BF16),
        pltpu.VMEM((seq, B_DK), F32),
        pltpu.VMEM((seq, B_DK), F32),
        pltpu.VMEM((B_HEADS, CHUNK, CHUNK), F32),
        pltpu.VMEM((B_HEADS, CHUNK, B_DK), F32),
        pltpu.VMEM((B_HEADS, CHUNK, B_DK), F32),
        pltpu.VMEM((n_dec, B_WIDTH), F32),
        pltpu.VMEM((n_dec, B_WIDTH), F32),
        pltpu.VMEM((n_dec, B_WIDTH), F32),
        pltpu.VMEM((n_dec, B_WIDTH), F32),
        pltpu.VMEM((n_dec, A_WIDTH), F32),
        pltpu.VMEM((n_dec, B_WIDTH), F32),
    ]
    return pl.pallas_call(
        functools.partial(_layer_kernel, tiles_per_seq=tiles_per_seq, n_tiles=n_tiles),
        grid=(n_tiles + 1,),
        in_specs=in_specs,
        out_specs=out_specs,
        out_shape=out_shape,
        scratch_shapes=scratch_shapes,
        compiler_params=pltpu.CompilerParams(
            dimension_semantics=("arbitrary",),
            vmem_limit_bytes=VMEM_LIMIT_BYTES),
        name="hybrid_layer",
    )(x, xs, s0, p["pre_mix_w"], p["ln_v_w"], p["ln_v_b"], p["w_s"], p["b_s_t"], p["b_s"],
      p["gn_w"], p["post_mix_w"], p["pre_mlp_w"], p["post_mlp_w"],
      p["w_in"], p["w_out"], p["w_up"], p["w_down"])


def kernel(x_prompt, x_sample, state_ret, pre_mix_w, w_in, ln_v_w, ln_v_b, w_s, b_s, gn_w,
           w_out, post_mix_w, pre_mlp_w, w_up, w_down, post_mlp_w):
    depth = w_in.shape[0]
    seq = x_prompt.shape[1]
    assert x_sample.shape[1] == 1 and seq % ROW_TILE == 0

    yp, ys = x_prompt, x_sample
    sp_list, ss_list, vs_list = [], [], []
    for l in range(depth):
        p = {
            "pre_mix_w": pre_mix_w[l][None, :],
            "w_in": w_in[l],
            "ln_v_w": ln_v_w[l].reshape(1, A_WIDTH),
            "ln_v_b": ln_v_b[l].reshape(1, A_WIDTH),
            "w_s": w_s[l],
            "b_s_t": b_s[l].T,
            "b_s": b_s[l],
            "gn_w": gn_w[l].reshape(1, B_WIDTH),
            "w_out": w_out[l],
            "post_mix_w": post_mix_w[l][None, :],
            "pre_mlp_w": pre_mlp_w[l][None, :],
            "w_up": w_up[l],
            "w_down": w_down[l],
            "post_mlp_w": post_mlp_w[l][None, :],
        }
        yp, sp, ys, ss, vs = _layer(yp, ys, state_ret[l], p)
        sp_list.append(sp)
        ss_list.append(ss)
        vs_list.append(vs)
    return (yp, ys, jnp.stack(sp_list, axis=0), jnp.stack(ss_list, axis=0), jnp.stack(vs_list, axis=0))
```

```python
import functools
import math
from typing import Any, NamedTuple

import jax
import jax.numpy as jnp
from jax import lax
from jax.experimental import pallas as pl
from jax.experimental.pallas import tpu as pltpu

D_MODEL = 1024
A_WIDTH = 512
B_WIDTH = 512
A_GROUPS = 4
A_DIM = 128
CHUNK = 128
B_HEADS = 4
B_DK = 128
B_DV = 128
D_FF = 4096
IN_COLS = 2 * A_WIDTH + 4 * B_WIDTH
ROPE_BASE = 10000.0
EPS = 1e-6
PAST_LEN = 16384
LOG_GAMMA = [math.log(1.0 - 2.0 ** (-5.0 - h)) for h in range(B_HEADS)]

ROW_TILE = 512
FF_TILE = 1024
MLP_PIECES = 2 * (D_FF // FF_TILE)
DEC_ROWS = 4
DEC_GROUP = 16
STAGE_BYTES = 1 << 20
STAGE_SLOTS = 6
VMEM_LIMIT_BYTES = 60 * 1024 * 1024

F32 = jnp.float32
BF16 = jnp.bfloat16


def _rms(x, w):
    return x * lax.rsqrt(jnp.mean(x * x, axis=-1, keepdims=True) + EPS) * w


def _center_norm(x):
    mu = jnp.mean(x, axis=-1, keepdims=True)
    xc = x - mu
    return xc * lax.rsqrt(jnp.mean(xc * xc, axis=-1, keepdims=True) + EPS)


def _gelu(x):
    return jax.nn.gelu(x, approximate=True)


def _silu(x):
    return x * (1.0 / (1.0 + jnp.exp(-x)))


def _rope(x, cosf, sinf):
    return x * cosf + pltpu.roll(x, B_DK // 2, axis=1) * sinf


def _rope_rows(pos):
    lane = lax.broadcasted_iota(jnp.int32, pos.shape, 1)
    j = lax.rem(lane, B_DK // 2).astype(F32)
    ang = pos * jnp.exp(j * (-math.log(ROPE_BASE) / (B_DK // 2)))
    sin = jnp.sin(ang)
    return jnp.cos(ang), jnp.where(lane < B_DK // 2, -sin, sin)


def _dot(a, b):
    return jnp.dot(a, b, preferred_element_type=F32)


def _dot_nt(a, b):
    return lax.dot_general(a, b, (((1,), (1,)), ((), ())), preferred_element_type=F32)


def _dot_tn(a, b):
    return lax.dot_general(a, b, (((0,), (0,)), ((), ())), preferred_element_type=F32)


class _Refs(NamedTuple):
    x: Any
    xs: Any
    s0: Any
    pre_mix: Any
    lnw: Any
    lnb: Any
    ws: Any
    bsr: Any
    gn: Any
    post_mix: Any
    pre_mlp: Any
    post_mlp: Any
    w_in_hbm: Any
    w_out_hbm: Any
    w_up_hbm: Any
    w_down_hbm: Any
    y: Any
    s: Any
    ys: Any
    s_dec: Any
    va: Any
    w_in: Any
    w_out: Any
    w_up: Any
    w_down: Any
    mix: Any
    x1: Any
    hn: Any
    cos: Any
    sin: Any
    mask: Any
    qdec: Any
    kdec: Any
    dq: Any
    dk: Any
    dv: Any
    dgate: Any
    douta: Any
    do: Any
    bias: Any


def _convert_weight(w_hbm, w_bf, col_pieces=1, meanwhile=None):
    n_rows, n_cols = w_hbm.shape
    piece = n_cols // col_pieces
    rows_per = 1 << int(math.log2(STAGE_BYTES // (4 * n_cols)))
    assert n_rows % rows_per == 0 and rows_per % 16 == 0
    n_chunks = n_rows // rows_per

    ahead = STAGE_SLOTS - 1

    def body(stage, sem):
        def copy(c):
            slot = lax.rem(c, STAGE_SLOTS)
            return pltpu.make_async_copy(
                w_hbm.at[pl.ds(c * rows_per, rows_per), :], stage.at[slot], sem.at[slot])

        for c in range(min(ahead, n_chunks)):
            copy(c).start()
        if meanwhile is not None:
            meanwhile()

        def step(c, carry):
            slot = lax.rem(c, STAGE_SLOTS)

            @pl.when(c + ahead < n_chunks)
            def _():
                copy(c + ahead).start()

            copy(c).wait()
            rows = pl.ds(pl.multiple_of(c * rows_per, rows_per), rows_per)
            if col_pieces == 1:
                w_bf[rows, :] = stage[slot].astype(BF16)
            else:
                for p in range(col_pieces):
                    w_bf[p, rows, :] = stage[slot, :, p * piece:(p + 1) * piece].astype(BF16)
            return carry

        lax.fori_loop(0, n_chunks, step, 0)

    pl.run_scoped(body, pltpu.VMEM((STAGE_SLOTS, rows_per, n_cols), F32),
                  pltpu.SemaphoreType.DMA((STAGE_SLOTS,)))


def _fill_rope_tables(r, part, n_parts):
    blocks = r.cos.shape[0] // ROW_TILE // n_parts

    def rope_block(b, carry):
        r0 = pl.multiple_of(b * ROW_TILE, ROW_TILE)
        pos = (lax.broadcasted_iota(jnp.int32, (ROW_TILE, B_DK), 0) + r0).astype(F32)
        cosf, sinf = _rope_rows(pos)
        r.cos[pl.ds(r0, ROW_TILE), :] = cosf
        r.sin[pl.ds(r0, ROW_TILE), :] = sinf
        return carry

    lax.fori_loop(part * blocks, (part + 1) * blocks, rope_block, 0)


def _fill_decay_tables(r):
    row = lax.broadcasted_iota(jnp.int32, (CHUNK, CHUNK), 0).astype(F32)
    col = lax.broadcasted_iota(jnp.int32, (CHUNK, CHUNK), 1).astype(F32)
    diff = row - col
    for g in range(A_GROUPS):
        on_diag = jnp.where(diff == 0, r.bsr[g:g + 1, :], 0.0)
        r.bias[g] = jnp.broadcast_to(jnp.sum(on_diag, axis=1, keepdims=True), (CHUNK, A_DIM))
    for hh in range(B_HEADS):
        lg = LOG_GAMMA[hh]
        r.mask[hh] = jnp.where(diff >= 0, jnp.exp(lg * jnp.maximum(diff, 0.0)), 0.0)
        r.qdec[hh] = jnp.exp(lg * (row + 1.0))
        r.kdec[hh] = jnp.exp(lg * (CHUNK - 1.0 - row))


def _decode_projections(r):
    h = _rms(r.xs[...], r.pre_mix[...]).astype(BF16)
    z = _dot(h, r.w_in[...])
    cosf, sinf = _rope_rows(jnp.full((1, B_DK), PAST_LEN, F32))
    for g in range(A_GROUPS):
        gc = slice(g * A_DIM, (g + 1) * A_DIM)
        ua = _gelu(z[:, g * A_DIM:(g + 1) * A_DIM])
        va = _gelu(z[:, A_WIDTH + g * A_DIM:A_WIDTH + (g + 1) * A_DIM])
        va = _center_norm(va) * r.lnw[:, gc] + r.lnb[:, gc]
        r.va[:, g, :] = va
        r.douta[:, gc] = ua * (r.ws[g][0:1, 0:1] * va + r.bsr[g:g + 1, 0:1])
    base = 2 * A_WIDTH
    for hh in range(B_HEADS):
        hc = slice(hh * B_DK, (hh + 1) * B_DK)
        r.dq[:, hc] = _rope(z[:, base + hh * B_DK:base + (hh + 1) * B_DK], cosf, sinf)
        r.dk[:, hc] = _rope(z[:, base + B_WIDTH + hh * B_DK:base + B_WIDTH + (hh + 1) * B_DK],
                            cosf, sinf) * (B_DK ** -0.5)
    r.dv[...] = z[:, base + 2 * B_WIDTH:base + 3 * B_WIDTH]
    r.dgate[...] = _silu(z[:, base + 3 * B_WIDTH:])
    r.do[...] = jnp.zeros_like(r.do)


def _decode_state_step(r):
    i = pl.program_id(0)
    per_group = DEC_GROUP // DEC_ROWS
    rows = pl.ds(pl.multiple_of((i // per_group) * DEC_GROUP, DEC_GROUP), DEC_GROUP)
    first = lax.rem(i, per_group) * DEC_ROWS
    rid = lax.broadcasted_iota(jnp.int32, (DEC_GROUP, B_DK), 0)
    live = jnp.logical_and(rid >= first, rid < first + DEC_ROWS)
    for hh in range(B_HEADS):
        hc = slice(hh * B_DK, (hh + 1) * B_DK)
        q = r.dq[rows, hc]
        k = r.dk[rows, hc]
        v = r.dv[rows, hc]
        gam = math.exp(LOG_GAMMA[hh])
        qk = jnp.sum(q * k, axis=-1, keepdims=True)
        qg = (q * gam).astype(BF16)
        v_sel = jnp.concatenate([jnp.where(rid == first + j, v, 0.0) for j in range(DEC_ROWS)],
                                axis=-1).astype(BF16)
        kv = _dot_tn(k.astype(BF16), v_sel)
        one_sel = jnp.concatenate([jnp.where(rid == first + j, 1.0, 0.0) for j in range(DEC_ROWS)],
                                  axis=-1).astype(BF16)
        q_cols = _dot_tn(qg, one_sel)
        cross = jnp.zeros((DEC_GROUP, B_DV), F32)
        for j in range(DEC_ROWS):
            s = r.s0[j, hh]
            cross_j = jnp.sum(q_cols[:, j * B_DV:(j + 1) * B_DV] * s, axis=0, keepdims=True)
            cross = jnp.where(rid == first + j, cross_j, cross)
            r.s_dec[j, hh] = s * gam + kv[:, j * B_DV:(j + 1) * B_DV]
        r.do[rows, hc] = jnp.where(live, qk * v + cross, r.do[rows, hc])


def _decode_output(r):
    parts = [r.douta[...].astype(BF16)]
    for hh in range(B_HEADS):
        hc = slice(hh * B_DV, (hh + 1) * B_DV)
        o = _center_norm(r.do[:, hc]) * r.gn[:, hc]
        parts.append((o * r.dgate[:, hc]).astype(BF16))
    mix = _dot(jnp.concatenate(parts, axis=-1), r.w_out[...])
    x1 = r.xs[...] + _rms(mix, r.post_mix[...])
    hn = _rms(x1, r.pre_mlp[...]).astype(BF16)

    def mlp_piece(j, f):
        u = jnp.maximum(_dot(hn, r.w_up[j]), 0.0)
        rows = pl.ds(pl.multiple_of(j * FF_TILE, FF_TILE), FF_TILE)
        return f + _dot((u * u).astype(BF16), r.w_down[rows, :])

    f = lax.fori_loop(0, D_FF // FF_TILE, mlp_piece, jnp.zeros_like(x1))
    r.ys[...] = x1 + _rms(f, r.post_mlp[...])


def _layer_kernel(*refs, tiles_per_seq, n_tiles):
    r = _Refs(*refs)
    i = pl.program_id(0)

    @pl.when(i == 0)
    def _():
        _convert_weight(r.w_in_hbm, r.w_in, meanwhile=lambda: _fill_rope_tables(r, 0, 2))
        _convert_weight(r.w_out_hbm, r.w_out, meanwhile=lambda: _fill_rope_tables(r, 1, 2))
        _convert_weight(r.w_up_hbm, r.w_up, col_pieces=D_FF // FF_TILE,
                        meanwhile=lambda: _decode_projections(r))
        _convert_weight(r.w_down_hbm, r.w_down, meanwhile=lambda: _fill_decay_tables(r))

    @pl.when(jnp.logical_and(lax.rem(i, tiles_per_seq) == 0, i < n_tiles))
    def _():
        r.s[...] = jnp.zeros_like(r.s)

    @pl.when(i == 0)
    def _():
        _prompt_step(r, tiles_per_seq=tiles_per_seq, with_mixer=True, with_mlp=False)

    @pl.when(jnp.logical_and(i > 0, i < n_tiles))
    def _():
        _prompt_step(r, tiles_per_seq=tiles_per_seq, with_mixer=True, with_mlp=True)

    @pl.when(i == n_tiles)
    def _():
        _prompt_step(r, tiles_per_seq=tiles_per_seq, with_mixer=False, with_mlp=True)
        _decode_output(r)


def _prompt_step(r, *, tiles_per_seq, with_mixer, with_mlp):
    n_chunks = ROW_TILE // CHUNK
    chunk_rows = [slice(c * CHUNK, (c + 1) * CHUNK) for c in range(n_chunks)]
    acc = []
    hidden = []

    def mlp_piece(k):
        if not with_mlp:
            return
        j = k // 2
        if k % 2 == 0:
            u = jnp.maximum(_dot(r.hn[...], r.w_up[j]), 0.0)
            hidden.append((u * u).astype(BF16))
        else:
            part = _dot(hidden.pop(), r.w_down[j * FF_TILE:(j + 1) * FF_TILE, :])
            acc[:] = [part if not acc else acc[0] + part]

    def mlp_finish():
        if with_mlp:
            r.y[...] = r.x1[...] + _rms(acc[0], r.post_mlp[...])

    if not with_mixer:
        def block(j, f):
            u = jnp.maximum(_dot(r.hn[...], r.w_up[j]), 0.0)
            rows = pl.ds(pl.multiple_of(j * FF_TILE, FF_TILE), FF_TILE)
            return f + _dot((u * u).astype(BF16), r.w_down[rows, :])

        acc.append(lax.fori_loop(0, D_FF // FF_TILE, block, jnp.zeros((ROW_TILE, D_MODEL), F32)))
        mlp_finish()
        return

    x = r.x[...]
    mlp_piece(0)
    h = _rms(x, r.pre_mix[...]).astype(BF16)
    split = 2 * A_WIDTH + 2 * B_WIDTH
    z1 = _dot(h, r.w_in[:, :split])
    _decode_state_step(r)
    z2 = _dot(h, r.w_in[:, split:])
    tile_pos = pl.multiple_of(lax.rem(pl.program_id(0), tiles_per_seq) * ROW_TILE, ROW_TILE)
    cosf = r.cos[pl.ds(tile_pos, ROW_TILE), :]
    sinf = r.sin[pl.ds(tile_pos, ROW_TILE), :]

    def head_inputs(hh):
        base = 2 * A_WIDTH + hh * B_DK
        q = _rope(z1[:, base:base + B_DK], cosf, sinf)
        k = _rope(z1[:, base + B_WIDTH:base + B_WIDTH + B_DK], cosf, sinf) * (B_DK ** -0.5)
        v = z2[:, hh * B_DV:(hh + 1) * B_DV].astype(BF16)
        return {
            "q": q.astype(BF16),
            "k": k.astype(BF16),
            "v": v,
            "qd": [(q[c] * r.qdec[hh]).astype(BF16) for c in chunk_rows],
            "kd": [(k[c] * r.kdec[hh]).astype(BF16) for c in chunk_rows],
        }

    def head_dots_a(hd):
        hd["scores"] = [_dot_nt(hd["q"][c], hd["k"][c]) for c in chunk_rows]
        hd["kv"] = [_dot_tn(hd["kd"][ci], hd["v"][c]) for ci, c in enumerate(chunk_rows)]

    def head_chain(hh, hd):
        s = r.s[hh]
        hd["s"] = []
        for ci in range(n_chunks):
            hd["s"].append(s.astype(BF16))
            s = s * math.exp(LOG_GAMMA[hh] * CHUNK) + hd["kv"][ci]
        r.s[hh] = s
        hd["p"] = [(hd["scores"][ci] * r.mask[hh]).astype(BF16) for ci in range(n_chunks)]

    def head_dots_b(hd):
        hd["o"] = [_dot(hd["p"][ci], hd["v"][c]) + _dot(hd["qd"][ci], hd["s"][ci])
                   for ci, c in enumerate(chunk_rows)]

    def head_finish(hh, hd):
        base = B_WIDTH + hh * B_DV
        gate = _silu(z2[:, base:base + B_DV])
        gnw = r.gn[:, hh * B_DV:(hh + 1) * B_DV]
        for ci, c in enumerate(chunk_rows):
            o = _center_norm(hd["o"][ci]) * gnw
            r.mix[c, A_WIDTH + hh * B_DV:A_WIDTH + (hh + 1) * B_DV] = (o * gate[c]).astype(BF16)

    mlp_piece(1)
    ua, va = [], []
    for g in range(A_GROUPS):
        gc = slice(g * A_DIM, (g + 1) * A_DIM)
        ua.append(_gelu(z1[:, g * A_DIM:(g + 1) * A_DIM]))
        vg = _gelu(z1[:, A_WIDTH + g * A_DIM:A_WIDTH + (g + 1) * A_DIM])
        va.append((_center_norm(vg) * r.lnw[:, gc] + r.lnb[:, gc]).astype(BF16))
        if g == 1:
            mlp_piece(2)
    heads = [head_inputs(0)]

    row = lax.broadcasted_iota(jnp.int32, (CHUNK, CHUNK), 0)
    col = lax.broadcasted_iota(jnp.int32, (CHUNK, CHUNK), 1)
    mixed = []
    for g in range(A_GROUPS):
        w_tril = jnp.where(row >= col, r.ws[g], 0.0).astype(BF16)
        mixed.append([_dot(w_tril, va[g][c]) for c in chunk_rows])
    head_dots_a(heads[0])

    for hh in range(B_HEADS):
        mlp_piece(3 + hh)
        if hh == 0:
            for g in range(A_GROUPS):
                gc = slice(g * A_DIM, (g + 1) * A_DIM)
                bias = r.bias[g]
                for ci, c in enumerate(chunk_rows):
                    r.mix[c, gc] = (ua[g][c] * (mixed[g][ci] + bias)).astype(BF16)
        else:
            head_finish(hh - 1, heads[hh - 1])
        head_chain(hh, heads[hh])
        if hh + 1 < B_HEADS:
            heads.append(head_inputs(hh + 1))
        head_dots_b(heads[hh])
        if hh + 1 < B_HEADS:
            head_dots_a(heads[hh + 1])

    mlp_piece(7)
    head_finish(B_HEADS - 1, heads[B_HEADS - 1])
    half = ROW_TILE // 2
    mix_a = _dot(r.mix[:half, :], r.w_out[...])
    mlp_finish()
    mix_b = _dot(r.mix[half:, :], r.w_out[...])
    for rows, mix in ((slice(0, half), mix_a), (slice(half, ROW_TILE), mix_b)):
        x1 = x[rows] + _rms(mix, r.post_mix[...])
        r.x1[rows, :] = x1
        r.hn[rows, :] = _rms(x1, r.pre_mlp[...]).astype(BF16)


def _resident(shape):
    zeros = (0,) * len(shape)
    return pl.BlockSpec(shape, lambda i: zeros, pipeline_mode=pl.Buffered(1))


def _layer(x, xs, s0, p):
    batch, seq, _ = x.shape
    n_dec = xs.shape[0]
    tiles_per_seq = seq // ROW_TILE
    n_tiles = batch * tiles_per_seq
    assert n_dec == n_tiles * DEC_ROWS and DEC_GROUP % DEC_ROWS == 0

    def mixer_tile(i):
        return jnp.minimum(i, n_tiles - 1)

    def mlp_tile(i):
        return jnp.maximum(i - 1, 0)

    x_spec = pl.BlockSpec((None, ROW_TILE, D_MODEL),
                          lambda i: (mixer_tile(i) // tiles_per_seq, mixer_tile(i) % tiles_per_seq, 0))
    y_spec = pl.BlockSpec((None, ROW_TILE, D_MODEL),
                          lambda i: (mlp_tile(i) // tiles_per_seq, mlp_tile(i) % tiles_per_seq, 0))
    tok_spec = pl.BlockSpec((n_dec, None, D_MODEL), lambda i: (0, 0, 0), pipeline_mode=pl.Buffered(1))
    ys_spec = pl.BlockSpec((n_dec, None, D_MODEL), lambda i: (0, 0, 0))
    va_spec = pl.BlockSpec((n_dec, None, A_GROUPS, A_DIM), lambda i: (0, 0, 0, 0))
    dec_state_spec = pl.BlockSpec((DEC_ROWS, B_HEADS, B_DK, B_DV), lambda i: (mixer_tile(i), 0, 0, 0))
    hbm = pl.BlockSpec(memory_space=pl.ANY)
    in_specs = [
        x_spec, tok_spec, dec_state_spec,
        _resident((1, D_MODEL)),
        _resident((1, A_WIDTH)), _resident((1, A_WIDTH)),
        _resident((A_GROUPS, CHUNK, CHUNK)), _resident((A_GROUPS, CHUNK)),
        _resident((1, B_WIDTH)),
        _resident((1, D_MODEL)), _resident((1, D_MODEL)), _resident((1, D_MODEL)),
        hbm, hbm, hbm, hbm,
    ]
    out_specs = [
        y_spec,
        pl.BlockSpec((None, B_HEADS, B_DK, B_DV), lambda i: (mixer_tile(i) // tiles_per_seq, 0, 0, 0)),
        ys_spec, dec_state_spec, va_spec,
    ]
    out_shape = [
        jax.ShapeDtypeStruct(x.shape, F32),
        jax.ShapeDtypeStruct((batch, B_HEADS, B_DK, B_DV), F32),
        jax.ShapeDtypeStruct((n_dec, 1, D_MODEL), F32),
        jax.ShapeDtypeStruct(s0.shape, F32),
        jax.ShapeDtypeStruct((n_dec, 1, A_GROUPS, A_DIM), F32),
    ]
    scratch_shapes = [
        pltpu.VMEM((D_MODEL, IN_COLS), BF16),
        pltpu.VMEM((D_MODEL, D_MODEL), BF16),
        pltpu.VMEM((D_FF // FF_TILE, D_MODEL, FF_TILE), BF16),
        pltpu.VMEM((D_FF, D_MODEL), BF16),
        pltpu.VMEM((ROW_TILE, D_MODEL), BF16),
        pltpu.VMEM((ROW_TILE, D_MODEL), F32),
        pltpu.VMEM((ROW_TILE, D_MODEL), BF16),
        pltpu.VMEM((seq, B_DK), F32),
        pltpu.VMEM((seq, B_DK), F32),
        pltpu.VMEM((B_HEADS, CHUNK, CHUNK), F32),
        pltpu.VMEM((B_HEADS, CHUNK, B_DK), F32),
        pltpu.VMEM((B_HEADS, CHUNK, B_DK), F32),
        pltpu.VMEM((n_dec, B_WIDTH), F32),
        pltpu.VMEM((n_dec, B_WIDTH), F32),
        pltpu.VMEM((n_dec, B_WIDTH), F32),
        pltpu.VMEM((n_dec, B_WIDTH), F32),
        pltpu.VMEM((n_dec, A_WIDTH), F32),
        pltpu.VMEM((n_dec, B_WIDTH), F32),
        pltpu.VMEM((A_GROUPS, CHUNK, A_DIM), F32),
    ]
    return pl.pallas_call(
        functools.partial(_layer_kernel, tiles_per_seq=tiles_per_seq, n_tiles=n_tiles),
        grid=(n_tiles + 1,),
        in_specs=in_specs,
        out_specs=out_specs,
        out_shape=out_shape,
        scratch_shapes=scratch_shapes,
        compiler_params=pltpu.CompilerParams(
            dimension_semantics=("arbitrary",),
            vmem_limit_bytes=VMEM_LIMIT_BYTES),
        name="hybrid_layer",
    )(x, xs, s0, p["pre_mix_w"], p["ln_v_w"], p["ln_v_b"], p["w_s"], p["b_s"],
      p["gn_w"], p["post_mix_w"], p["pre_mlp_w"], p["post_mlp_w"],
      p["w_in"], p["w_out"], p["w_up"], p["w_down"])


def kernel(x_prompt, x_sample, state_ret, pre_mix_w, w_in, ln_v_w, ln_v_b, w_s, b_s, gn_w,
           w_out, post_mix_w, pre_mlp_w, w_up, w_down, post_mlp_w):
    depth = w_in.shape[0]
    seq = x_prompt.shape[1]
    assert x_sample.shape[1] == 1 and seq % ROW_TILE == 0

    yp, ys = x_prompt, x_sample
    sp_list, ss_list, vs_list = [], [], []
    for l in range(depth):
        p = {
            "pre_mix_w": pre_mix_w[l][None, :],
            "w_in": w_in[l],
            "ln_v_w": ln_v_w[l].reshape(1, A_WIDTH),
            "ln_v_b": ln_v_b[l].reshape(1, A_WIDTH),
            "w_s": w_s[l],
            "b_s": b_s[l],
            "gn_w": gn_w[l].reshape(1, B_WIDTH),
            "w_out": w_out[l],
            "post_mix_w": post_mix_w[l][None, :],
            "pre_mlp_w": pre_mlp_w[l][None, :],
            "w_up": w_up[l],
            "w_down": w_down[l],
            "post_mlp_w": post_mlp_w[l][None, :],
        }
        yp, sp, ys, ss, vs = _layer(yp, ys, state_ret[l], p)
        sp_list.append(sp)
        ss_list.append(ss)
        vs_list.append(vs)
    return (yp, ys, jnp.stack(sp_list, axis=0), jnp.stack(ss_list, axis=0), jnp.stack(vs_list, axis=0))
```

```python
import functools
import math
from typing import Any, NamedTuple

import jax
import jax.numpy as jnp
from jax import lax
from jax.experimental import pallas as pl
from jax.experimental.pallas import tpu as pltpu

D_MODEL = 1024
A_WIDTH = 512
B_WIDTH = 512
A_GROUPS = 4
A_DIM = 128
CHUNK = 128
B_HEADS = 4
B_DK = 128
B_DV = 128
D_FF = 4096
IN_COLS = 2 * A_WIDTH + 4 * B_WIDTH
ROPE_BASE = 10000.0
EPS = 1e-6
PAST_LEN = 16384
LOG_GAMMA = [math.log(1.0 - 2.0 ** (-5.0 - h)) for h in range(B_HEADS)]

ROW_TILE = 512
FF_TILE = 1024
MLP_PIECES = 2 * (D_FF // FF_TILE)
DEC_ROWS = 4
DEC_GROUP = 16
STAGE_BYTES = 1 << 20
STAGE_SLOTS = 6
PARAM_ROWS = 8
VMEM_LIMIT_BYTES = 60 * 1024 * 1024

F32 = jnp.float32
BF16 = jnp.bfloat16


def _rms(x, w):
    return x * lax.rsqrt(jnp.mean(x * x, axis=-1, keepdims=True) + EPS) * w


def _center_norm(x):
    mu = jnp.mean(x, axis=-1, keepdims=True)
    xc = x - mu
    return xc * lax.rsqrt(jnp.mean(xc * xc, axis=-1, keepdims=True) + EPS)


def _gelu(x):
    return jax.nn.gelu(x, approximate=True)


def _silu(x):
    return x * (1.0 / (1.0 + jnp.exp(-x)))


def _rope(x, cosf, sinf):
    return x * cosf + pltpu.roll(x, B_DK // 2, axis=1) * sinf


def _rope_rows(pos):
    lane = lax.broadcasted_iota(jnp.int32, pos.shape, 1)
    j = lax.rem(lane, B_DK // 2).astype(F32)
    ang = pos * jnp.exp(j * (-math.log(ROPE_BASE) / (B_DK // 2)))
    sin = jnp.sin(ang)
    return jnp.cos(ang), jnp.where(lane < B_DK // 2, -sin, sin)


def _dot(a, b):
    return jnp.dot(a, b, preferred_element_type=F32)


def _dot_nt(a, b):
    return lax.dot_general(a, b, (((1,), (1,)), ((), ())), preferred_element_type=F32)


def _dot_tn(a, b):
    return lax.dot_general(a, b, (((0,), (0,)), ((), ())), preferred_element_type=F32)


class _Refs(NamedTuple):
    x: Any
    xs: Any
    s0: Any
    pre_mix: Any
    lnw: Any
    lnb: Any
    ws: Any
    bsr: Any
    gn: Any
    post_mix: Any
    pre_mlp: Any
    post_mlp: Any
    w_in_hbm: Any
    w_out_hbm: Any
    w_up_hbm: Any
    w_down_hbm: Any
    y: Any
    s: Any
    ys: Any
    s_dec: Any
    va: Any
    w_in: Any
    w_out: Any
    w_up: Any
    w_down: Any
    mix: Any
    x1: Any
    hn: Any
    cos: Any
    sin: Any
    mask: Any
    qdec: Any
    kdec: Any
    dq: Any
    dk: Any
    dv: Any
    dgate: Any
    douta: Any
    do: Any
    bias: Any


def _convert_weight(w_hbm, w_bf, col_pieces=1, meanwhile=None):
    n_rows, n_cols = w_hbm.shape
    piece = n_cols // col_pieces
    rows_per = 1 << int(math.log2(STAGE_BYTES // (4 * n_cols)))
    assert n_rows % rows_per == 0 and rows_per % 16 == 0
    n_chunks = n_rows // rows_per

    ahead = STAGE_SLOTS - 1

    def body(stage, sem):
        def copy(c):
            slot = lax.rem(c, STAGE_SLOTS)
            return pltpu.make_async_copy(
                w_hbm.at[pl.ds(c * rows_per, rows_per), :], stage.at[slot], sem.at[slot])

        for c in range(min(ahead, n_chunks)):
            copy(c).start()
        if meanwhile is not None:
            meanwhile()

        def step(c, carry):
            slot = lax.rem(c, STAGE_SLOTS)

            @pl.when(c + ahead < n_chunks)
            def _():
                copy(c + ahead).start()

            copy(c).wait()
            rows = pl.ds(pl.multiple_of(c * rows_per, rows_per), rows_per)
            if col_pieces == 1:
                w_bf[rows, :] = stage[slot].astype(BF16)
            else:
                for p in range(col_pieces):
                    w_bf[p, rows, :] = stage[slot, :, p * piece:(p + 1) * piece].astype(BF16)
            return carry

        lax.fori_loop(0, n_chunks, step, 0)

    pl.run_scoped(body, pltpu.VMEM((STAGE_SLOTS, rows_per, n_cols), F32),
                  pltpu.SemaphoreType.DMA((STAGE_SLOTS,)))


def _fill_rope_tables(r, part, n_parts):
    blocks = r.cos.shape[0] // ROW_TILE // n_parts

    def rope_block(b, carry):
        r0 = pl.multiple_of(b * ROW_TILE, ROW_TILE)
        pos = (lax.broadcasted_iota(jnp.int32, (ROW_TILE, B_DK), 0) + r0).astype(F32)
        cosf, sinf = _rope_rows(pos)
        r.cos[pl.ds(r0, ROW_TILE), :] = cosf
        r.sin[pl.ds(r0, ROW_TILE), :] = sinf
        return carry

    lax.fori_loop(part * blocks, (part + 1) * blocks, rope_block, 0)


def _fill_decay_tables(r):
    row = lax.broadcasted_iota(jnp.int32, (CHUNK, CHUNK), 0).astype(F32)
    col = lax.broadcasted_iota(jnp.int32, (CHUNK, CHUNK), 1).astype(F32)
    diff = row - col
    for g in range(A_GROUPS):
        on_diag = jnp.where(diff == 0, r.bsr[:, g * A_DIM:(g + 1) * A_DIM], 0.0)
        r.bias[g] = jnp.broadcast_to(jnp.sum(on_diag, axis=1, keepdims=True), (CHUNK, A_DIM))
    for hh in range(B_HEADS):
        lg = LOG_GAMMA[hh]
        r.mask[hh] = jnp.where(diff >= 0, jnp.exp(lg * jnp.maximum(diff, 0.0)), 0.0)
        r.qdec[hh] = jnp.exp(lg * (row + 1.0))
        r.kdec[hh] = jnp.exp(lg * (CHUNK - 1.0 - row))


def _decode_projections(r):
    h = _rms(r.xs[...], r.pre_mix[...]).astype(BF16)
    z = _dot(h, r.w_in[...])
    cosf, sinf = _rope_rows(jnp.full((1, B_DK), PAST_LEN, F32))
    for g in range(A_GROUPS):
        gc = slice(g * A_DIM, (g + 1) * A_DIM)
        ua = _gelu(z[:, g * A_DIM:(g + 1) * A_DIM])
        va = _gelu(z[:, A_WIDTH + g * A_DIM:A_WIDTH + (g + 1) * A_DIM])
        va = _center_norm(va) * r.lnw[:, gc] + r.lnb[:, gc]
        r.va[:, g, :] = va
        r.douta[:, gc] = ua * (r.ws[g][0:1, 0:1] * va + r.bsr[:, g * A_DIM:g * A_DIM + 1])
    base = 2 * A_WIDTH
    for hh in range(B_HEADS):
        hc = slice(hh * B_DK, (hh + 1) * B_DK)
        r.dq[:, hc] = _rope(z[:, base + hh * B_DK:base + (hh + 1) * B_DK], cosf, sinf)
        r.dk[:, hc] = _rope(z[:, base + B_WIDTH + hh * B_DK:base + B_WIDTH + (hh + 1) * B_DK],
                            cosf, sinf) * (B_DK ** -0.5)
    r.dv[...] = z[:, base + 2 * B_WIDTH:base + 3 * B_WIDTH]
    r.dgate[...] = _silu(z[:, base + 3 * B_WIDTH:])
    r.do[...] = jnp.zeros_like(r.do)


def _decode_state_step(r):
    i = pl.program_id(0)
    per_group = DEC_GROUP // DEC_ROWS
    rows = pl.ds(pl.multiple_of((i // per_group) * DEC_GROUP, DEC_GROUP), DEC_GROUP)
    first = lax.rem(i, per_group) * DEC_ROWS
    rid = lax.broadcasted_iota(jnp.int32, (DEC_GROUP, B_DK), 0)
    live = jnp.logical_and(rid >= first, rid < first + DEC_ROWS)
    for hh in range(B_HEADS):
        hc = slice(hh * B_DK, (hh + 1) * B_DK)
        q = r.dq[rows, hc]
        k = r.dk[rows, hc]
        v = r.dv[rows, hc]
        gam = math.exp(LOG_GAMMA[hh])
        qk = jnp.sum(q * k, axis=-1, keepdims=True)
        qg = (q * gam).astype(BF16)
        v_sel = jnp.concatenate([jnp.where(rid == first + j, v, 0.0) for j in range(DEC_ROWS)],
                                axis=-1).astype(BF16)
        kv = _dot_tn(k.astype(BF16), v_sel)
        one_sel = jnp.concatenate([jnp.where(rid == first + j, 1.0, 0.0) for j in range(DEC_ROWS)],
                                  axis=-1).astype(BF16)
        q_cols = _dot_tn(qg, one_sel)
        cross = jnp.zeros((DEC_GROUP, B_DV), F32)
        for j in range(DEC_ROWS):
            s = r.s0[j, hh]
            cross_j = jnp.sum(q_cols[:, j * B_DV:(j + 1) * B_DV] * s, axis=0, keepdims=True)
            cross = jnp.where(rid == first + j, cross_j, cross)
            r.s_dec[j, hh] = s * gam + kv[:, j * B_DV:(j + 1) * B_DV]
        r.do[rows, hc] = jnp.where(live, qk * v + cross, r.do[rows, hc])


def _decode_output(r):
    parts = [r.douta[...].astype(BF16)]
    for hh in range(B_HEADS):
        hc = slice(hh * B_DV, (hh + 1) * B_DV)
        o = _center_norm(r.do[:, hc]) * r.gn[:, hc]
        parts.append((o * r.dgate[:, hc]).astype(BF16))
    mix = _dot(jnp.concatenate(parts, axis=-1), r.w_out[...])
    x1 = r.xs[...] + _rms(mix, r.post_mix[...])
    hn = _rms(x1, r.pre_mlp[...]).astype(BF16)

    def mlp_piece(j, f):
        u = jnp.maximum(_dot(hn, r.w_up[j]), 0.0)
        rows = pl.ds(pl.multiple_of(j * FF_TILE, FF_TILE), FF_TILE)
        return f + _dot((u * u).astype(BF16), r.w_down[rows, :])

    f = lax.fori_loop(0, D_FF // FF_TILE, mlp_piece, jnp.zeros_like(x1))
    r.ys[...] = x1 + _rms(f, r.post_mlp[...])


def _layer_kernel(x_ref, xs_ref, s0_ref, par_ref, ws_ref, *rest, tiles_per_seq, n_tiles):
    r = _Refs(x_ref, xs_ref, s0_ref,
              par_ref.at[0:1, :],
              par_ref.at[4:5, :A_WIDTH], par_ref.at[4:5, A_WIDTH:],
              ws_ref,
              par_ref.at[5:6, B_WIDTH:],
              par_ref.at[5:6, :B_WIDTH],
              par_ref.at[1:2, :], par_ref.at[2:3, :], par_ref.at[3:4, :],
              *rest)
    i = pl.program_id(0)

    @pl.when(i == 0)
    def _():
        _convert_weight(r.w_in_hbm, r.w_in, meanwhile=lambda: _fill_rope_tables(r, 0, 2))
        _convert_weight(r.w_out_hbm, r.w_out, meanwhile=lambda: _fill_rope_tables(r, 1, 2))
        _convert_weight(r.w_up_hbm, r.w_up, col_pieces=D_FF // FF_TILE,
                        meanwhile=lambda: _decode_projections(r))
        _convert_weight(r.w_down_hbm, r.w_down, meanwhile=lambda: _fill_decay_tables(r))

    @pl.when(jnp.logical_and(lax.rem(i, tiles_per_seq) == 0, i < n_tiles))
    def _():
        r.s[...] = jnp.zeros_like(r.s)

    @pl.when(i == 0)
    def _():
        _prompt_step(r, tiles_per_seq=tiles_per_seq, with_mixer=True, with_mlp=False)

    @pl.when(jnp.logical_and(i > 0, i < n_tiles))
    def _():
        _prompt_step(r, tiles_per_seq=tiles_per_seq, with_mixer=True, with_mlp=True)

    @pl.when(i == n_tiles)
    def _():
        _prompt_step(r, tiles_per_seq=tiles_per_seq, with_mixer=False, with_mlp=True)
        _decode_output(r)


def _prompt_step(r, *, tiles_per_seq, with_mixer, with_mlp):
    n_chunks = ROW_TILE // CHUNK
    chunk_rows = [slice(c * CHUNK, (c + 1) * CHUNK) for c in range(n_chunks)]
    acc = []
    hidden = []

    def mlp_piece(k):
        if not with_mlp:
            return
        j = k // 2
        if k % 2 == 0:
            u = jnp.maximum(_dot(r.hn[...], r.w_up[j]), 0.0)
            hidden.append((u * u).astype(BF16))
        else:
            part = _dot(hidden.pop(), r.w_down[j * FF_TILE:(j + 1) * FF_TILE, :])
            acc[:] = [part if not acc else acc[0] + part]

    def mlp_finish():
        if with_mlp:
            r.y[...] = r.x1[...] + _rms(acc[0], r.post_mlp[...])

    if not with_mixer:
        def block(j, f):
            u = jnp.maximum(_dot(r.hn[...], r.w_up[j]), 0.0)
            rows = pl.ds(pl.multiple_of(j * FF_TILE, FF_TILE), FF_TILE)
            return f + _dot((u * u).astype(BF16), r.w_down[rows, :])

        acc.append(lax.fori_loop(0, D_FF // FF_TILE, block, jnp.zeros((ROW_TILE, D_MODEL), F32)))
        mlp_finish()
        return

    x = r.x[...]
    mlp_piece(0)
    h = _rms(x, r.pre_mix[...]).astype(BF16)
    split = 2 * A_WIDTH + 2 * B_WIDTH
    z1 = _dot(h, r.w_in[:, :split])
    _decode_state_step(r)
    z2 = _dot(h, r.w_in[:, split:])
    tile_pos = pl.multiple_of(lax.rem(pl.program_id(0), tiles_per_seq) * ROW_TILE, ROW_TILE)
    cosf = r.cos[pl.ds(tile_pos, ROW_TILE), :]
    sinf = r.sin[pl.ds(tile_pos, ROW_TILE), :]

    def head_inputs(hh):
        base = 2 * A_WIDTH + hh * B_DK
        q = _rope(z1[:, base:base + B_DK], cosf, sinf)
        k = _rope(z1[:, base + B_WIDTH:base + B_WIDTH + B_DK], cosf, sinf) * (B_DK ** -0.5)
        v = z2[:, hh * B_DV:(hh + 1) * B_DV].astype(BF16)
        return {
            "q": q.astype(BF16),
            "k": k.astype(BF16),
            "v": v,
            "qd": [(q[c] * r.qdec[hh]).astype(BF16) for c in chunk_rows],
            "kd": [(k[c] * r.kdec[hh]).astype(BF16) for c in chunk_rows],
        }

    def head_dots_a(hd):
        hd["scores"] = [_dot_nt(hd["q"][c], hd["k"][c]) for c in chunk_rows]
        hd["kv"] = [_dot_tn(hd["kd"][ci], hd["v"][c]) for ci, c in enumerate(chunk_rows)]

    def head_chain(hh, hd):
        s = r.s[hh]
        hd["s"] = []
        for ci in range(n_chunks):
            hd["s"].append(s.astype(BF16))
            s = s * math.exp(LOG_GAMMA[hh] * CHUNK) + hd["kv"][ci]
        r.s[hh] = s
        hd["p"] = [(hd["scores"][ci] * r.mask[hh]).astype(BF16) for ci in range(n_chunks)]

    def head_dots_b(hd):
        hd["o"] = [_dot(hd["p"][ci], hd["v"][c]) + _dot(hd["qd"][ci], hd["s"][ci])
                   for ci, c in enumerate(chunk_rows)]

    def head_finish(hh, hd):
        base = B_WIDTH + hh * B_DV
        gate = _silu(z2[:, base:base + B_DV])
        gnw = r.gn[:, hh * B_DV:(hh + 1) * B_DV]
        for ci, c in enumerate(chunk_rows):
            o = _center_norm(hd["o"][ci]) * gnw
            r.mix[c, A_WIDTH + hh * B_DV:A_WIDTH + (hh + 1) * B_DV] = (o * gate[c]).astype(BF16)

    mlp_piece(1)
    ua, va = [], []
    for g in range(A_GROUPS):
        gc = slice(g * A_DIM, (g + 1) * A_DIM)
        ua.append(_gelu(z1[:, g * A_DIM:(g + 1) * A_DIM]))
        vg = _gelu(z1[:, A_WIDTH + g * A_DIM:A_WIDTH + (g + 1) * A_DIM])
        va.append((_center_norm(vg) * r.lnw[:, gc] + r.lnb[:, gc]).astype(BF16))
        if g == 1:
            mlp_piece(2)
    heads = [head_inputs(0)]

    row = lax.broadcasted_iota(jnp.int32, (CHUNK, CHUNK), 0)
    col = lax.broadcasted_iota(jnp.int32, (CHUNK, CHUNK), 1)
    mixed = []
    for g in range(A_GROUPS):
        w_tril = jnp.where(row >= col, r.ws[g], 0.0).astype(BF16)
        mixed.append([_dot(w_tril, va[g][c]) for c in chunk_rows])
    head_dots_a(heads[0])

    for hh in range(B_HEADS):
        mlp_piece(3 + hh)
        if hh == 0:
            for g in range(A_GROUPS):
                gc = slice(g * A_DIM, (g + 1) * A_DIM)
                bias = r.bias[g]
                for ci, c in enumerate(chunk_rows):
                    r.mix[c, gc] = (ua[g][c] * (mixed[g][ci] + bias)).astype(BF16)
        else:
            head_finish(hh - 1, heads[hh - 1])
        head_chain(hh, heads[hh])
        if hh + 1 < B_HEADS:
            heads.append(head_inputs(hh + 1))
        head_dots_b(heads[hh])
        if hh + 1 < B_HEADS:
            head_dots_a(heads[hh + 1])

    mlp_piece(7)
    head_finish(B_HEADS - 1, heads[B_HEADS - 1])
    half = ROW_TILE // 2
    mix_a = _dot(r.mix[:half, :], r.w_out[...])
    mlp_finish()
    mix_b = _dot(r.mix[half:, :], r.w_out[...])
    for rows, mix in ((slice(0, half), mix_a), (slice(half, ROW_TILE), mix_b)):
        x1 = x[rows] + _rms(mix, r.post_mix[...])
        r.x1[rows, :] = x1
        r.hn[rows, :] = _rms(x1, r.pre_mlp[...]).astype(BF16)


def _pack_params(pre_mix_w, post_mix_w, pre_mlp_w, post_mlp_w, ln_v_w, ln_v_b, gn_w, b_s):
    rows = [pre_mix_w, post_mix_w, pre_mlp_w, post_mlp_w,
            jnp.concatenate([ln_v_w.reshape(-1), ln_v_b.reshape(-1)]),
            jnp.concatenate([gn_w.reshape(-1), b_s.reshape(-1)])]
    rows += [jnp.zeros_like(pre_mix_w)] * (PARAM_ROWS - len(rows))
    return jnp.stack(rows)


def _resident(shape):
    zeros = (0,) * len(shape)
    return pl.BlockSpec(shape, lambda i: zeros, pipeline_mode=pl.Buffered(1))


def _layer(x, xs, s0, p):
    batch, seq, _ = x.shape
    n_dec = xs.shape[0]
    tiles_per_seq = seq // ROW_TILE
    n_tiles = batch * tiles_per_seq
    assert n_dec == n_tiles * DEC_ROWS and DEC_GROUP % DEC_ROWS == 0

    def mixer_tile(i):
        return jnp.minimum(i, n_tiles - 1)

    def mlp_tile(i):
        return jnp.maximum(i - 1, 0)

    x_spec = pl.BlockSpec((None, ROW_TILE, D_MODEL),
                          lambda i: (mixer_tile(i) // tiles_per_seq, mixer_tile(i) % tiles_per_seq, 0))
    y_spec = pl.BlockSpec((None, ROW_TILE, D_MODEL),
                          lambda i: (mlp_tile(i) // tiles_per_seq, mlp_tile(i) % tiles_per_seq, 0))
    tok_spec = pl.BlockSpec((n_dec, None, D_MODEL), lambda i: (0, 0, 0), pipeline_mode=pl.Buffered(1))
    ys_spec = pl.BlockSpec((n_dec, None, D_MODEL), lambda i: (0, 0, 0))
    va_spec = pl.BlockSpec((n_dec, None, A_GROUPS, A_DIM), lambda i: (0, 0, 0, 0))
    dec_state_spec = pl.BlockSpec((DEC_ROWS, B_HEADS, B_DK, B_DV), lambda i: (mixer_tile(i), 0, 0, 0))
    hbm = pl.BlockSpec(memory_space=pl.ANY)
    in_specs = [
        x_spec, tok_spec, dec_state_spec,
        _resident((PARAM_ROWS, D_MODEL)),
        _resident((A_GROUPS, CHUNK, CHUNK)),
        hbm, hbm, hbm, hbm,
    ]
    out_specs = [
        y_spec,
        pl.BlockSpec((None, B_HEADS, B_DK, B_DV), lambda i: (mixer_tile(i) // tiles_per_seq, 0, 0, 0)),
        ys_spec, dec_state_spec, va_spec,
    ]
    out_shape = [
        jax.ShapeDtypeStruct(x.shape, F32),
        jax.ShapeDtypeStruct((batch, B_HEADS, B_DK, B_DV), F32),
        jax.ShapeDtypeStruct((n_dec, 1, D_MODEL), F32),
        jax.ShapeDtypeStruct(s0.shape, F32),
        jax.ShapeDtypeStruct((n_dec, 1, A_GROUPS, A_DIM), F32),
    ]
    scratch_shapes = [
        pltpu.VMEM((D_MODEL, IN_COLS), BF16),
        pltpu.VMEM((D_MODEL, D_MODEL), BF16),
        pltpu.VMEM((D_FF // FF_TILE, D_MODEL, FF_TILE), BF16),
        pltpu.VMEM((D_FF, D_MODEL), BF16),
        pltpu.VMEM((ROW_TILE, D_MODEL), BF16),
        pltpu.VMEM((ROW_TILE, D_MODEL), F32),
        pltpu.VMEM((ROW_TILE, D_MODEL), BF16),
        pltpu.VMEM((seq, B_DK), F32),
        pltpu.VMEM((seq, B_DK), F32),
        pltpu.VMEM((B_HEADS, CHUNK, CHUNK), F32),
        pltpu.VMEM((B_HEADS, CHUNK, B_DK), F32),
        pltpu.VMEM((B_HEADS, CHUNK, B_DK), F32),
        pltpu.VMEM((n_dec, B_WIDTH), F32),
        pltpu.VMEM((n_dec, B_WIDTH), F32),
        pltpu.VMEM((n_dec, B_WIDTH), F32),
        pltpu.VMEM((n_dec, B_WIDTH), F32),
        pltpu.VMEM((n_dec, A_WIDTH), F32),
        pltpu.VMEM((n_dec, B_WIDTH), F32),
        pltpu.VMEM((A_GROUPS, CHUNK, A_DIM), F32),
    ]
    return pl.pallas_call(
        functools.partial(_layer_kernel, tiles_per_seq=tiles_per_seq, n_tiles=n_tiles),
        grid=(n_tiles + 1,),
        in_specs=in_specs,
        out_specs=out_specs,
        out_shape=out_shape,
        scratch_shapes=scratch_shapes,
        compiler_params=pltpu.CompilerParams(
            dimension_semantics=("arbitrary",),
            vmem_limit_bytes=VMEM_LIMIT_BYTES),
        name="hybrid_layer",
    )(x, xs, s0, p["packed"], p["w_s"],
      p["w_in"], p["w_out"], p["w_up"], p["w_down"])


def kernel(x_prompt, x_sample, state_ret, pre_mix_w, w_in, ln_v_w, ln_v_b, w_s, b_s, gn_w,
           w_out, post_mix_w, pre_mlp_w, w_up, w_down, post_mlp_w):
    depth = w_in.shape[0]
    seq = x_prompt.shape[1]
    assert x_sample.shape[1] == 1 and seq % ROW_TILE == 0

    yp, ys = x_prompt, x_sample
    sp_list, ss_list, vs_list = [], [], []
    for l in range(depth):
        p = {
            "packed": _pack_params(pre_mix_w[l], post_mix_w[l], pre_mlp_w[l], post_mlp_w[l],
                                   ln_v_w[l], ln_v_b[l], gn_w[l], b_s[l]),
            "w_s": w_s[l],
            "w_in": w_in[l],
            "w_out": w_out[l],
            "w_up": w_up[l],
            "w_down": w_down[l],
        }
        yp, sp, ys, ss, vs = _layer(yp, ys, state_ret[l], p)
        sp_list.append(sp)
        ss_list.append(ss)
        vs_list.append(vs)
    return (yp, ys, jnp.stack(sp_list, axis=0), jnp.stack(ss_list, axis=0), jnp.stack(vs_list, axis=0))
```

```python
import functools
import math
from typing import Any, NamedTuple

import jax
import jax.numpy as jnp
from jax import lax
from jax.experimental import pallas as pl
from jax.experimental.pallas import tpu as pltpu

D_MODEL = 1024
A_WIDTH = 512
B_WIDTH = 512
A_GROUPS = 4
A_DIM = 128
CHUNK = 128
B_HEADS = 4
B_DK = 128
B_DV = 128
D_FF = 4096
IN_COLS = 2 * A_WIDTH + 4 * B_WIDTH
ROPE_BASE = 10000.0
EPS = 1e-6
PAST_LEN = 16384
LOG_GAMMA = [math.log(1.0 - 2.0 ** (-5.0 - h)) for h in range(B_HEADS)]

ROW_TILE = 512
FF_TILE = 1024
MLP_PIECES = 2 * (D_FF // FF_TILE)
DEC_ROWS = 4
DEC_GROUP = 16
STAGE_BYTES = 1 << 20
STAGE_SLOTS = 6
VMEM_LIMIT_BYTES = 60 * 1024 * 1024

F32 = jnp.float32
BF16 = jnp.bfloat16


def _rms(x, w):
    return x * lax.rsqrt(jnp.mean(x * x, axis=-1, keepdims=True) + EPS) * w


def _center_norm(x):
    mu = jnp.mean(x, axis=-1, keepdims=True)
    xc = x - mu
    return xc * lax.rsqrt(jnp.mean(xc * xc, axis=-1, keepdims=True) + EPS)


def _gelu(x):
    return jax.nn.gelu(x, approximate=True)


def _silu(x):
    return x * (1.0 / (1.0 + jnp.exp(-x)))


def _rope(x, cosf, sinf):
    return x * cosf + pltpu.roll(x, B_DK // 2, axis=1) * sinf


def _rope_rows(pos):
    lane = lax.broadcasted_iota(jnp.int32, pos.shape, 1)
    j = lax.rem(lane, B_DK // 2).astype(F32)
    ang = pos * jnp.exp(j * (-math.log(ROPE_BASE) / (B_DK // 2)))
    sin = jnp.sin(ang)
    return jnp.cos(ang), jnp.where(lane < B_DK // 2, -sin, sin)


def _dot(a, b):
    return jnp.dot(a, b, preferred_element_type=F32)


def _dot_nt(a, b):
    return lax.dot_general(a, b, (((1,), (1,)), ((), ())), preferred_element_type=F32)


def _dot_tn(a, b):
    return lax.dot_general(a, b, (((0,), (0,)), ((), ())), preferred_element_type=F32)


class _Refs(NamedTuple):
    x: Any
    xs: Any
    s0: Any
    pre_mix: Any
    lnw: Any
    lnb: Any
    ws: Any
    bsr: Any
    gn: Any
    post_mix: Any
    pre_mlp: Any
    post_mlp: Any
    w_in_hbm: Any
    w_out_hbm: Any
    w_up_hbm: Any
    w_down_hbm: Any
    y: Any
    s: Any
    ys: Any
    s_dec: Any
    va: Any
    w_in: Any
    w_out: Any
    w_up: Any
    w_down: Any
    mix: Any
    x1: Any
    hn: Any
    cos: Any
    sin: Any
    mask: Any
    qdec: Any
    kdec: Any
    dq: Any
    dk: Any
    dv: Any
    dgate: Any
    douta: Any
    do: Any
    bias: Any


def _convert_weight(w_hbm, w_bf, col_pieces=1, meanwhile=None):
    n_rows, n_cols = w_hbm.shape
    piece = n_cols // col_pieces
    rows_per = 1 << int(math.log2(STAGE_BYTES // (4 * n_cols)))
    assert n_rows % rows_per == 0 and rows_per % 16 == 0
    n_chunks = n_rows // rows_per

    ahead = STAGE_SLOTS - 1

    def body(stage, sem):
        def copy(c):
            slot = lax.rem(c, STAGE_SLOTS)
            return pltpu.make_async_copy(
                w_hbm.at[pl.ds(c * rows_per, rows_per), :], stage.at[slot], sem.at[slot])

        for c in range(min(ahead, n_chunks)):
            copy(c).start()
        if meanwhile is not None:
            meanwhile()

        def step(c, carry):
            slot = lax.rem(c, STAGE_SLOTS)

            @pl.when(c + ahead < n_chunks)
            def _():
                copy(c + ahead).start()

            copy(c).wait()
            rows = pl.ds(pl.multiple_of(c * rows_per, rows_per), rows_per)
            if col_pieces == 1:
                w_bf[rows, :] = stage[slot].astype(BF16)
            else:
                for p in range(col_pieces):
                    w_bf[p, rows, :] = stage[slot, :, p * piece:(p + 1) * piece].astype(BF16)
            return carry

        lax.fori_loop(0, n_chunks, step, 0)

    pl.run_scoped(body, pltpu.VMEM((STAGE_SLOTS, rows_per, n_cols), F32),
                  pltpu.SemaphoreType.DMA((STAGE_SLOTS,)))


def _fill_rope_tables(r, part, n_parts):
    blocks = r.cos.shape[0] // ROW_TILE // n_parts

    def rope_block(b, carry):
        r0 = pl.multiple_of(b * ROW_TILE, ROW_TILE)
        pos = (lax.broadcasted_iota(jnp.int32, (ROW_TILE, B_DK), 0) + r0).astype(F32)
        cosf, sinf = _rope_rows(pos)
        r.cos[pl.ds(r0, ROW_TILE), :] = cosf
        r.sin[pl.ds(r0, ROW_TILE), :] = sinf
        return carry

    lax.fori_loop(part * blocks, (part + 1) * blocks, rope_block, 0)


def _fill_decay_tables(r):
    row = lax.broadcasted_iota(jnp.int32, (CHUNK, CHUNK), 0).astype(F32)
    col = lax.broadcasted_iota(jnp.int32, (CHUNK, CHUNK), 1).astype(F32)
    diff = row - col
    for g in range(A_GROUPS):
        on_diag = jnp.where(diff == 0, r.bsr[g:g + 1, :], 0.0)
        r.bias[g] = jnp.broadcast_to(jnp.sum(on_diag, axis=1, keepdims=True), (CHUNK, A_DIM))
    for hh in range(B_HEADS):
        lg = LOG_GAMMA[hh]
        r.mask[hh] = jnp.where(diff >= 0, jnp.exp(lg * jnp.maximum(diff, 0.0)), 0.0)
        r.qdec[hh] = jnp.exp(lg * (row + 1.0))
        r.kdec[hh] = jnp.exp(lg * (CHUNK - 1.0 - row))


def _decode_projections(r):
    h = _rms(r.xs[...], r.pre_mix[...]).astype(BF16)
    z = _dot(h, r.w_in[...])
    cosf, sinf = _rope_rows(jnp.full((1, B_DK), PAST_LEN, F32))
    for g in range(A_GROUPS):
        gc = slice(g * A_DIM, (g + 1) * A_DIM)
        ua = _gelu(z[:, g * A_DIM:(g + 1) * A_DIM])
        va = _gelu(z[:, A_WIDTH + g * A_DIM:A_WIDTH + (g + 1) * A_DIM])
        va = _center_norm(va) * r.lnw[:, gc] + r.lnb[:, gc]
        r.va[:, g, :] = va
        r.douta[:, gc] = ua * (r.ws[g][0:1, 0:1] * va + r.bsr[g:g + 1, 0:1])
    base = 2 * A_WIDTH
    for hh in range(B_HEADS):
        hc = slice(hh * B_DK, (hh + 1) * B_DK)
        r.dq[:, hc] = _rope(z[:, base + hh * B_DK:base + (hh + 1) * B_DK], cosf, sinf)
        r.dk[:, hc] = _rope(z[:, base + B_WIDTH + hh * B_DK:base + B_WIDTH + (hh + 1) * B_DK],
                            cosf, sinf) * (B_DK ** -0.5)
    r.dv[...] = z[:, base + 2 * B_WIDTH:base + 3 * B_WIDTH]
    r.dgate[...] = _silu(z[:, base + 3 * B_WIDTH:])
    r.do[...] = jnp.zeros_like(r.do)


def _decode_state_step(r):
    i = pl.program_id(0)
    per_group = DEC_GROUP // DEC_ROWS
    rows = pl.ds(pl.multiple_of((i // per_group) * DEC_GROUP, DEC_GROUP), DEC_GROUP)
    first = lax.rem(i, per_group) * DEC_ROWS
    rid = lax.broadcasted_iota(jnp.int32, (DEC_GROUP, B_DK), 0)
    live = jnp.logical_and(rid >= first, rid < first + DEC_ROWS)
    for hh in range(B_HEADS):
        hc = slice(hh * B_DK, (hh + 1) * B_DK)
        q = r.dq[rows, hc]
        k = r.dk[rows, hc]
        v = r.dv[rows, hc]
        gam = math.exp(LOG_GAMMA[hh])
        qk = jnp.sum(q * k, axis=-1, keepdims=True)
        qg = (q * gam).astype(BF16)
        v_sel = jnp.concatenate([jnp.where(rid == first + j, v, 0.0) for j in range(DEC_ROWS)],
                                axis=-1).astype(BF16)
        kv = _dot_tn(k.astype(BF16), v_sel)
        one_sel = jnp.concatenate([jnp.where(rid == first + j, 1.0, 0.0) for j in range(DEC_ROWS)],
                                  axis=-1).astype(BF16)
        q_cols = _dot_tn(qg, one_sel)
        cross = jnp.zeros((DEC_GROUP, B_DV), F32)
        for j in range(DEC_ROWS):
            s = r.s0[j, hh]
            cross_j = jnp.sum(q_cols[:, j * B_DV:(j + 1) * B_DV] * s, axis=0, keepdims=True)
            cross = jnp.where(rid == first + j, cross_j, cross)
            r.s_dec[j, hh] = s * gam + kv[:, j * B_DV:(j + 1) * B_DV]
        r.do[rows, hc] = jnp.where(live, qk * v + cross, r.do[rows, hc])


def _last_step(r):
    parts = [r.douta[...].astype(BF16)]
    for hh in range(B_HEADS):
        hc = slice(hh * B_DV, (hh + 1) * B_DV)
        o = _center_norm(r.do[:, hc]) * r.gn[:, hc]
        parts.append((o * r.dgate[:, hc]).astype(BF16))
    mix = _dot(jnp.concatenate(parts, axis=-1), r.w_out[...])
    x1_dec = r.xs[...] + _rms(mix, r.post_mix[...])
    hn = jnp.concatenate([r.hn[...], _rms(x1_dec, r.pre_mlp[...]).astype(BF16)], axis=0)

    def block(j, f):
        u = jnp.maximum(_dot(hn, r.w_up[j]), 0.0)
        rows = pl.ds(pl.multiple_of(j * FF_TILE, FF_TILE), FF_TILE)
        return f + _dot((u * u).astype(BF16), r.w_down[rows, :])

    f = lax.fori_loop(0, D_FF // FF_TILE, block, jnp.zeros((hn.shape[0], D_MODEL), F32))
    r.y[...] = r.x1[...] + _rms(f[:ROW_TILE], r.post_mlp[...])
    r.ys[...] = x1_dec + _rms(f[ROW_TILE:], r.post_mlp[...])


def _layer_kernel(*refs, tiles_per_seq, n_tiles):
    r = _Refs(*refs)
    i = pl.program_id(0)

    @pl.when(i == 0)
    def _():
        _convert_weight(r.w_in_hbm, r.w_in, meanwhile=lambda: _fill_rope_tables(r, 0, 2))
        _convert_weight(r.w_out_hbm, r.w_out, meanwhile=lambda: _fill_rope_tables(r, 1, 2))
        _convert_weight(r.w_up_hbm, r.w_up, col_pieces=D_FF // FF_TILE,
                        meanwhile=lambda: _decode_projections(r))
        _convert_weight(r.w_down_hbm, r.w_down, meanwhile=lambda: _fill_decay_tables(r))

    @pl.when(jnp.logical_and(lax.rem(i, tiles_per_seq) == 0, i < n_tiles))
    def _():
        r.s[...] = jnp.zeros_like(r.s)

    @pl.when(i == 0)
    def _():
        _prompt_step(r, tiles_per_seq=tiles_per_seq, with_mlp=False)

    @pl.when(jnp.logical_and(i > 0, i < n_tiles))
    def _():
        _prompt_step(r, tiles_per_seq=tiles_per_seq, with_mlp=True)

    @pl.when(i == n_tiles)
    def _():
        _last_step(r)


def _prompt_step(r, *, tiles_per_seq, with_mlp):
    n_chunks = ROW_TILE // CHUNK
    chunk_rows = [slice(c * CHUNK, (c + 1) * CHUNK) for c in range(n_chunks)]
    acc = []
    hidden = []

    def mlp_piece(k):
        if not with_mlp:
            return
        j = k // 2
        if k % 2 == 0:
            u = jnp.maximum(_dot(r.hn[...], r.w_up[j]), 0.0)
            hidden.append((u * u).astype(BF16))
        else:
            part = _dot(hidden.pop(), r.w_down[j * FF_TILE:(j + 1) * FF_TILE, :])
            acc[:] = [part if not acc else acc[0] + part]

    def mlp_finish():
        if with_mlp:
            r.y[...] = r.x1[...] + _rms(acc[0], r.post_mlp[...])

    x = r.x[...]
    mlp_piece(0)
    h = _rms(x, r.pre_mix[...]).astype(BF16)
    split = 2 * A_WIDTH + 2 * B_WIDTH
    z1 = _dot(h, r.w_in[:, :split])
    _decode_state_step(r)
    z2 = _dot(h, r.w_in[:, split:])
    tile_pos = pl.multiple_of(lax.rem(pl.program_id(0), tiles_per_seq) * ROW_TILE, ROW_TILE)
    cosf = r.cos[pl.ds(tile_pos, ROW_TILE), :]
    sinf = r.sin[pl.ds(tile_pos, ROW_TILE), :]

    def head_inputs(hh):
        base = 2 * A_WIDTH + hh * B_DK
        q = _rope(z1[:, base:base + B_DK], cosf, sinf)
        k = _rope(z1[:, base + B_WIDTH:base + B_WIDTH + B_DK], cosf, sinf) * (B_DK ** -0.5)
        v = z2[:, hh * B_DV:(hh + 1) * B_DV].astype(BF16)
        return {
            "q": q.astype(BF16),
            "k": k.astype(BF16),
            "v": v,
            "qd": [(q[c] * r.qdec[hh]).astype(BF16) for c in chunk_rows],
            "kd": [(k[c] * r.kdec[hh]).astype(BF16) for c in chunk_rows],
        }

    def head_dots_a(hd):
        hd["scores"] = [_dot_nt(hd["q"][c], hd["k"][c]) for c in chunk_rows]
        hd["kv"] = [_dot_tn(hd["kd"][ci], hd["v"][c]) for ci, c in enumerate(chunk_rows)]

    def head_chain(hh, hd):
        s = r.s[hh]
        hd["s"] = []
        for ci in range(n_chunks):
            hd["s"].append(s.astype(BF16))
            s = s * math.exp(LOG_GAMMA[hh] * CHUNK) + hd["kv"][ci]
        r.s[hh] = s
        hd["p"] = [(hd["scores"][ci] * r.mask[hh]).astype(BF16) for ci in range(n_chunks)]

    def head_dots_b(hd):
        hd["o"] = [_dot(hd["p"][ci], hd["v"][c]) + _dot(hd["qd"][ci], hd["s"][ci])
                   for ci, c in enumerate(chunk_rows)]

    def head_finish(hh, hd):
        base = B_WIDTH + hh * B_DV
        gate = _silu(z2[:, base:base + B_DV])
        gnw = r.gn[:, hh * B_DV:(hh + 1) * B_DV]
        for ci, c in enumerate(chunk_rows):
            o = _center_norm(hd["o"][ci]) * gnw
            r.mix[c, A_WIDTH + hh * B_DV:A_WIDTH + (hh + 1) * B_DV] = (o * gate[c]).astype(BF16)

    mlp_piece(1)
    ua, va = [], []
    for g in range(A_GROUPS):
        gc = slice(g * A_DIM, (g + 1) * A_DIM)
        ua.append(_gelu(z1[:, g * A_DIM:(g + 1) * A_DIM]))
        vg = _gelu(z1[:, A_WIDTH + g * A_DIM:A_WIDTH + (g + 1) * A_DIM])
        va.append((_center_norm(vg) * r.lnw[:, gc] + r.lnb[:, gc]).astype(BF16))
        if g == 1:
            mlp_piece(2)
    heads = [head_inputs(0)]

    row = lax.broadcasted_iota(jnp.int32, (CHUNK, CHUNK), 0)
    col = lax.broadcasted_iota(jnp.int32, (CHUNK, CHUNK), 1)
    mixed = []
    for g in range(A_GROUPS):
        w_tril = jnp.where(row >= col, r.ws[g], 0.0).astype(BF16)
        mixed.append([_dot(w_tril, va[g][c]) for c in chunk_rows])
    head_dots_a(heads[0])

    for hh in range(B_HEADS):
        mlp_piece(3 + hh)
        if hh == 0:
            for g in range(A_GROUPS):
                gc = slice(g * A_DIM, (g + 1) * A_DIM)
                bias = r.bias[g]
                for ci, c in enumerate(chunk_rows):
                    r.mix[c, gc] = (ua[g][c] * (mixed[g][ci] + bias)).astype(BF16)
        else:
            head_finish(hh - 1, heads[hh - 1])
        head_chain(hh, heads[hh])
        if hh + 1 < B_HEADS:
            heads.append(head_inputs(hh + 1))
        head_dots_b(heads[hh])
        if hh + 1 < B_HEADS:
            head_dots_a(heads[hh + 1])

    mlp_piece(7)
    head_finish(B_HEADS - 1, heads[B_HEADS - 1])
    half = ROW_TILE // 2
    mix_a = _dot(r.mix[:half, :], r.w_out[...])
    mlp_finish()
    mix_b = _dot(r.mix[half:, :], r.w_out[...])
    for rows, mix in ((slice(0, half), mix_a), (slice(half, ROW_TILE), mix_b)):
        x1 = x[rows] + _rms(mix, r.post_mix[...])
        r.x1[rows, :] = x1
        r.hn[rows, :] = _rms(x1, r.pre_mlp[...]).astype(BF16)


def _resident(shape):
    zeros = (0,) * len(shape)
    return pl.BlockSpec(shape, lambda i: zeros, pipeline_mode=pl.Buffered(1))


def _layer(x, xs, s0, p):
    batch, seq, _ = x.shape
    n_dec = xs.shape[0]
    tiles_per_seq = seq // ROW_TILE
    n_tiles = batch * tiles_per_seq
    assert n_dec == n_tiles * DEC_ROWS and DEC_GROUP % DEC_ROWS == 0

    def mixer_tile(i):
        return jnp.minimum(i, n_tiles - 1)

    def mlp_tile(i):
        return jnp.maximum(i - 1, 0)

    x_spec = pl.BlockSpec((None, ROW_TILE, D_MODEL),
                          lambda i: (mixer_tile(i) // tiles_per_seq, mixer_tile(i) % tiles_per_seq, 0))
    y_spec = pl.BlockSpec((None, ROW_TILE, D_MODEL),
                          lambda i: (mlp_tile(i) // tiles_per_seq, mlp_tile(i) % tiles_per_seq, 0))
    tok_spec = pl.BlockSpec((n_dec, None, D_MODEL), lambda i: (0, 0, 0), pipeline_mode=pl.Buffered(1))
    ys_spec = pl.BlockSpec((n_dec, None, D_MODEL), lambda i: (0, 0, 0))
    va_spec = pl.BlockSpec((n_dec, None, A_GROUPS, A_DIM), lambda i: (0, 0, 0, 0))
    dec_state_spec = pl.BlockSpec((DEC_ROWS, B_HEADS, B_DK, B_DV), lambda i: (mixer_tile(i), 0, 0, 0))
    hbm = pl.BlockSpec(memory_space=pl.ANY)
    in_specs = [
        x_spec, tok_spec, dec_state_spec,
        _resident((1, D_MODEL)),
        _resident((1, A_WIDTH)), _resident((1, A_WIDTH)),
        _resident((A_GROUPS, CHUNK, CHUNK)), _resident((A_GROUPS, CHUNK)),
        _resident((1, B_WIDTH)),
        _resident((1, D_MODEL)), _resident((1, D_MODEL)), _resident((1, D_MODEL)),
        hbm, hbm, hbm, hbm,
    ]
    out_specs = [
        y_spec,
        pl.BlockSpec((None, B_HEADS, B_DK, B_DV), lambda i: (mixer_tile(i) // tiles_per_seq, 0, 0, 0)),
        ys_spec, dec_state_spec, va_spec,
    ]
    out_shape = [
        jax.ShapeDtypeStruct(x.shape, F32),
        jax.ShapeDtypeStruct((batch, B_HEADS, B_DK, B_DV), F32),
        jax.ShapeDtypeStruct((n_dec, 1, D_MODEL), F32),
        jax.ShapeDtypeStruct(s0.shape, F32),
        jax.ShapeDtypeStruct((n_dec, 1, A_GROUPS, A_DIM), F32),
    ]
    scratch_shapes = [
        pltpu.VMEM((D_MODEL, IN_COLS), BF16),
        pltpu.VMEM((D_MODEL, D_MODEL), BF16),
        pltpu.VMEM((D_FF // FF_TILE, D_MODEL, FF_TILE), BF16),
        pltpu.VMEM((D_FF, D_MODEL), BF16),
        pltpu.VMEM((ROW_TILE, D_MODEL), BF16),
        pltpu.VMEM((ROW_TILE, D_MODEL), F32),
        pltpu.VMEM((ROW_TILE, D_MODEL), BF16),
        pltpu.VMEM((seq, B_DK), F32),
        pltpu.VMEM((seq, B_DK), F32),
        pltpu.VMEM((B_HEADS, CHUNK, CHUNK), F32),
        pltpu.VMEM((B_HEADS, CHUNK, B_DK), F32),
        pltpu.VMEM((B_HEADS, CHUNK, B_DK), F32),
        pltpu.VMEM((n_dec, B_WIDTH), F32),
        pltpu.VMEM((n_dec, B_WIDTH), F32),
        pltpu.VMEM((n_dec, B_WIDTH), F32),
        pltpu.VMEM((n_dec, B_WIDTH), F32),
        pltpu.VMEM((n_dec, A_WIDTH), F32),
        pltpu.VMEM((n_dec, B_WIDTH), F32),
        pltpu.VMEM((A_GROUPS, CHUNK, A_DIM), F32),
    ]
    return pl.pallas_call(
        functools.partial(_layer_kernel, tiles_per_seq=tiles_per_seq, n_tiles=n_tiles),
        grid=(n_tiles + 1,),
        in_specs=in_specs,
        out_specs=out_specs,
        out_shape=out_shape,
        scratch_shapes=scratch_shapes,
        compiler_params=pltpu.CompilerParams(
            dimension_semantics=("arbitrary",),
            vmem_limit_bytes=VMEM_LIMIT_BYTES),
        name="hybrid_layer",
    )(x, xs, s0, p["pre_mix_w"], p["ln_v_w"], p["ln_v_b"], p["w_s"], p["b_s"],
      p["gn_w"], p["post_mix_w"], p["pre_mlp_w"], p["post_mlp_w"],
      p["w_in"], p["w_out"], p["w_up"], p["w_down"])


def kernel(x_prompt, x_sample, state_ret, pre_mix_w, w_in, ln_v_w, ln_v_b, w_s, b_s, gn_w,
           w_out, post_mix_w, pre_mlp_w, w_up, w_down, post_mlp_w):
    depth = w_in.shape[0]
    seq = x_prompt.shape[1]
    assert x_sample.shape[1] == 1 and seq % ROW_TILE == 0

    yp, ys = x_prompt, x_sample
    sp_list, ss_list, vs_list = [], [], []
    for l in range(depth):
        p = {
            "pre_mix_w": pre_mix_w[l][None, :],
            "w_in": w_in[l],
            "ln_v_w": ln_v_w[l].reshape(1, A_WIDTH),
            "ln_v_b": ln_v_b[l].reshape(1, A_WIDTH),
            "w_s": w_s[l],
            "b_s": b_s[l],
            "gn_w": gn_w[l].reshape(1, B_WIDTH),
            "w_out": w_out[l],
            "post_mix_w": post_mix_w[l][None, :],
            "pre_mlp_w": pre_mlp_w[l][None, :],
            "w_up": w_up[l],
            "w_down": w_down[l],
            "post_mlp_w": post_mlp_w[l][None, :],
        }
        yp, sp, ys, ss, vs = _layer(yp, ys, state_ret[l], p)
        sp_list.append(sp)
        ss_list.append(ss)
        vs_list.append(vs)
    return (yp, ys, jnp.stack(sp_list, axis=0), jnp.stack(ss_list, axis=0), jnp.stack(vs_list, axis=0))
```

```python
import functools
import math
from typing import Any, NamedTuple

import jax
import jax.numpy as jnp
from jax import lax
from jax.experimental import pallas as pl
from jax.experimental.pallas import tpu as pltpu

D_MODEL = 1024
A_WIDTH = 512
B_WIDTH = 512
A_GROUPS = 4
A_DIM = 128
CHUNK = 128
B_HEADS = 4
B_DK = 128
B_DV = 128
D_FF = 4096
IN_COLS = 2 * A_WIDTH + 4 * B_WIDTH
ROPE_BASE = 10000.0
EPS = 1e-6
PAST_LEN = 16384
LOG_GAMMA = [math.log(1.0 - 2.0 ** (-5.0 - h)) for h in range(B_HEADS)]

ROW_TILE = 512
FF_TILE = 1024
MLP_PIECES = 2 * (D_FF // FF_TILE)
DEC_ROWS = 4
DEC_GROUP = 16
STAGE_BYTES = 1 << 20
STAGE_SLOTS = 10
VMEM_LIMIT_BYTES = 60 * 1024 * 1024

F32 = jnp.float32
BF16 = jnp.bfloat16


def _rms(x, w):
    return x * lax.rsqrt(jnp.mean(x * x, axis=-1, keepdims=True) + EPS) * w


def _center_norm(x):
    mu = jnp.mean(x, axis=-1, keepdims=True)
    xc = x - mu
    return xc * lax.rsqrt(jnp.mean(xc * xc, axis=-1, keepdims=True) + EPS)


def _gelu(x):
    return jax.nn.gelu(x, approximate=True)


def _silu(x):
    return x * (1.0 / (1.0 + jnp.exp(-x)))


def _rope(x, cosf, sinf):
    return x * cosf + pltpu.roll(x, B_DK // 2, axis=1) * sinf


def _rope_rows(pos):
    lane = lax.broadcasted_iota(jnp.int32, pos.shape, 1)
    j = lax.rem(lane, B_DK // 2).astype(F32)
    ang = pos * jnp.exp(j * (-math.log(ROPE_BASE) / (B_DK // 2)))
    sin = jnp.sin(ang)
    return jnp.cos(ang), jnp.where(lane < B_DK // 2, -sin, sin)


def _dot(a, b):
    return jnp.dot(a, b, preferred_element_type=F32)


def _dot_nt(a, b):
    return lax.dot_general(a, b, (((1,), (1,)), ((), ())), preferred_element_type=F32)


def _dot_tn(a, b):
    return lax.dot_general(a, b, (((0,), (0,)), ((), ())), preferred_element_type=F32)


class _Refs(NamedTuple):
    x: Any
    xs: Any
    s0: Any
    pre_mix: Any
    lnw: Any
    lnb: Any
    ws: Any
    bsr: Any
    gn: Any
    post_mix: Any
    pre_mlp: Any
    post_mlp: Any
    w_in_hbm: Any
    w_out_hbm: Any
    w_up_hbm: Any
    w_down_hbm: Any
    y: Any
    s: Any
    ys: Any
    s_dec: Any
    va: Any
    w_in: Any
    w_out: Any
    w_up: Any
    w_down: Any
    mix: Any
    x1: Any
    hn: Any
    cos: Any
    sin: Any
    mask: Any
    qdec: Any
    kdec: Any
    dq: Any
    dk: Any
    dv: Any
    dgate: Any
    douta: Any
    do: Any
    bias: Any


def _convert_weight(w_hbm, w_bf, col_pieces=1, meanwhile=None):
    n_rows, n_cols = w_hbm.shape
    piece = n_cols // col_pieces
    rows_per = 1 << int(math.log2(STAGE_BYTES // (4 * n_cols)))
    assert n_rows % rows_per == 0 and rows_per % 16 == 0
    n_chunks = n_rows // rows_per

    ahead = STAGE_SLOTS - 1

    def body(stage, sem):
        def copy(c):
            slot = lax.rem(c, STAGE_SLOTS)
            return pltpu.make_async_copy(
                w_hbm.at[pl.ds(c * rows_per, rows_per), :], stage.at[slot], sem.at[slot])

        for c in range(min(ahead, n_chunks)):
            copy(c).start()
        if meanwhile is not None:
            meanwhile()

        def step(c, carry):
            slot = lax.rem(c, STAGE_SLOTS)

            @pl.when(c + ahead < n_chunks)
            def _():
                copy(c + ahead).start()

            copy(c).wait()
            rows = pl.ds(pl.multiple_of(c * rows_per, rows_per), rows_per)
            if col_pieces == 1:
                w_bf[rows, :] = stage[slot].astype(BF16)
            else:
                for p in range(col_pieces):
                    w_bf[p, rows, :] = stage[slot, :, p * piece:(p + 1) * piece].astype(BF16)
            return carry

        lax.fori_loop(0, n_chunks, step, 0)

    pl.run_scoped(body, pltpu.VMEM((STAGE_SLOTS, rows_per, n_cols), F32),
                  pltpu.SemaphoreType.DMA((STAGE_SLOTS,)))


def _fill_rope_tables(r, part, n_parts):
    blocks = r.cos.shape[0] // ROW_TILE // n_parts

    def rope_block(b, carry):
        r0 = pl.multiple_of(b * ROW_TILE, ROW_TILE)
        pos = (lax.broadcasted_iota(jnp.int32, (ROW_TILE, B_DK), 0) + r0).astype(F32)
        cosf, sinf = _rope_rows(pos)
        r.cos[pl.ds(r0, ROW_TILE), :] = cosf
        r.sin[pl.ds(r0, ROW_TILE), :] = sinf
        return carry

    lax.fori_loop(part * blocks, (part + 1) * blocks, rope_block, 0)


def _fill_decay_tables(r):
    row = lax.broadcasted_iota(jnp.int32, (CHUNK, CHUNK), 0).astype(F32)
    col = lax.broadcasted_iota(jnp.int32, (CHUNK, CHUNK), 1).astype(F32)
    diff = row - col
    for g in range(A_GROUPS):
        on_diag = jnp.where(diff == 0, r.bsr[g:g + 1, :], 0.0)
        r.bias[g] = jnp.broadcast_to(jnp.sum(on_diag, axis=1, keepdims=True), (CHUNK, A_DIM))
    for hh in range(B_HEADS):
        lg = LOG_GAMMA[hh]
        r.mask[hh] = jnp.where(diff >= 0, jnp.exp(lg * jnp.maximum(diff, 0.0)), 0.0)
        r.qdec[hh] = jnp.exp(lg * (row + 1.0))
        r.kdec[hh] = jnp.exp(lg * (CHUNK - 1.0 - row))


def _decode_projections(r):
    h = _rms(r.xs[...], r.pre_mix[...]).astype(BF16)
    z = _dot(h, r.w_in[...])
    cosf, sinf = _rope_rows(jnp.full((1, B_DK), PAST_LEN, F32))
    for g in range(A_GROUPS):
        gc = slice(g * A_DIM, (g + 1) * A_DIM)
        ua = _gelu(z[:, g * A_DIM:(g + 1) * A_DIM])
        va = _gelu(z[:, A_WIDTH + g * A_DIM:A_WIDTH + (g + 1) * A_DIM])
        va = _center_norm(va) * r.lnw[:, gc] + r.lnb[:, gc]
        r.va[:, g, :] = va
        r.douta[:, gc] = ua * (r.ws[g][0:1, 0:1] * va + r.bsr[g:g + 1, 0:1])
    base = 2 * A_WIDTH
    for hh in range(B_HEADS):
        hc = slice(hh * B_DK, (hh + 1) * B_DK)
        r.dq[:, hc] = _rope(z[:, base + hh * B_DK:base + (hh + 1) * B_DK], cosf, sinf)
        r.dk[:, hc] = _rope(z[:, base + B_WIDTH + hh * B_DK:base + B_WIDTH + (hh + 1) * B_DK],
                            cosf, sinf) * (B_DK ** -0.5)
    r.dv[...] = z[:, base + 2 * B_WIDTH:base + 3 * B_WIDTH]
    r.dgate[...] = _silu(z[:, base + 3 * B_WIDTH:])
    r.do[...] = jnp.zeros_like(r.do)


def _decode_state_step(r):
    i = pl.program_id(0)
    per_group = DEC_GROUP // DEC_ROWS
    rows = pl.ds(pl.multiple_of((i // per_group) * DEC_GROUP, DEC_GROUP), DEC_GROUP)
    first = lax.rem(i, per_group) * DEC_ROWS
    rid = lax.broadcasted_iota(jnp.int32, (DEC_GROUP, B_DK), 0)
    live = jnp.logical_and(rid >= first, rid < first + DEC_ROWS)
    for hh in range(B_HEADS):
        hc = slice(hh * B_DK, (hh + 1) * B_DK)
        q = r.dq[rows, hc]
        k = r.dk[rows, hc]
        v = r.dv[rows, hc]
        gam = math.exp(LOG_GAMMA[hh])
        qk = jnp.sum(q * k, axis=-1, keepdims=True)
        qg = (q * gam).astype(BF16)
        v_sel = jnp.concatenate([jnp.where(rid == first + j, v, 0.0) for j in range(DEC_ROWS)],
                                axis=-1).astype(BF16)
        kv = _dot_tn(k.astype(BF16), v_sel)
        one_sel = jnp.concatenate([jnp.where(rid == first + j, 1.0, 0.0) for j in range(DEC_ROWS)],
                                  axis=-1).astype(BF16)
        q_cols = _dot_tn(qg, one_sel)
        cross = jnp.zeros((DEC_GROUP, B_DV), F32)
        for j in range(DEC_ROWS):
            s = r.s0[j, hh]
            cross_j = jnp.sum(q_cols[:, j * B_DV:(j + 1) * B_DV] * s, axis=0, keepdims=True)
            cross = jnp.where(rid == first + j, cross_j, cross)
            r.s_dec[j, hh] = s * gam + kv[:, j * B_DV:(j + 1) * B_DV]
        r.do[rows, hc] = jnp.where(live, qk * v + cross, r.do[rows, hc])


def _last_step(r):
    parts = [r.douta[...].astype(BF16)]
    for hh in range(B_HEADS):
        hc = slice(hh * B_DV, (hh + 1) * B_DV)
        o = _center_norm(r.do[:, hc]) * r.gn[:, hc]
        parts.append((o * r.dgate[:, hc]).astype(BF16))
    mix = _dot(jnp.concatenate(parts, axis=-1), r.w_out[...])
    x1_dec = r.xs[...] + _rms(mix, r.post_mix[...])
    hn = jnp.concatenate([r.hn[...], _rms(x1_dec, r.pre_mlp[...]).astype(BF16)], axis=0)

    def block(j, f):
        u = jnp.maximum(_dot(hn, r.w_up[j]), 0.0)
        rows = pl.ds(pl.multiple_of(j * FF_TILE, FF_TILE), FF_TILE)
        return f + _dot((u * u).astype(BF16), r.w_down[rows, :])

    f = lax.fori_loop(0, D_FF // FF_TILE, block, jnp.zeros((hn.shape[0], D_MODEL), F32))
    r.y[...] = r.x1[...] + _rms(f[:ROW_TILE], r.post_mlp[...])
    r.ys[...] = x1_dec + _rms(f[ROW_TILE:], r.post_mlp[...])


def _layer_kernel(*refs, tiles_per_seq, n_tiles):
    r = _Refs(*refs)
    i = pl.program_id(0)

    @pl.when(i == 0)
    def _():
        _convert_weight(r.w_in_hbm, r.w_in, meanwhile=lambda: _fill_rope_tables(r, 0, 2))
        _convert_weight(r.w_out_hbm, r.w_out, meanwhile=lambda: _fill_decay_tables(r))
        _convert_weight(r.w_up_hbm, r.w_up, col_pieces=D_FF // FF_TILE,
                        meanwhile=lambda: _decode_projections(r))
        _convert_weight(r.w_down_hbm, r.w_down, meanwhile=lambda: _fill_rope_tables(r, 1, 2))

    @pl.when(jnp.logical_and(lax.rem(i, tiles_per_seq) == 0, i < n_tiles))
    def _():
        r.s[...] = jnp.zeros_like(r.s)

    @pl.when(i == 0)
    def _():
        _prompt_step(r, tiles_per_seq=tiles_per_seq, with_mlp=False)

    @pl.when(jnp.logical_and(i > 0, i < n_tiles))
    def _():
        _prompt_step(r, tiles_per_seq=tiles_per_seq, with_mlp=True)

    @pl.when(i == n_tiles)
    def _():
        _last_step(r)


def _prompt_step(r, *, tiles_per_seq, with_mlp):
    n_chunks = ROW_TILE // CHUNK
    chunk_rows = [slice(c * CHUNK, (c + 1) * CHUNK) for c in range(n_chunks)]
    acc = []
    hidden = []

    def mlp_piece(k):
        if not with_mlp:
            return
        j = k // 2
        if k % 2 == 0:
            u = jnp.maximum(_dot(r.hn[...], r.w_up[j]), 0.0)
            hidden.append((u * u).astype(BF16))
        else:
            part = _dot(hidden.pop(), r.w_down[j * FF_TILE:(j + 1) * FF_TILE, :])
            acc[:] = [part if not acc else acc[0] + part]

    def mlp_finish():
        if with_mlp:
            r.y[...] = r.x1[...] + _rms(acc[0], r.post_mlp[...])

    x = r.x[...]
    mlp_piece(0)
    h = _rms(x, r.pre_mix[...]).astype(BF16)
    split = 2 * A_WIDTH + 2 * B_WIDTH
    z1 = _dot(h, r.w_in[:, :split])
    _decode_state_step(r)
    z2 = _dot(h, r.w_in[:, split:])
    tile_pos = pl.multiple_of(lax.rem(pl.program_id(0), tiles_per_seq) * ROW_TILE, ROW_TILE)
    cosf = r.cos[pl.ds(tile_pos, ROW_TILE), :]
    sinf = r.sin[pl.ds(tile_pos, ROW_TILE), :]

    def head_inputs(hh):
        base = 2 * A_WIDTH + hh * B_DK
        q = _rope(z1[:, base:base + B_DK], cosf, sinf)
        k = _rope(z1[:, base + B_WIDTH:base + B_WIDTH + B_DK], cosf, sinf) * (B_DK ** -0.5)
        v = z2[:, hh * B_DV:(hh + 1) * B_DV].astype(BF16)
        return {
            "q": q.astype(BF16),
            "k": k.astype(BF16),
            "v": v,
            "qd": [(q[c] * r.qdec[hh]).astype(BF16) for c in chunk_rows],
            "kd": [(k[c] * r.kdec[hh]).astype(BF16) for c in chunk_rows],
        }

    def head_dots_a(hd):
        hd["scores"] = [_dot_nt(hd["q"][c], hd["k"][c]) for c in chunk_rows]
        hd["kv"] = [_dot_tn(hd["kd"][ci], hd["v"][c]) for ci, c in enumerate(chunk_rows)]

    def head_chain(hh, hd):
        s = r.s[hh]
        hd["s"] = []
        for ci in range(n_chunks):
            hd["s"].append(s.astype(BF16))
            s = s * math.exp(LOG_GAMMA[hh] * CHUNK) + hd["kv"][ci]
        r.s[hh] = s
        hd["p"] = [(hd["scores"][ci] * r.mask[hh]).astype(BF16) for ci in range(n_chunks)]

    def head_dots_b(hd):
        hd["o"] = [_dot(hd["p"][ci], hd["v"][c]) + _dot(hd["qd"][ci], hd["s"][ci])
                   for ci, c in enumerate(chunk_rows)]

    def head_finish(hh, hd):
        base = B_WIDTH + hh * B_DV
        gate = _silu(z2[:, base:base + B_DV])
        gnw = r.gn[:, hh * B_DV:(hh + 1) * B_DV]
        for ci, c in enumerate(chunk_rows):
            o = _center_norm(hd["o"][ci]) * gnw
            r.mix[c, A_WIDTH + hh * B_DV:A_WIDTH + (hh + 1) * B_DV] = (o * gate[c]).astype(BF16)

    mlp_piece(1)
    ua, va = [], []
    for g in range(A_GROUPS):
        gc = slice(g * A_DIM, (g + 1) * A_DIM)
        ua.append(_gelu(z1[:, g * A_DIM:(g + 1) * A_DIM]))
        vg = _gelu(z1[:, A_WIDTH + g * A_DIM:A_WIDTH + (g + 1) * A_DIM])
        va.append((_center_norm(vg) * r.lnw[:, gc] + r.lnb[:, gc]).astype(BF16))
        if g == 1:
            mlp_piece(2)
    heads = [head_inputs(0)]

    row = lax.broadcasted_iota(jnp.int32, (CHUNK, CHUNK), 0)
    col = lax.broadcasted_iota(jnp.int32, (CHUNK, CHUNK), 1)
    mixed = []
    for g in range(A_GROUPS):
        w_tril = jnp.where(row >= col, r.ws[g], 0.0).astype(BF16)
        mixed.append([_dot(w_tril, va[g][c]) for c in chunk_rows])
    head_dots_a(heads[0])

    for hh in range(B_HEADS):
        mlp_piece(3 + hh)
        if hh == 0:
            for g in range(A_GROUPS):
                gc = slice(g * A_DIM, (g + 1) * A_DIM)
                bias = r.bias[g]
                for ci, c in enumerate(chunk_rows):
                    r.mix[c, gc] = (ua[g][c] * (mixed[g][ci] + bias)).astype(BF16)
        else:
            head_finish(hh - 1, heads[hh - 1])
        head_chain(hh, heads[hh])
        if hh + 1 < B_HEADS:
            heads.append(head_inputs(hh + 1))
        head_dots_b(heads[hh])
        if hh + 1 < B_HEADS:
            head_dots_a(heads[hh + 1])

    mlp_piece(7)
    head_finish(B_HEADS - 1, heads[B_HEADS - 1])
    half = ROW_TILE // 2
    mix_a = _dot(r.mix[:half, :], r.w_out[...])
    mlp_finish()
    mix_b = _dot(r.mix[half:, :], r.w_out[...])
    for rows, mix in ((slice(0, half), mix_a), (slice(half, ROW_TILE), mix_b)):
        x1 = x[rows] + _rms(mix, r.post_mix[...])
        r.x1[rows, :] = x1
        r.hn[rows, :] = _rms(x1, r.pre_mlp[...]).astype(BF16)


def _resident(shape):
    zeros = (0,) * len(shape)
    return pl.BlockSpec(shape, lambda i: zeros, pipeline_mode=pl.Buffered(1))


def _layer(x, xs, s0, p):
    batch, seq, _ = x.shape
    n_dec = xs.shape[0]
    tiles_per_seq = seq // ROW_TILE
    n_tiles = batch * tiles_per_seq
    assert n_dec == n_tiles * DEC_ROWS and DEC_GROUP % DEC_ROWS == 0

    def mixer_tile(i):
        return jnp.minimum(i, n_tiles - 1)

    def mlp_tile(i):
        return jnp.maximum(i - 1, 0)

    x_spec = pl.BlockSpec((None, ROW_TILE, D_MODEL),
                          lambda i: (mixer_tile(i) // tiles_per_seq, mixer_tile(i) % tiles_per_seq, 0))
    y_spec = pl.BlockSpec((None, ROW_TILE, D_MODEL),
                          lambda i: (mlp_tile(i) // tiles_per_seq, mlp_tile(i) % tiles_per_seq, 0))
    tok_spec = pl.BlockSpec((n_dec, None, D_MODEL), lambda i: (0, 0, 0), pipeline_mode=pl.Buffered(1))
    ys_spec = pl.BlockSpec((n_dec, None, D_MODEL), lambda i: (0, 0, 0))
    va_spec = pl.BlockSpec((n_dec, None, A_GROUPS, A_DIM), lambda i: (0, 0, 0, 0))
    dec_state_spec = pl.BlockSpec((DEC_ROWS, B_HEADS, B_DK, B_DV), lambda i: (mixer_tile(i), 0, 0, 0))
    hbm = pl.BlockSpec(memory_space=pl.ANY)
    in_specs = [
        x_spec, tok_spec, dec_state_spec,
        _resident((1, D_MODEL)),
        _resident((1, A_WIDTH)), _resident((1, A_WIDTH)),
        _resident((A_GROUPS, CHUNK, CHUNK)), _resident((A_GROUPS, CHUNK)),
        _resident((1, B_WIDTH)),
        _resident((1, D_MODEL)), _resident((1, D_MODEL)), _resident((1, D_MODEL)),
        hbm, hbm, hbm, hbm,
    ]
    out_specs = [
        y_spec,
        pl.BlockSpec((None, B_HEADS, B_DK, B_DV), lambda i: (mixer_tile(i) // tiles_per_seq, 0, 0, 0)),
        ys_spec, dec_state_spec, va_spec,
    ]
    out_shape = [
        jax.ShapeDtypeStruct(x.shape, F32),
        jax.ShapeDtypeStruct((batch, B_HEADS, B_DK, B_DV), F32),
        jax.ShapeDtypeStruct((n_dec, 1, D_MODEL), F32),
        jax.ShapeDtypeStruct(s0.shape, F32),
        jax.ShapeDtypeStruct((n_dec, 1, A_GROUPS, A_DIM), F32),
    ]
    scratch_shapes = [
        pltpu.VMEM((D_MODEL, IN_COLS), BF16),
        pltpu.VMEM((D_MODEL, D_MODEL), BF16),
        pltpu.VMEM((D_FF // FF_TILE, D_MODEL, FF_TILE), BF16),
        pltpu.VMEM((D_FF, D_MODEL), BF16),
        pltpu.VMEM((ROW_TILE, D_MODEL), BF16),
        pltpu.VMEM((ROW_TILE, D_MODEL), F32),
        pltpu.VMEM((ROW_TILE, D_MODEL), BF16),
        pltpu.VMEM((seq, B_DK), F32),
        pltpu.VMEM((seq, B_DK), F32),
        pltpu.VMEM((B_HEADS, CHUNK, CHUNK), F32),
        pltpu.VMEM((B_HEADS, CHUNK, B_DK), F32),
        pltpu.VMEM((B_HEADS, CHUNK, B_DK), F32),
        pltpu.VMEM((n_dec, B_WIDTH), F32),
        pltpu.VMEM((n_dec, B_WIDTH), F32),
        pltpu.VMEM((n_dec, B_WIDTH), F32),
        pltpu.VMEM((n_dec, B_WIDTH), F32),
        pltpu.VMEM((n_dec, A_WIDTH), F32),
        pltpu.VMEM((n_dec, B_WIDTH), F32),
        pltpu.VMEM((A_GROUPS, CHUNK, A_DIM), F32),
    ]
    return pl.pallas_call(
        functools.partial(_layer_kernel, tiles_per_seq=tiles_per_seq, n_tiles=n_tiles),
        grid=(n_tiles + 1,),
        in_specs=in_specs,
        out_specs=out_specs,
        out_shape=out_shape,
        scratch_shapes=scratch_shapes,
        compiler_params=pltpu.CompilerParams(
            dimension_semantics=("arbitrary",),
            vmem_limit_bytes=VMEM_LIMIT_BYTES),
        name="hybrid_layer",
    )(x, xs, s0, p["pre_mix_w"], p["ln_v_w"], p["ln_v_b"], p["w_s"], p["b_s"],
      p["gn_w"], p["post_mix_w"], p["pre_mlp_w"], p["post_mlp_w"],
      p["w_in"], p["w_out"], p["w_up"], p["w_down"])


def kernel(x_prompt, x_sample, state_ret, pre_mix_w, w_in, ln_v_w, ln_v_b, w_s, b_s, gn_w,
           w_out, post_mix_w, pre_mlp_w, w_up, w_down, post_mlp_w):
    depth = w_in.shape[0]
    seq = x_prompt.shape[1]
    assert x_sample.shape[1] == 1 and seq % ROW_TILE == 0

    yp, ys = x_prompt, x_sample
    sp_list, ss_list, vs_list = [], [], []
    for l in range(depth):
        p = {
            "pre_mix_w": pre_mix_w[l][None, :],
            "w_in": w_in[l],
            "ln_v_w": ln_v_w[l].reshape(1, A_WIDTH),
            "ln_v_b": ln_v_b[l].reshape(1, A_WIDTH),
            "w_s": w_s[l],
            "b_s": b_s[l],
            "gn_w": gn_w[l].reshape(1, B_WIDTH),
            "w_out": w_out[l],
            "post_mix_w": post_mix_w[l][None, :],
            "pre_mlp_w": pre_mlp_w[l][None, :],
            "w_up": w_up[l],
            "w_down": w_down[l],
            "post_mlp_w": post_mlp_w[l][None, :],
        }
        yp, sp, ys, ss, vs = _layer(yp, ys, state_ret[l], p)
        sp_list.append(sp)
        ss_list.append(ss)
        vs_list.append(vs)
    return (yp, ys, jnp.stack(sp_list, axis=0), jnp.stack(ss_list, axis=0), jnp.stack(vs_list, axis=0))
```

```python
import functools
import math
from typing import Any, NamedTuple

import jax
import jax.numpy as jnp
from jax import lax
from jax.experimental import pallas as pl
from jax.experimental.pallas import tpu as pltpu

D_MODEL = 1024
A_WIDTH = 512
B_WIDTH = 512
A_GROUPS = 4
A_DIM = 128
CHUNK = 128
B_HEADS = 4
B_DK = 128
B_DV = 128
D_FF = 4096
IN_COLS = 2 * A_WIDTH + 4 * B_WIDTH
ROPE_BASE = 10000.0
EPS = 1e-6
PAST_LEN = 16384
LOG_GAMMA = [math.log(1.0 - 2.0 ** (-5.0 - h)) for h in range(B_HEADS)]

ROW_TILE = 512
FF_TILE = 1024
MLP_PIECES = 2 * (D_FF // FF_TILE)
DEC_ROWS = 4
DEC_GROUP = 16
STAGE_BYTES = 1 << 20
STAGE_SLOTS = 10
VMEM_LIMIT_BYTES = 60 * 1024 * 1024

F32 = jnp.float32
BF16 = jnp.bfloat16


def _rms(x, w):
    return x * lax.rsqrt(jnp.mean(x * x, axis=-1, keepdims=True) + EPS) * w


def _center_norm(x):
    mu = jnp.mean(x, axis=-1, keepdims=True)
    xc = x - mu
    return xc * lax.rsqrt(jnp.mean(xc * xc, axis=-1, keepdims=True) + EPS)


def _gelu(x):
    return jax.nn.gelu(x, approximate=True)


def _silu(x):
    return x * (1.0 / (1.0 + jnp.exp(-x)))


def _rope(x, cosf, sinf):
    return x * cosf + pltpu.roll(x, B_DK // 2, axis=1) * sinf


def _rope_rows(pos):
    lane = lax.broadcasted_iota(jnp.int32, pos.shape, 1)
    j = lax.rem(lane, B_DK // 2).astype(F32)
    ang = pos * jnp.exp(j * (-math.log(ROPE_BASE) / (B_DK // 2)))
    sin = jnp.sin(ang)
    return jnp.cos(ang), jnp.where(lane < B_DK // 2, -sin, sin)


def _dot(a, b):
    return jnp.dot(a, b, preferred_element_type=F32)


def _dot_nt(a, b):
    return lax.dot_general(a, b, (((1,), (1,)), ((), ())), preferred_element_type=F32)


def _dot_tn(a, b):
    return lax.dot_general(a, b, (((0,), (0,)), ((), ())), preferred_element_type=F32)


class _Refs(NamedTuple):
    x: Any
    xs: Any
    s0: Any
    pre_mix: Any
    lnw: Any
    lnb: Any
    ws: Any
    bsr: Any
    gn: Any
    post_mix: Any
    pre_mlp: Any
    post_mlp: Any
    w_in_hbm: Any
    w_out_hbm: Any
    w_up_hbm: Any
    w_down_hbm: Any
    y: Any
    s: Any
    ys: Any
    s_dec: Any
    va: Any
    w_in: Any
    w_out: Any
    w_up: Any
    w_down: Any
    mix: Any
    x1: Any
    hn: Any
    cos: Any
    sin: Any
    mask: Any
    qdec: Any
    kdec: Any
    dq: Any
    dk: Any
    dv: Any
    dgate: Any
    douta: Any
    do: Any
    bias: Any


def _convert_weight(w_hbm, w_bf, col_pieces=1, meanwhile=None):
    n_rows, n_cols = w_hbm.shape
    piece = n_cols // col_pieces
    rows_per = 1 << int(math.log2(STAGE_BYTES // (4 * n_cols)))
    assert n_rows % rows_per == 0 and rows_per % 16 == 0
    n_chunks = n_rows // rows_per

    ahead = STAGE_SLOTS - 1

    def body(stage, sem):
        def copy(c):
            slot = lax.rem(c, STAGE_SLOTS)
            return pltpu.make_async_copy(
                w_hbm.at[pl.ds(c * rows_per, rows_per), :], stage.at[slot], sem.at[slot])

        for c in range(min(ahead, n_chunks)):
            copy(c).start()
        if meanwhile is not None:
            meanwhile()

        def step(c, carry):
            slot = lax.rem(c, STAGE_SLOTS)

            @pl.when(c + ahead < n_chunks)
            def _():
                copy(c + ahead).start()

            copy(c).wait()
            rows = pl.ds(pl.multiple_of(c * rows_per, rows_per), rows_per)
            if col_pieces == 1:
                w_bf[rows, :] = stage[slot].astype(BF16)
            else:
                for p in range(col_pieces):
                    w_bf[p, rows, :] = stage[slot, :, p * piece:(p + 1) * piece].astype(BF16)
            return carry

        lax.fori_loop(0, n_chunks, step, 0)

    pl.run_scoped(body, pltpu.VMEM((STAGE_SLOTS, rows_per, n_cols), F32),
                  pltpu.SemaphoreType.DMA((STAGE_SLOTS,)))


def _fill_rope_tables(r, part, n_parts):
    blocks = r.cos.shape[0] // ROW_TILE // n_parts

    def rope_block(b, carry):
        r0 = pl.multiple_of(b * ROW_TILE, ROW_TILE)
        pos = (lax.broadcasted_iota(jnp.int32, (ROW_TILE, B_DK), 0) + r0).astype(F32)
        cosf, sinf = _rope_rows(pos)
        r.cos[pl.ds(r0, ROW_TILE), :] = cosf
        r.sin[pl.ds(r0, ROW_TILE), :] = sinf
        return carry

    lax.fori_loop(part * blocks, (part + 1) * blocks, rope_block, 0)


def _fill_decay_tables(r):
    row = lax.broadcasted_iota(jnp.int32, (CHUNK, CHUNK), 0).astype(F32)
    col = lax.broadcasted_iota(jnp.int32, (CHUNK, CHUNK), 1).astype(F32)
    diff = row - col
    for g in range(A_GROUPS):
        on_diag = jnp.where(diff == 0, r.bsr[g:g + 1, :], 0.0)
        r.bias[g] = jnp.broadcast_to(jnp.sum(on_diag, axis=1, keepdims=True), (CHUNK, A_DIM))
    for hh in range(B_HEADS):
        lg = LOG_GAMMA[hh]
        r.mask[hh] = jnp.where(diff >= 0, jnp.exp(lg * jnp.maximum(diff, 0.0)), 0.0)
        r.qdec[hh] = jnp.exp(lg * (row + 1.0))
        r.kdec[hh] = jnp.exp(lg * (CHUNK - 1.0 - row))


def _decode_projections(r):
    h = _rms(r.xs[...], r.pre_mix[...]).astype(BF16)
    z = _dot(h, r.w_in[...])
    cosf, sinf = _rope_rows(jnp.full((1, B_DK), PAST_LEN, F32))
    for g in range(A_GROUPS):
        gc = slice(g * A_DIM, (g + 1) * A_DIM)
        ua = _gelu(z[:, g * A_DIM:(g + 1) * A_DIM])
        va = _gelu(z[:, A_WIDTH + g * A_DIM:A_WIDTH + (g + 1) * A_DIM])
        va = _center_norm(va) * r.lnw[:, gc] + r.lnb[:, gc]
        r.va[:, g, :] = va
        r.douta[:, gc] = ua * (r.ws[g][0:1, 0:1] * va + r.bsr[g:g + 1, 0:1])
    base = 2 * A_WIDTH
    for hh in range(B_HEADS):
        hc = slice(hh * B_DK, (hh + 1) * B_DK)
        r.dq[:, hc] = _rope(z[:, base + hh * B_DK:base + (hh + 1) * B_DK], cosf, sinf)
        r.dk[:, hc] = _rope(z[:, base + B_WIDTH + hh * B_DK:base + B_WIDTH + (hh + 1) * B_DK],
                            cosf, sinf) * (B_DK ** -0.5)
    r.dv[...] = z[:, base + 2 * B_WIDTH:base + 3 * B_WIDTH]
    r.dgate[...] = _silu(z[:, base + 3 * B_WIDTH:])
    r.do[...] = jnp.zeros_like(r.do)


def _decode_state_step(r):
    i = pl.program_id(0)
    per_group = DEC_GROUP // DEC_ROWS
    rows = pl.ds(pl.multiple_of((i // per_group) * DEC_GROUP, DEC_GROUP), DEC_GROUP)
    first = lax.rem(i, per_group) * DEC_ROWS
    rid = lax.broadcasted_iota(jnp.int32, (DEC_GROUP, B_DK), 0)
    live = jnp.logical_and(rid >= first, rid < first + DEC_ROWS)
    for hh in range(B_HEADS):
        hc = slice(hh * B_DK, (hh + 1) * B_DK)
        q = r.dq[rows, hc]
        k = r.dk[rows, hc]
        v = r.dv[rows, hc]
        gam = math.exp(LOG_GAMMA[hh])
        qk = jnp.sum(q * k, axis=-1, keepdims=True)
        qg = (q * gam).astype(BF16)
        v_sel = jnp.concatenate([jnp.where(rid == first + j, v, 0.0) for j in range(DEC_ROWS)],
                                axis=-1).astype(BF16)
        kv = _dot_tn(k.astype(BF16), v_sel)
        one_sel = jnp.concatenate([jnp.where(rid == first + j, 1.0, 0.0) for j in range(DEC_ROWS)],
                                  axis=-1).astype(BF16)
        q_cols = _dot_tn(qg, one_sel)
        cross = jnp.zeros((DEC_GROUP, B_DV), F32)
        for j in range(DEC_ROWS):
            s = r.s0[j, hh]
            cross_j = jnp.sum(q_cols[:, j * B_DV:(j + 1) * B_DV] * s, axis=0, keepdims=True)
            cross = jnp.where(rid == first + j, cross_j, cross)
            r.s_dec[j, hh] = s * gam + kv[:, j * B_DV:(j + 1) * B_DV]
        r.do[rows, hc] = jnp.where(live, qk * v + cross, r.do[rows, hc])


def _last_step(r):
    parts = [r.douta[...].astype(BF16)]
    for hh in range(B_HEADS):
        hc = slice(hh * B_DV, (hh + 1) * B_DV)
        o = _center_norm(r.do[:, hc]) * r.gn[:, hc]
        parts.append((o * r.dgate[:, hc]).astype(BF16))
    mix = _dot(jnp.concatenate(parts, axis=-1), r.w_out[...])
    x1_dec = r.xs[...] + _rms(mix, r.post_mix[...])
    hn = jnp.concatenate([r.hn[...], _rms(x1_dec, r.pre_mlp[...]).astype(BF16)], axis=0)

    def block(j, f):
        u = jnp.maximum(_dot(hn, r.w_up[j]), 0.0)
        rows = pl.ds(pl.multiple_of(j * FF_TILE, FF_TILE), FF_TILE)
        return f + _dot((u * u).astype(BF16), r.w_down[rows, :])

    f = lax.fori_loop(0, D_FF // FF_TILE, block, jnp.zeros((hn.shape[0], D_MODEL), F32))
    r.y[...] = r.x1[...] + _rms(f[:ROW_TILE], r.post_mlp[...])
    r.ys[...] = x1_dec + _rms(f[ROW_TILE:], r.post_mlp[...])


def _layer_kernel(*refs, tiles_per_seq, n_tiles):
    r = _Refs(*refs)
    i = pl.program_id(0)

    @pl.when(i == 0)
    def _():
        _convert_weight(r.w_in_hbm, r.w_in, meanwhile=lambda: _fill_rope_tables(r, 0, 2))
        _convert_weight(r.w_out_hbm, r.w_out, meanwhile=lambda: _fill_decay_tables(r))
        _convert_weight(r.w_up_hbm, r.w_up, col_pieces=D_FF // FF_TILE,
                        meanwhile=lambda: _decode_projections(r))
        _convert_weight(r.w_down_hbm, r.w_down, meanwhile=lambda: _fill_rope_tables(r, 1, 2))

    @pl.when(jnp.logical_and(lax.rem(i, tiles_per_seq) == 0, i < n_tiles))
    def _():
        r.s[...] = jnp.zeros_like(r.s)

    @pl.when(i == 0)
    def _():
        _prompt_step(r, tiles_per_seq=tiles_per_seq, with_mlp=False)

    @pl.when(jnp.logical_and(i > 0, i < n_tiles))
    def _():
        _prompt_step(r, tiles_per_seq=tiles_per_seq, with_mlp=True)

    @pl.when(i == n_tiles)
    def _():
        _last_step(r)


def _prompt_step(r, *, tiles_per_seq, with_mlp):
    n_chunks = ROW_TILE // CHUNK
    chunk_rows = [slice(c * CHUNK, (c + 1) * CHUNK) for c in range(n_chunks)]
    acc = []
    hidden = []

    def mlp_piece(k):
        if not with_mlp:
            return
        j = k // 2
        if k % 2 == 0:
            u = jnp.maximum(_dot(r.hn[...], r.w_up[j]), 0.0)
            hidden.append((u * u).astype(BF16))
        else:
            part = _dot(hidden.pop(), r.w_down[j * FF_TILE:(j + 1) * FF_TILE, :])
            acc[:] = [part if not acc else acc[0] + part]

    def mlp_finish():
        if with_mlp:
            r.y[...] = r.x1[...] + _rms(acc[0], r.post_mlp[...])

    x = r.x[...]
    mlp_piece(0)
    h = _rms(x, r.pre_mix[...]).astype(BF16)
    split = 2 * A_WIDTH + 2 * B_WIDTH
    z1 = _dot(h, r.w_in[:, :split])
    _decode_state_step(r)
    z2 = _dot(h, r.w_in[:, split:])
    tile_pos = pl.multiple_of(lax.rem(pl.program_id(0), tiles_per_seq) * ROW_TILE, ROW_TILE)
    cosf = r.cos[pl.ds(tile_pos, ROW_TILE), :]
    sinf = r.sin[pl.ds(tile_pos, ROW_TILE), :]

    def head_inputs(hh):
        base = 2 * A_WIDTH + hh * B_DK
        q = _rope(z1[:, base:base + B_DK], cosf, sinf)
        k = _rope(z1[:, base + B_WIDTH:base + B_WIDTH + B_DK], cosf, sinf) * (B_DK ** -0.5)
        v = z2[:, hh * B_DV:(hh + 1) * B_DV].astype(BF16)
        return {
            "q": q.astype(BF16),
            "k": k.astype(BF16),
            "v": v,
            "qd": [(q[c] * r.qdec[hh]).astype(BF16) for c in chunk_rows],
            "kd": [(k[c] * r.kdec[hh]).astype(BF16) for c in chunk_rows],
        }

    def head_dots_a(hd):
        hd["scores"] = [_dot_nt(hd["q"][c], hd["k"][c]) for c in chunk_rows]
        hd["kv"] = [_dot_tn(hd["kd"][ci], hd["v"][c]) for ci, c in enumerate(chunk_rows)]

    def head_chain(hh, hd):
        s = r.s[hh]
        hd["s"] = []
        for ci in range(n_chunks):
            hd["s"].append(s.astype(BF16))
            s = s * math.exp(LOG_GAMMA[hh] * CHUNK) + hd["kv"][ci]
        r.s[hh] = s
        hd["p"] = [(hd["scores"][ci] * r.mask[hh]).astype(BF16) for ci in range(n_chunks)]

    def head_dots_b(hd):
        hd["o"] = [_dot(hd["p"][ci], hd["v"][c]) + _dot(hd["qd"][ci], hd["s"][ci])
                   for ci, c in enumerate(chunk_rows)]

    def head_finish(hh, hd):
        base = B_WIDTH + hh * B_DV
        gate = _silu(z2[:, base:base + B_DV])
        gnw = r.gn[:, hh * B_DV:(hh + 1) * B_DV]
        for ci, c in enumerate(chunk_rows):
            o = _center_norm(hd["o"][ci]) * gnw
            r.mix[c, A_WIDTH + hh * B_DV:A_WIDTH + (hh + 1) * B_DV] = (o * gate[c]).astype(BF16)

    mlp_piece(1)
    ua, va = [], []
    for g in range(A_GROUPS):
        gc = slice(g * A_DIM, (g + 1) * A_DIM)
        ua.append(_gelu(z1[:, g * A_DIM:(g + 1) * A_DIM]))
        vg = _gelu(z1[:, A_WIDTH + g * A_DIM:A_WIDTH + (g + 1) * A_DIM])
        va.append((_center_norm(vg) * r.lnw[:, gc] + r.lnb[:, gc]).astype(BF16))
        if g == 1:
            mlp_piece(2)
    heads = [head_inputs(0)]

    row = lax.broadcasted_iota(jnp.int32, (CHUNK, CHUNK), 0)
    col = lax.broadcasted_iota(jnp.int32, (CHUNK, CHUNK), 1)
    mixed = []
    for g in range(A_GROUPS):
        w_tril = jnp.where(row >= col, r.ws[g], 0.0).astype(BF16)
        mixed.append([_dot(w_tril, va[g][c]) for c in chunk_rows])
    head_dots_a(heads[0])

    for hh in range(B_HEADS):
        mlp_piece(3 + hh)
        if hh == 0:
            for g in range(A_GROUPS):
                gc = slice(g * A_DIM, (g + 1) * A_DIM)
                bias = r.bias[g]
                for ci, c in enumerate(chunk_rows):
                    r.mix[c, gc] = (ua[g][c] * (mixed[g][ci] + bias)).astype(BF16)
        else:
            head_finish(hh - 1, heads[hh - 1])
        head_chain(hh, heads[hh])
        if hh + 1 < B_HEADS:
            heads.append(head_inputs(hh + 1))
        if hh + 1 < B_HEADS:
            head_dots_a(heads[hh + 1])
        head_dots_b(heads[hh])

    mlp_piece(7)
    head_finish(B_HEADS - 1, heads[B_HEADS - 1])
    half = ROW_TILE // 2
    mix_a = _dot(r.mix[:half, :], r.w_out[...])
    mlp_finish()
    mix_b = _dot(r.mix[half:, :], r.w_out[...])
    for rows, mix in ((slice(0, half), mix_a), (slice(half, ROW_TILE), mix_b)):
        x1 = x[rows] + _rms(mix, r.post_mix[...])
        r.x1[rows, :] = x1
        r.hn[rows, :] = _rms(x1, r.pre_mlp[...]).astype(BF16)


def _resident(shape):
    zeros = (0,) * len(shape)
    return pl.BlockSpec(shape, lambda i: zeros, pipeline_mode=pl.Buffered(1))


def _layer(x, xs, s0, p):
    batch, seq, _ = x.shape
    n_dec = xs.shape[0]
    tiles_per_seq = seq // ROW_TILE
    n_tiles = batch * tiles_per_seq
    assert n_dec == n_tiles * DEC_ROWS and DEC_GROUP % DEC_ROWS == 0

    def mixer_tile(i):
        return jnp.minimum(i, n_tiles - 1)

    def mlp_tile(i):
        return jnp.maximum(i - 1, 0)

    x_spec = pl.BlockSpec((None, ROW_TILE, D_MODEL),
                          lambda i: (mixer_tile(i) // tiles_per_seq, mixer_tile(i) % tiles_per_seq, 0))
    y_spec = pl.BlockSpec((None, ROW_TILE, D_MODEL),
                          lambda i: (mlp_tile(i) // tiles_per_seq, mlp_tile(i) % tiles_per_seq, 0))
    tok_spec = pl.BlockSpec((n_dec, None, D_MODEL), lambda i: (0, 0, 0), pipeline_mode=pl.Buffered(1))
    ys_spec = pl.BlockSpec((n_dec, None, D_MODEL), lambda i: (0, 0, 0))
    va_spec = pl.BlockSpec((n_dec, None, A_GROUPS, A_DIM), lambda i: (0, 0, 0, 0))
    dec_state_spec = pl.BlockSpec((DEC_ROWS, B_HEADS, B_DK, B_DV), lambda i: (mixer_tile(i), 0, 0, 0))
    hbm = pl.BlockSpec(memory_space=pl.ANY)
    in_specs = [
        x_spec, tok_spec, dec_state_spec,
        _resident((1, D_MODEL)),
        _resident((1, A_WIDTH)), _resident((1, A_WIDTH)),
        _resident((A_GROUPS, CHUNK, CHUNK)), _resident((A_GROUPS, CHUNK)),
        _resident((1, B_WIDTH)),
        _resident((1, D_MODEL)), _resident((1, D_MODEL)), _resident((1, D_MODEL)),
        hbm, hbm, hbm, hbm,
    ]
    out_specs = [
        y_spec,
        pl.BlockSpec((None, B_HEADS, B_DK, B_DV), lambda i: (mixer_tile(i) // tiles_per_seq, 0, 0, 0)),
        ys_spec, dec_state_spec, va_spec,
    ]
    out_shape = [
        jax.ShapeDtypeStruct(x.shape, F32),
        jax.ShapeDtypeStruct((batch, B_HEADS, B_DK, B_DV), F32),
        jax.ShapeDtypeStruct((n_dec, 1, D_MODEL), F32),
        jax.ShapeDtypeStruct(s0.shape, F32),
        jax.ShapeDtypeStruct((n_dec, 1, A_GROUPS, A_DIM), F32),
    ]
    scratch_shapes = [
        pltpu.VMEM((D_MODEL, IN_COLS), BF16),
        pltpu.VMEM((D_MODEL, D_MODEL), BF16),
        pltpu.VMEM((D_FF // FF_TILE, D_MODEL, FF_TILE), BF16),
        pltpu.VMEM((D_FF, D_MODEL), BF16),
        pltpu.VMEM((ROW_TILE, D_MODEL), BF16),
        pltpu.VMEM((ROW_TILE, D_MODEL), F32),
        pltpu.VMEM((ROW_TILE, D_MODEL), BF16),
        pltpu.VMEM((seq, B_DK), F32),
        pltpu.VMEM((seq, B_DK), F32),
        pltpu.VMEM((B_HEADS, CHUNK, CHUNK), F32),
        pltpu.VMEM((B_HEADS, CHUNK, B_DK), F32),
        pltpu.VMEM((B_HEADS, CHUNK, B_DK), F32),
        pltpu.VMEM((n_dec, B_WIDTH), F32),
        pltpu.VMEM((n_dec, B_WIDTH), F32),
        pltpu.VMEM((n_dec, B_WIDTH), F32),
        pltpu.VMEM((n_dec, B_WIDTH), F32),
        pltpu.VMEM((n_dec, A_WIDTH), F32),
        pltpu.VMEM((n_dec, B_WIDTH), F32),
        pltpu.VMEM((A_GROUPS, CHUNK, A_DIM), F32),
    ]
    return pl.pallas_call(
        functools.partial(_layer_kernel, tiles_per_seq=tiles_per_seq, n_tiles=n_tiles),
        grid=(n_tiles + 1,),
        in_specs=in_specs,
        out_specs=out_specs,
        out_shape=out_shape,
        scratch_shapes=scratch_shapes,
        compiler_params=pltpu.CompilerParams(
            dimension_semantics=("arbitrary",),
            vmem_limit_bytes=VMEM_LIMIT_BYTES),
        name="hybrid_layer",
    )(x, xs, s0, p["pre_mix_w"], p["ln_v_w"], p["ln_v_b"], p["w_s"], p["b_s"],
      p["gn_w"], p["post_mix_w"], p["pre_mlp_w"], p["post_mlp_w"],
      p["w_in"], p["w_out"], p["w_up"], p["w_down"])


def kernel(x_prompt, x_sample, state_ret, pre_mix_w, w_in, ln_v_w, ln_v_b, w_s, b_s, gn_w,
           w_out, post_mix_w, pre_mlp_w, w_up, w_down, post_mlp_w):
    depth = w_in.shape[0]
    seq = x_prompt.shape[1]
    assert x_sample.shape[1] == 1 and seq % ROW_TILE == 0

    yp, ys = x_prompt, x_sample
    sp_list, ss_list, vs_list = [], [], []
    for l in range(depth):
        p = {
            "pre_mix_w": pre_mix_w[l][None, :],
            "w_in": w_in[l],
            "ln_v_w": ln_v_w[l].reshape(1, A_WIDTH),
            "ln_v_b": ln_v_b[l].reshape(1, A_WIDTH),
            "w_s": w_s[l],
            "b_s": b_s[l],
            "gn_w": gn_w[l].reshape(1, B_WIDTH),
            "w_out": w_out[l],
            "post_mix_w": post_mix_w[l][None, :],
            "pre_mlp_w": pre_mlp_w[l][None, :],
            "w_up": w_up[l],
            "w_down": w_down[l],
            "post_mlp_w": post_mlp_w[l][None, :],
        }
        yp, sp, ys, ss, vs = _layer(yp, ys, state_ret[l], p)
        sp_list.append(sp)
        ss_list.append(ss)
        vs_list.append(vs)
    return (yp, ys, jnp.stack(sp_list, axis=0), jnp.stack(ss_list, axis=0), jnp.stack(vs_list, axis=0))
```
